```python
import math
import jax, jax.numpy as jnp
from jax import lax
import numpy as np

D_MODEL = 2048
BATCH = 8
SEQ = 8192
DEPTH = 4

S5_WIDTH = D_MODEL // 2
S5_GROUP = 16
S5_GROUPS = S5_WIDTH // S5_GROUP
S5_STATE = 64
S5_DT_MIN = 0.001
S5_DT_MAX = 0.1
DN_HEADS = 8
DN_DK = 128
DN_DV = 128
DN_QK_WIDTH = DN_HEADS * DN_DK
DN_V_WIDTH = DN_HEADS * DN_DV
DN_CONV = 4
DN_CHUNK = 64
DN_DT_MIN = 0.001
DN_DT_MAX = 0.1
FFN_DIM = 5632
FFN_CONV = 3
NORM_EPS = 1e-6

OFF_U = S5_WIDTH
OFF_QKV = OFF_U + 2 * DN_QK_WIDTH + DN_V_WIDTH
OFF_Z = OFF_QKV + DN_V_WIDTH
OFF_BETA = OFF_Z + DN_HEADS
OFF_ALPHA = OFF_BETA + DN_HEADS
OFF_GS = OFF_ALPHA + D_MODEL
N_IN = OFF_GS + D_MODEL
SPLITS = [OFF_U, OFF_QKV, OFF_Z, OFF_BETA, OFF_ALPHA, OFF_GS]

kernel_name = "hybrid_s5_gdn_convffn"


def rmsnorm(x, w):
    xf = x.astype(jnp.float32)
    y = xf * lax.rsqrt(jnp.mean(xf * xf, axis=-1, keepdims=True) + NORM_EPS) * w.astype(jnp.float32)
    return y.astype(x.dtype)


def l2norm(x):
    return x * lax.rsqrt(jnp.sum(x * x, axis=-1, keepdims=True) + NORM_EPS)


def causal_dwconv(x, w):
    K, C = w.shape
    return lax.conv_general_dilated(
        x, w[:, None, :].astype(x.dtype), window_strides=(1,), padding=[(K - 1, 0)],
        dimension_numbers=("NWC", "WIO", "NWC"), feature_group_count=C)


def s5_branch(u, log_dt, a_re, a_im, b_re, b_im, c_re, c_im, d):
    Bn, L, _ = u.shape
    f32 = jnp.float32
    uf = u.astype(f32)
    ug = uf.reshape(Bn, L, S5_GROUPS, S5_GROUP)
    lr, li = a_re.astype(f32), a_im.astype(f32)
    dt = jnp.exp(log_dt.astype(f32))[:, None]
    mag = jnp.exp(lr * dt)
    abar_re, abar_im = mag * jnp.cos(li * dt), mag * jnp.sin(li * dt)
    den = lr * lr + li * li
    nr, ni = abar_re - 1.0, abar_im
    coef_re = (nr * lr + ni * li) / den
    coef_im = (ni * lr - nr * li) / den
    br, bi = b_re.astype(f32), b_im.astype(f32)
    bbar_re = coef_re[..., None] * br - coef_im[..., None] * bi
    bbar_im = coef_re[..., None] * bi + coef_im[..., None] * br
    bu_re = jnp.einsum("gph,blgh->blgp", bbar_re, ug)
    bu_im = jnp.einsum("gph,blgh->blgp", bbar_im, ug)
    ar_t = jnp.broadcast_to(abar_re, bu_re.shape)
    ai_t = jnp.broadcast_to(abar_im, bu_re.shape)

    def combine(e1, e2):
        a1r, a1i, b1r, b1i = e1
        a2r, a2i, b2r, b2i = e2
        return (a2r * a1r - a2i * a1i,
                a2r * a1i + a2i * a1r,
                a2r * b1r - a2i * b1i + b2r,
                a2r * b1i + a2i * b1r + b2i)

    _, _, xr, xi = lax.associative_scan(combine, (ar_t, ai_t, bu_re, bu_im), axis=1)
    y = (jnp.einsum("ghp,blgp->blgh", c_re.astype(f32), xr)
         - jnp.einsum("ghp,blgp->blgh", c_im.astype(f32), xi))
    y = y.reshape(Bn, L, S5_WIDTH) + d.astype(f32) * uf
    return jax.nn.gelu(y, approximate=False).astype(u.dtype)


def chunk_gated_delta_rule(q, k, v, g, beta):
    Bn, L, H, DK = q.shape
    DV = v.shape[-1]
    N, C = L // DN_CHUNK, DN_CHUNK

    def chunks(t):
        return t.reshape(Bn, N, C, H, -1).transpose(1, 0, 3, 2, 4)

    qc, kc, vc = chunks(q), chunks(k), chunks(v)
    bc = beta.reshape(Bn, N, C, H).transpose(1, 0, 3, 2)
    gc = jnp.cumsum(g.reshape(Bn, N, C, H).transpose(1, 0, 3, 2), axis=-1)
    tril = jnp.tril(jnp.ones((C, C), dtype=bool))
    strict = jnp.tril(jnp.ones((C, C), dtype=bool), -1)
    decay = jnp.exp(jnp.where(tril, gc[..., :, None] - gc[..., None, :], -jnp.inf))
    k_beta = kc * bc[..., None]
    v_beta = vc * bc[..., None]
    lmat = jnp.where(strict, jnp.einsum("nbhcd,nbhsd->nbhcs", k_beta, kc) * decay, 0.0)
    eye = jnp.eye(C, dtype=q.dtype)
    rhs = jnp.concatenate([v_beta, k_beta * jnp.exp(gc)[..., None]], axis=-1)
    sol = lax.linalg.triangular_solve(eye + lmat, rhs, left_side=True, lower=True, unit_diagonal=True)
    u_c, w_c = sol[..., :DV], sol[..., DV:]
    attn = jnp.where(tril, jnp.einsum("nbhcd,nbhsd->nbhcs", qc, kc) * decay, 0.0)

    def step(S, inp):
        q_i, k_i, u_i, w_i, g_i, a_i = inp
        v_new = u_i - jnp.einsum("bhcd,bhdv->bhcv", w_i, S)
        o_i = (jnp.einsum("bhcd,bhdv->bhcv", q_i * jnp.exp(g_i)[..., None], S)
               + jnp.einsum("bhcs,bhsv->bhcv", a_i, v_new))
        g_last = g_i[..., -1]
        S = (S * jnp.exp(g_last)[..., None, None]
             + jnp.einsum("bhcd,bhcv->bhdv", k_i * jnp.exp(g_last[..., None] - g_i)[..., None], v_new))
        return S, o_i

    S0 = jnp.zeros((Bn, H, DK, DV), q.dtype)
    _, o = lax.scan(step, S0, (qc, kc, u_c, w_c, gc, attn))
    return o.transpose(1, 0, 3, 2, 4).reshape(Bn, L, H, DV)


def deltanet_branch(qkv, z, b_raw, a_raw, conv_w, a_log, dt_bias, norm_w):
    Bn, L, _ = qkv.shape
    f32 = jnp.float32
    qkv = jax.nn.silu(causal_dwconv(qkv, conv_w))
    q, k, v = jnp.split(qkv, [DN_QK_WIDTH, 2 * DN_QK_WIDTH], axis=-1)
    q = l2norm(q.reshape(Bn, L, DN_HEADS, DN_DK).astype(f32)) * (DN_DK ** -0.5)
    k = l2norm(k.reshape(Bn, L, DN_HEADS, DN_DK).astype(f32))
    v = v.reshape(Bn, L, DN_HEADS, DN_DV).astype(f32)
    beta = jax.nn.sigmoid(b_raw.astype(f32))
    g = -jnp.exp(a_log.astype(f32)) * jax.nn.softplus(a_raw.astype(f32) + dt_bias.astype(f32))
    o = chunk_gated_delta_rule(q, k, v, g, beta)
    o = rmsnorm(o, norm_w) * jax.nn.silu(z.reshape(Bn, L, DN_HEADS, DN_DV).astype(f32))
    return o.reshape(Bn, L, DN_V_WIDTH).astype(qkv.dtype)


def _fwd_setup_inputs(seed: int = 0) -> dict:
    key = jax.random.key(seed)
    ks = jax.random.split(key, 26)
    f32 = jnp.float32

    def nrm(k, shape, scale):
        return jax.random.normal(k, shape, f32) * scale

    G, P, HG = S5_GROUPS, S5_STATE, S5_GROUP
    x = nrm(ks[0], (BATCH, SEQ, D_MODEL), 1.0)
    mix_norm_w = 1.0 + nrm(ks[1], (DEPTH, D_MODEL), 0.02)
    w_in = nrm(ks[2], (DEPTH, D_MODEL, N_IN), D_MODEL ** -0.5)
    s5_log_dt = jax.random.uniform(ks[3], (DEPTH, G), f32, math.log(S5_DT_MIN), math.log(S5_DT_MAX))
    s5_a_re = -0.5 + nrm(ks[4], (DEPTH, G, P), 0.01)
    s5_a_im = math.pi * jnp.arange(P, dtype=f32) + nrm(ks[5], (DEPTH, G, P), 0.01)
    s5_b_re = nrm(ks[6], (DEPTH, G, P, HG), (2 * HG) ** -0.5)
    s5_b_im = nrm(ks[7], (DEPTH, G, P, HG), (2 * HG) ** -0.5)
    s5_c_re = nrm(ks[8], (DEPTH, G, HG, P), (2 * P) ** -0.5)
    s5_c_im = nrm(ks[9], (DEPTH, G, HG, P), (2 * P) ** -0.5)
    s5_d = nrm(ks[10], (DEPTH, S5_WIDTH), 1.0)
    s5_glu_w = nrm(ks[11], (DEPTH, S5_WIDTH, 2 * D_MODEL), S5_WIDTH ** -0.5)
    dn_conv_w = nrm(ks[12], (DEPTH, DN_CONV, 2 * DN_QK_WIDTH + DN_V_WIDTH), DN_CONV ** -0.5)
    dn_a_log = jnp.log(jax.random.uniform(ks[13], (DEPTH, DN_HEADS), f32, 1.0, 16.0))
    dn_dt = jnp.exp(jax.random.uniform(ks[14], (DEPTH, DN_HEADS), f32, math.log(DN_DT_MIN), math.log(DN_DT_MAX)))
    dn_dt_bias = dn_dt + jnp.log(-jnp.expm1(-dn_dt))
    dn_norm_w = 1.0 + nrm(ks[15], (DEPTH, DN_DV), 0.02)
    dn_proj_w = nrm(ks[16], (DEPTH, DN_V_WIDTH, D_MODEL), DN_V_WIDTH ** -0.5)
    w_out = nrm(ks[17], (DEPTH, D_MODEL, D_MODEL), D_MODEL ** -0.5)
    ffn_norm_w = 1.0 + nrm(ks[18], (DEPTH, D_MODEL), 0.02)
    ffn_up = nrm(ks[19], (DEPTH, D_MODEL, 2 * FFN_DIM), D_MODEL ** -0.5)
    ffn_conv_w = nrm(ks[20], (DEPTH, FFN_CONV, 2 * FFN_DIM), FFN_CONV ** -0.5)
    ffn_down = nrm(ks[21], (DEPTH, FFN_DIM, D_MODEL), FFN_DIM ** -0.5)
    final_norm_w = 1.0 + nrm(ks[22], (D_MODEL,), 0.02)
    return {"x": x, "mix_norm_w": mix_norm_w, "w_in": w_in, "s5_log_dt": s5_log_dt,
            "s5_a_re": s5_a_re, "s5_a_im": s5_a_im, "s5_b_re": s5_b_re, "s5_b_im": s5_b_im,
            "s5_c_re": s5_c_re, "s5_c_im": s5_c_im, "s5_d": s5_d, "s5_glu_w": s5_glu_w,
            "dn_conv_w": dn_conv_w, "dn_a_log": dn_a_log, "dn_dt_bias": dn_dt_bias,
            "dn_norm_w": dn_norm_w, "dn_proj_w": dn_proj_w, "w_out": w_out,
            "ffn_norm_w": ffn_norm_w, "ffn_up": ffn_up, "ffn_conv_w": ffn_conv_w,
            "ffn_down": ffn_down, "final_norm_w": final_norm_w}


def _fwd_reference(x, mix_norm_w, w_in, s5_log_dt, s5_a_re, s5_a_im, s5_b_re, s5_b_im,
              s5_c_re, s5_c_im, s5_d, s5_glu_w, dn_conv_w, dn_a_log, dn_dt_bias,
              dn_norm_w, dn_proj_w, w_out, ffn_norm_w, ffn_up, ffn_conv_w, ffn_down,
              final_norm_w):
    for l in range(DEPTH):
        h = rmsnorm(x, mix_norm_w[l])
        proj = jnp.einsum("bld,de->ble", h, w_in[l])
        u, qkv, z, b_raw, a_raw, g_s5, g_dn = jnp.split(proj, SPLITS, axis=-1)
        y_s5 = s5_branch(u, s5_log_dt[l], s5_a_re[l], s5_a_im[l], s5_b_re[l], s5_b_im[l],
                         s5_c_re[l], s5_c_im[l], s5_d[l])
        glu_a, glu_b = jnp.split(jnp.einsum("blc,ce->ble", y_s5, s5_glu_w[l]), 2, axis=-1)
        br_s5 = glu_a * jax.nn.sigmoid(glu_b)
        y_dn = deltanet_branch(qkv, z, b_raw, a_raw, dn_conv_w[l], dn_a_log[l], dn_dt_bias[l], dn_norm_w[l])
        br_dn = jnp.einsum("blc,cd->bld", y_dn, dn_proj_w[l])
        merged = jax.nn.sigmoid(g_s5) * br_s5 + jax.nn.sigmoid(g_dn) * br_dn
        x = x + jnp.einsum("bld,de->ble", merged, w_out[l])
        h = rmsnorm(x, ffn_norm_w[l])
        up = causal_dwconv(jnp.einsum("bld,df->blf", h, ffn_up[l]), ffn_conv_w[l])
        act, val = jnp.split(up, 2, axis=-1)
        x = x + jnp.einsum("blf,fd->bld", jax.nn.silu(act) * val, ffn_down[l])
    return rmsnorm(x, final_norm_w)


import jax as _jax
import jax.numpy as _jnp

TWIN_FORMAT = 'train_step'
FWD_PARAMS = ['x', 'mix_norm_w', 'w_in', 's5_log_dt', 's5_a_re', 's5_a_im', 's5_b_re', 's5_b_im', 's5_c_re', 's5_c_im', 's5_d', 's5_glu_w', 'dn_conv_w', 'dn_a_log', 'dn_dt_bias', 'dn_norm_w', 'dn_proj_w', 'w_out', 'ffn_norm_w', 'ffn_up', 'ffn_conv_w', 'ffn_down', 'final_norm_w']
TWIN_WEIGHTS = ['mix_norm_w', 'w_in', 's5_log_dt', 's5_a_re', 's5_a_im', 's5_b_re', 's5_b_im', 's5_c_re', 's5_c_im', 's5_d', 's5_glu_w', 'dn_conv_w', 'dn_a_log', 'dn_dt_bias', 'dn_norm_w', 'dn_proj_w', 'w_out', 'ffn_norm_w', 'ffn_up', 'ffn_conv_w', 'ffn_down', 'final_norm_w']
TWIN_DIFF_INPUT = 'x'
TWIN_INPUTS = ['x', 'mix_norm_w', 'w_in', 's5_log_dt', 's5_a_re', 's5_a_im', 's5_b_re', 's5_b_im', 's5_c_re', 's5_c_im', 's5_d', 's5_glu_w', 'dn_conv_w', 'dn_a_log', 'dn_dt_bias', 'dn_norm_w', 'dn_proj_w', 'w_out', 'ffn_norm_w', 'ffn_up', 'ffn_conv_w', 'ffn_down', 'final_norm_w', 'loss_target', 'm_mix_norm_w', 'm_w_in', 'm_s5_log_dt', 'm_s5_a_re', 'm_s5_a_im', 'm_s5_b_re', 'm_s5_b_im', 'm_s5_c_re', 'm_s5_c_im', 'm_s5_d', 'm_s5_glu_w', 'm_dn_conv_w', 'm_dn_a_log', 'm_dn_dt_bias', 'm_dn_norm_w', 'm_dn_proj_w', 'm_w_out', 'm_ffn_norm_w', 'm_ffn_up', 'm_ffn_conv_w', 'm_ffn_down', 'm_final_norm_w', 'v_mix_norm_w', 'v_w_in', 'v_s5_log_dt', 'v_s5_a_re', 'v_s5_a_im', 'v_s5_b_re', 'v_s5_b_im', 'v_s5_c_re', 'v_s5_c_im', 'v_s5_d', 'v_s5_glu_w', 'v_dn_conv_w', 'v_dn_a_log', 'v_dn_dt_bias', 'v_dn_norm_w', 'v_dn_proj_w', 'v_w_out', 'v_ffn_norm_w', 'v_ffn_up', 'v_ffn_conv_w', 'v_ffn_down', 'v_final_norm_w']
TWIN_OUTPUTS = ['loss', 'grad_x', 'grad_mix_norm_w', 'grad_w_in', 'grad_s5_log_dt', 'grad_s5_a_re', 'grad_s5_a_im', 'grad_s5_b_re', 'grad_s5_b_im', 'grad_s5_c_re', 'grad_s5_c_im', 'grad_s5_d', 'grad_s5_glu_w', 'grad_dn_conv_w', 'grad_dn_a_log', 'grad_dn_dt_bias', 'grad_dn_norm_w', 'grad_dn_proj_w', 'grad_w_out', 'grad_ffn_norm_w', 'grad_ffn_up', 'grad_ffn_conv_w', 'grad_ffn_down', 'grad_final_norm_w', 'delta_mix_norm_w', 'delta_w_in', 'delta_s5_log_dt', 'delta_s5_a_re', 'delta_s5_a_im', 'delta_s5_b_re', 'delta_s5_b_im', 'delta_s5_c_re', 'delta_s5_c_im', 'delta_s5_d', 'delta_s5_glu_w', 'delta_dn_conv_w', 'delta_dn_a_log', 'delta_dn_dt_bias', 'delta_dn_norm_w', 'delta_dn_proj_w', 'delta_w_out', 'delta_ffn_norm_w', 'delta_ffn_up', 'delta_ffn_conv_w', 'delta_ffn_down', 'delta_final_norm_w', 'new_m_mix_norm_w', 'new_m_w_in', 'new_m_s5_log_dt', 'new_m_s5_a_re', 'new_m_s5_a_im', 'new_m_s5_b_re', 'new_m_s5_b_im', 'new_m_s5_c_re', 'new_m_s5_c_im', 'new_m_s5_d', 'new_m_s5_glu_w', 'new_m_dn_conv_w', 'new_m_dn_a_log', 'new_m_dn_dt_bias', 'new_m_dn_norm_w', 'new_m_dn_proj_w', 'new_m_w_out', 'new_m_ffn_norm_w', 'new_m_ffn_up', 'new_m_ffn_conv_w', 'new_m_ffn_down', 'new_m_final_norm_w', 'new_v_mix_norm_w', 'new_v_w_in', 'new_v_s5_log_dt', 'new_v_s5_a_re', 'new_v_s5_a_im', 'new_v_s5_b_re', 'new_v_s5_b_im', 'new_v_s5_c_re', 'new_v_s5_c_im', 'new_v_s5_d', 'new_v_s5_glu_w', 'new_v_dn_conv_w', 'new_v_dn_a_log', 'new_v_dn_dt_bias', 'new_v_dn_norm_w', 'new_v_dn_proj_w', 'new_v_w_out', 'new_v_ffn_norm_w', 'new_v_ffn_up', 'new_v_ffn_conv_w', 'new_v_ffn_down', 'new_v_final_norm_w']
TWIN_LEAF_KINDS = {'loss': 'loss', 'grad_x': 'grad_x', 'grad_mix_norm_w': 'grad_w', 'grad_w_in': 'grad_w', 'grad_s5_log_dt': 'grad_w', 'grad_s5_a_re': 'grad_w', 'grad_s5_a_im': 'grad_w', 'grad_s5_b_re': 'grad_w', 'grad_s5_b_im': 'grad_w', 'grad_s5_c_re': 'grad_w', 'grad_s5_c_im': 'grad_w', 'grad_s5_d': 'grad_w', 'grad_s5_glu_w': 'grad_w', 'grad_dn_conv_w': 'grad_w', 'grad_dn_a_log': 'grad_w', 'grad_dn_dt_bias': 'grad_w', 'grad_dn_norm_w': 'grad_w', 'grad_dn_proj_w': 'grad_w', 'grad_w_out': 'grad_w', 'grad_ffn_norm_w': 'grad_w', 'grad_ffn_up': 'grad_w', 'grad_ffn_conv_w': 'grad_w', 'grad_ffn_down': 'grad_w', 'grad_final_norm_w': 'grad_w', 'delta_mix_norm_w': 'delta_w', 'delta_w_in': 'delta_w', 'delta_s5_log_dt': 'delta_w', 'delta_s5_a_re': 'delta_w', 'delta_s5_a_im': 'delta_w', 'delta_s5_b_re': 'delta_w', 'delta_s5_b_im': 'delta_w', 'delta_s5_c_re': 'delta_w', 'delta_s5_c_im': 'delta_w', 'delta_s5_d': 'delta_w', 'delta_s5_glu_w': 'delta_w', 'delta_dn_conv_w': 'delta_w', 'delta_dn_a_log': 'delta_w', 'delta_dn_dt_bias': 'delta_w', 'delta_dn_norm_w': 'delta_w', 'delta_dn_proj_w': 'delta_w', 'delta_w_out': 'delta_w', 'delta_ffn_norm_w': 'delta_w', 'delta_ffn_up': 'delta_w', 'delta_ffn_conv_w': 'delta_w', 'delta_ffn_down': 'delta_w', 'delta_final_norm_w': 'delta_w', 'new_m_mix_norm_w': 'new_m', 'new_m_w_in': 'new_m', 'new_m_s5_log_dt': 'new_m', 'new_m_s5_a_re': 'new_m', 'new_m_s5_a_im': 'new_m', 'new_m_s5_b_re': 'new_m', 'new_m_s5_b_im': 'new_m', 'new_m_s5_c_re': 'new_m', 'new_m_s5_c_im': 'new_m', 'new_m_s5_d': 'new_m', 'new_m_s5_glu_w': 'new_m', 'new_m_dn_conv_w': 'new_m', 'new_m_dn_a_log': 'new_m', 'new_m_dn_dt_bias': 'new_m', 'new_m_dn_norm_w': 'new_m', 'new_m_dn_proj_w': 'new_m', 'new_m_w_out': 'new_m', 'new_m_ffn_norm_w': 'new_m', 'new_m_ffn_up': 'new_m', 'new_m_ffn_conv_w': 'new_m', 'new_m_ffn_down': 'new_m', 'new_m_final_norm_w': 'new_m', 'new_v_mix_norm_w': 'new_v', 'new_v_w_in': 'new_v', 'new_v_s5_log_dt': 'new_v', 'new_v_s5_a_re': 'new_v', 'new_v_s5_a_im': 'new_v', 'new_v_s5_b_re': 'new_v', 'new_v_s5_b_im': 'new_v', 'new_v_s5_c_re': 'new_v', 'new_v_s5_c_im': 'new_v', 'new_v_s5_d': 'new_v', 'new_v_s5_glu_w': 'new_v', 'new_v_dn_conv_w': 'new_v', 'new_v_dn_a_log': 'new_v', 'new_v_dn_dt_bias': 'new_v', 'new_v_dn_norm_w': 'new_v', 'new_v_dn_proj_w': 'new_v', 'new_v_w_out': 'new_v', 'new_v_ffn_norm_w': 'new_v', 'new_v_ffn_up': 'new_v', 'new_v_ffn_conv_w': 'new_v', 'new_v_ffn_down': 'new_v', 'new_v_final_norm_w': 'new_v'}


def _forward(args):
    return _fwd_reference(*[args[k] for k in FWD_PARAMS])


def _output_shape():
    def fwd():
        inp = _fwd_setup_inputs(0)
        return _fwd_reference(*[inp[k] for k in FWD_PARAMS])
    out = _jax.eval_shape(fwd)
    return out.shape, out.dtype

N_MICROBATCH = 1
ADAM_LR = 0.001
ADAM_B1 = 0.9
ADAM_B2 = 0.999
ADAM_EPS = 1e-08
ADAM_WD = 0.01
ADAM_STEP = 10
PER_EXAMPLE_BATCH_AXIS = {'x': 0, 'loss_target': 0}
SHARED_INPUTS = []
_WEIGHT_DTYPES = {'mix_norm_w': _jnp.float32, 'w_in': _jnp.float32, 's5_log_dt': _jnp.float32, 's5_a_re': _jnp.float32, 's5_a_im': _jnp.float32, 's5_b_re': _jnp.float32, 's5_b_im': _jnp.float32, 's5_c_re': _jnp.float32, 's5_c_im': _jnp.float32, 's5_d': _jnp.float32, 's5_glu_w': _jnp.float32, 'dn_conv_w': _jnp.float32, 'dn_a_log': _jnp.float32, 'dn_dt_bias': _jnp.float32, 'dn_norm_w': _jnp.float32, 'dn_proj_w': _jnp.float32, 'w_out': _jnp.float32, 'ffn_norm_w': _jnp.float32, 'ffn_up': _jnp.float32, 'ffn_conv_w': _jnp.float32, 'ffn_down': _jnp.float32, 'final_norm_w': _jnp.float32}
MOMENT_SCALE = {'mix_norm_w': 7.893013e-02, 'w_in': 3.716553e-02, 's5_log_dt': 1.483477e+00, 's5_a_re': 2.132157e-03, 's5_a_im': 2.018081e-03, 's5_b_re': 1.285771e-03, 's5_b_im': 1.296636e-03, 's5_c_re': 2.595752e-03, 's5_c_im': 2.570941e-03, 's5_d': 3.899003e-02, 's5_glu_w': 1.921382e-02, 'dn_conv_w': 4.624299e-02, 'dn_a_log': 2.027847e-01, 'dn_dt_bias': 1.979419e-01, 'dn_norm_w': 1.594119e-01, 'dn_proj_w': 4.236288e-02, 'w_out': 4.990011e-02, 'ffn_norm_w': 9.531367e-02, 'ffn_up': 4.070993e-02, 'ffn_conv_w': 4.032625e-02, 'ffn_down': 6.641349e-02, 'final_norm_w': 3.198452e+01}


def _to_microbatches(a, axis):
    t = _jnp.moveaxis(a, axis, 0)
    t = t.reshape((N_MICROBATCH, t.shape[0] // N_MICROBATCH) + t.shape[1:])
    return _jnp.moveaxis(t, 1, axis + 1)


def setup_inputs(seed: int = 0) -> dict:
    inp = _fwd_setup_inputs(seed)
    key = _jax.random.fold_in(_jax.random.key(seed), 7919)
    shape, _ = _output_shape()
    out = dict(inp)
    out["loss_target"] = _jax.random.normal(_jax.random.fold_in(key, 0), shape, _jnp.float32)
    for i, name in enumerate(TWIN_WEIGHTS):
        w = inp[name].astype(_jnp.float32)
        if MOMENT_SCALE is None:
            s = _jnp.sqrt(_jnp.mean(_jnp.square(w)) + 1e-30)
        else:
            s = MOMENT_SCALE[name]
        km, kv = _jax.random.split(_jax.random.fold_in(key, i + 1))
        out[name] = w
        out["m_" + name] = s * _jax.random.normal(km, w.shape, _jnp.float32)
        out["v_" + name] = (s * s) * _jax.random.uniform(kv, w.shape, _jnp.float32, 0.5, 1.5)
    if N_MICROBATCH > 1:
        for name, axis in PER_EXAMPLE_BATCH_AXIS.items():
            out[name] = _to_microbatches(out[name], axis)
    return {'x': out['x'], 'mix_norm_w': out['mix_norm_w'], 'w_in': out['w_in'], 's5_log_dt': out['s5_log_dt'], 's5_a_re': out['s5_a_re'], 's5_a_im': out['s5_a_im'], 's5_b_re': out['s5_b_re'], 's5_b_im': out['s5_b_im'], 's5_c_re': out['s5_c_re'], 's5_c_im': out['s5_c_im'], 's5_d': out['s5_d'], 's5_glu_w': out['s5_glu_w'], 'dn_conv_w': out['dn_conv_w'], 'dn_a_log': out['dn_a_log'], 'dn_dt_bias': out['dn_dt_bias'], 'dn_norm_w': out['dn_norm_w'], 'dn_proj_w': out['dn_proj_w'], 'w_out': out['w_out'], 'ffn_norm_w': out['ffn_norm_w'], 'ffn_up': out['ffn_up'], 'ffn_conv_w': out['ffn_conv_w'], 'ffn_down': out['ffn_down'], 'final_norm_w': out['final_norm_w'], 'loss_target': out['loss_target'], 'm_mix_norm_w': out['m_mix_norm_w'], 'm_w_in': out['m_w_in'], 'm_s5_log_dt': out['m_s5_log_dt'], 'm_s5_a_re': out['m_s5_a_re'], 'm_s5_a_im': out['m_s5_a_im'], 'm_s5_b_re': out['m_s5_b_re'], 'm_s5_b_im': out['m_s5_b_im'], 'm_s5_c_re': out['m_s5_c_re'], 'm_s5_c_im': out['m_s5_c_im'], 'm_s5_d': out['m_s5_d'], 'm_s5_glu_w': out['m_s5_glu_w'], 'm_dn_conv_w': out['m_dn_conv_w'], 'm_dn_a_log': out['m_dn_a_log'], 'm_dn_dt_bias': out['m_dn_dt_bias'], 'm_dn_norm_w': out['m_dn_norm_w'], 'm_dn_proj_w': out['m_dn_proj_w'], 'm_w_out': out['m_w_out'], 'm_ffn_norm_w': out['m_ffn_norm_w'], 'm_ffn_up': out['m_ffn_up'], 'm_ffn_conv_w': out['m_ffn_conv_w'], 'm_ffn_down': out['m_ffn_down'], 'm_final_norm_w': out['m_final_norm_w'], 'v_mix_norm_w': out['v_mix_norm_w'], 'v_w_in': out['v_w_in'], 'v_s5_log_dt': out['v_s5_log_dt'], 'v_s5_a_re': out['v_s5_a_re'], 'v_s5_a_im': out['v_s5_a_im'], 'v_s5_b_re': out['v_s5_b_re'], 'v_s5_b_im': out['v_s5_b_im'], 'v_s5_c_re': out['v_s5_c_re'], 'v_s5_c_im': out['v_s5_c_im'], 'v_s5_d': out['v_s5_d'], 'v_s5_glu_w': out['v_s5_glu_w'], 'v_dn_conv_w': out['v_dn_conv_w'], 'v_dn_a_log': out['v_dn_a_log'], 'v_dn_dt_bias': out['v_dn_dt_bias'], 'v_dn_norm_w': out['v_dn_norm_w'], 'v_dn_proj_w': out['v_dn_proj_w'], 'v_w_out': out['v_w_out'], 'v_ffn_norm_w': out['v_ffn_norm_w'], 'v_ffn_up': out['v_ffn_up'], 'v_ffn_conv_w': out['v_ffn_conv_w'], 'v_ffn_down': out['v_ffn_down'], 'v_final_norm_w': out['v_final_norm_w']}


def _loss(weights, diff, rest, loss_target):
    with _jax.named_scope("forward"):
        args = {**rest, TWIN_DIFF_INPUT: diff, **{k: w.astype(_WEIGHT_DTYPES[k]) for k, w in weights.items()}}
        y = _forward(args)
    with _jax.named_scope("loss_head"):
        err = _jnp.square(y.astype(_jnp.float32) - loss_target)
        return 0.5 * _jnp.sum(_jnp.mean(err, axis=-1)) if err.ndim else 0.5 * err


def _adamw(w, g, m, v):
    m = ADAM_B1 * m + (1.0 - ADAM_B1) * g
    v = ADAM_B2 * v + (1.0 - ADAM_B2) * _jnp.square(g)
    m_hat = m / (1.0 - ADAM_B1 ** ADAM_STEP)
    v_hat = v / (1.0 - ADAM_B2 ** ADAM_STEP)
    delta = -ADAM_LR * (m_hat / (_jnp.sqrt(v_hat) + ADAM_EPS) + ADAM_WD * w)
    return delta, m, v


def reference(x, mix_norm_w, w_in, s5_log_dt, s5_a_re, s5_a_im, s5_b_re, s5_b_im, s5_c_re, s5_c_im, s5_d, s5_glu_w, dn_conv_w, dn_a_log, dn_dt_bias, dn_norm_w, dn_proj_w, w_out, ffn_norm_w, ffn_up, ffn_conv_w, ffn_down, final_norm_w, loss_target, m_mix_norm_w, m_w_in, m_s5_log_dt, m_s5_a_re, m_s5_a_im, m_s5_b_re, m_s5_b_im, m_s5_c_re, m_s5_c_im, m_s5_d, m_s5_glu_w, m_dn_conv_w, m_dn_a_log, m_dn_dt_bias, m_dn_norm_w, m_dn_proj_w, m_w_out, m_ffn_norm_w, m_ffn_up, m_ffn_conv_w, m_ffn_down, m_final_norm_w, v_mix_norm_w, v_w_in, v_s5_log_dt, v_s5_a_re, v_s5_a_im, v_s5_b_re, v_s5_b_im, v_s5_c_re, v_s5_c_im, v_s5_d, v_s5_glu_w, v_dn_conv_w, v_dn_a_log, v_dn_dt_bias, v_dn_norm_w, v_dn_proj_w, v_w_out, v_ffn_norm_w, v_ffn_up, v_ffn_conv_w, v_ffn_down, v_final_norm_w):
    given = dict(x=x, mix_norm_w=mix_norm_w, w_in=w_in, s5_log_dt=s5_log_dt, s5_a_re=s5_a_re, s5_a_im=s5_a_im, s5_b_re=s5_b_re, s5_b_im=s5_b_im, s5_c_re=s5_c_re, s5_c_im=s5_c_im, s5_d=s5_d, s5_glu_w=s5_glu_w, dn_conv_w=dn_conv_w, dn_a_log=dn_a_log, dn_dt_bias=dn_dt_bias, dn_norm_w=dn_norm_w, dn_proj_w=dn_proj_w, w_out=w_out, ffn_norm_w=ffn_norm_w, ffn_up=ffn_up, ffn_conv_w=ffn_conv_w, ffn_down=ffn_down, final_norm_w=final_norm_w, loss_target=loss_target, m_mix_norm_w=m_mix_norm_w, m_w_in=m_w_in, m_s5_log_dt=m_s5_log_dt, m_s5_a_re=m_s5_a_re, m_s5_a_im=m_s5_a_im, m_s5_b_re=m_s5_b_re, m_s5_b_im=m_s5_b_im, m_s5_c_re=m_s5_c_re, m_s5_c_im=m_s5_c_im, m_s5_d=m_s5_d, m_s5_glu_w=m_s5_glu_w, m_dn_conv_w=m_dn_conv_w, m_dn_a_log=m_dn_a_log, m_dn_dt_bias=m_dn_dt_bias, m_dn_norm_w=m_dn_norm_w, m_dn_proj_w=m_dn_proj_w, m_w_out=m_w_out, m_ffn_norm_w=m_ffn_norm_w, m_ffn_up=m_ffn_up, m_ffn_conv_w=m_ffn_conv_w, m_ffn_down=m_ffn_down, m_final_norm_w=m_final_norm_w, v_mix_norm_w=v_mix_norm_w, v_w_in=v_w_in, v_s5_log_dt=v_s5_log_dt, v_s5_a_re=v_s5_a_re, v_s5_a_im=v_s5_a_im, v_s5_b_re=v_s5_b_re, v_s5_b_im=v_s5_b_im, v_s5_c_re=v_s5_c_re, v_s5_c_im=v_s5_c_im, v_s5_d=v_s5_d, v_s5_glu_w=v_s5_glu_w, v_dn_conv_w=v_dn_conv_w, v_dn_a_log=v_dn_a_log, v_dn_dt_bias=v_dn_dt_bias, v_dn_norm_w=v_dn_norm_w, v_dn_proj_w=v_dn_proj_w, v_w_out=v_w_out, v_ffn_norm_w=v_ffn_norm_w, v_ffn_up=v_ffn_up, v_ffn_conv_w=v_ffn_conv_w, v_ffn_down=v_ffn_down, v_final_norm_w=v_final_norm_w)
    weights = {n: given[n] for n in TWIN_WEIGHTS}
    shared = {n: given[n] for n in SHARED_INPUTS}
    per_example = {n: given[n] for n in ['x']}
    grad_fn = _jax.value_and_grad(_loss, argnums=(0, 1))

    def one_microbatch(ex, loss_target):
        ex = dict(ex)
        diff = ex.pop(TWIN_DIFF_INPUT)
        return grad_fn(weights, diff, {**shared, **ex}, loss_target)

    if N_MICROBATCH == 1:
        loss, (grad_w, grad_x) = one_microbatch(per_example, given["loss_target"])
    else:
        def body(carry, xs):
            loss_sum, grad_sum = carry
            l_k, (gw_k, gx_k) = one_microbatch(xs[0], xs[1])
            with _jax.named_scope("update"):
                return (loss_sum + l_k, _jax.tree.map(_jnp.add, grad_sum, gw_k)), gx_k

        init = (_jnp.zeros((), _jnp.float32), _jax.tree.map(_jnp.zeros_like, weights))
        (loss, grad_w), grad_x = _jax.lax.scan(body, init, (per_example, given["loss_target"]))
    with _jax.named_scope("update"):
        delta_w, new_m, new_v = {}, {}, {}
        for n in TWIN_WEIGHTS:
            delta_w[n], new_m[n], new_v[n] = _adamw(weights[n], grad_w[n], given["m_" + n], given["v_" + n])
    return (loss, grad_x, *[grad_w[n] for n in TWIN_WEIGHTS], *[delta_w[n] for n in TWIN_WEIGHTS],
            *[new_m[n] for n in TWIN_WEIGHTS], *[new_v[n] for n in TWIN_WEIGHTS])
```

```python
import functools
import math

import jax
import jax.numpy as jnp
from jax import lax
from jax.experimental import pallas as pl
from jax.experimental.pallas import tpu as pltpu

F32 = jnp.float32
BF16 = jnp.bfloat16

D_MODEL = 2048
DEPTH = 4
S5_WIDTH = 1024
S5_GROUP = 16
S5_GROUPS = 64
S5_STATE = 64
DN_HEADS = 8
DN_DK = 128
DN_QKV = 3072
DN_CONV = 4
DN_CHUNK = 64
FFN_DIM = 5632
FFN_CONV = 3
NORM_EPS = 1e-6
N_IN = 9232
OFF_Z = 4096
OFF_BA = 5120
OFF_GS = 5136
N_MAIN = 9216
MAIN_Z = 4096
MAIN_GS = 5120
MAIN_GD = 7168

ADAM_LR = 0.001
ADAM_B1 = 0.9
ADAM_B2 = 0.999
ADAM_EPS = 1e-08
ADAM_WD = 0.01
ADAM_STEP = 10

N_DEV = 8
LANES = 128
SUBLANES = 8
VMEM_LIMIT_BYTES = 48 * 1024 * 1024

S5_SLABS = 8
S5_SLAB_STATE = 512
S5_TB = 256
DN_TB = 512


def _cparams(sem):
    return pltpu.CompilerParams(dimension_semantics=sem, vmem_limit_bytes=VMEM_LIMIT_BYTES)


def _dot(a, b, dims, precision=None):
    return lax.dot_general(a, b, (dims, ((), ())), precision=precision, preferred_element_type=F32)


_NN = ((1,), (0,))
_NT = ((1,), (1,))
_TN = ((0,), (0,))


def _mm(a, b, mode, out_dtype, name, res=None, tm=1024, tn=1024, tk=2048):
    if mode == "nn":
        (m, k), (_, n) = a.shape, b.shape
    elif mode == "nt":
        (m, k), (n, _) = a.shape, b.shape
    else:
        (k, m), (_, n) = a.shape, b.shape
    tm, tn, tk = min(tm, m), min(tn, n), min(tk, k)
    assert m % tm == 0 and n % tn == 0 and k % tk == 0, (name, a.shape, b.shape)
    nk = k // tk
    if mode == "tn":
        a_spec = pl.BlockSpec((tk, tm), lambda i, j, kk: (kk, i))
    else:
        a_spec = pl.BlockSpec((tm, tk), lambda i, j, kk: (i, kk))
    if mode == "nt":
        b_spec = pl.BlockSpec((tn, tk), lambda i, j, kk: (j, kk))
    else:
        b_spec = pl.BlockSpec((tk, tn), lambda i, j, kk: (kk, j))
    dims = {"nn": _NN, "nt": _NT, "tn": _TN}[mode]
    o_spec = pl.BlockSpec((tm, tn), lambda i, j, kk: (i, j))
    has_res = res is not None

    def body(*refs):
        if has_res:
            a_ref, b_ref, r_ref, o_ref, acc = refs
        else:
            a_ref, b_ref, o_ref, acc = refs
        p = _dot(a_ref[...], b_ref[...], dims)

        def finish(total):
            if has_res:
                total = total + r_ref[...]
            o_ref[...] = total.astype(out_dtype)

        if nk == 1:
            finish(p)
        else:
            kk = pl.program_id(2)

            @pl.when(kk == 0)
            def _():
                acc[...] = p

            @pl.when(jnp.logical_and(kk > 0, kk < nk - 1))
            def _():
                acc[...] += p

            @pl.when(kk == nk - 1)
            def _():
                finish(acc[...] + p)

    in_specs = [a_spec, b_spec] + ([o_spec] if has_res else [])
    args = (a, b) + ((res,) if has_res else ())
    return pl.pallas_call(
        body, name=name, grid=(m // tm, n // tn, nk), in_specs=in_specs, out_specs=o_spec,
        out_shape=jax.ShapeDtypeStruct((m, n), out_dtype),
        scratch_shapes=[pltpu.VMEM((tm, tn) if nk > 1 else (SUBLANES, LANES), F32)],
        compiler_params=_cparams(("parallel", "parallel", "arbitrary")),
    )(*args)


def _row_spec(tb, width, cw, off):
    if cw is None:
        return pl.BlockSpec((tb, width), lambda j, i: (i, 0))
    return pl.BlockSpec((tb, cw), lambda j, i: (i, j + off))


def _par_spec(rows, width, cw, off):
    if cw is None:
        return pl.BlockSpec((rows, width), lambda j, i: (0, 0))
    return pl.BlockSpec((rows, cw), lambda j, i: (0, j + off))


def _tile_fwd(name, fn, tiled, params, outs, rows, tb, ncol=1):
    tb = min(tb, rows)
    nt, npar = len(tiled), len(params)

    def body(*refs):
        vals = [r[...] for r in refs[:nt + npar]]
        res = fn(*vals)
        for o_ref, r in zip(refs[nt + npar:], res):
            o_ref[...] = r.astype(o_ref.dtype)

    in_specs = [_row_spec(tb, a.shape[1], cw, off) for a, cw, off in tiled]
    in_specs += [_par_spec(a.shape[0], a.shape[1], cw, off) for a, cw, off in params]
    out_specs = [_row_spec(tb, w, cw, 0) for w, cw, _ in outs]
    out_shape = [jax.ShapeDtypeStruct((rows, w), dt) for w, _, dt in outs]
    return pl.pallas_call(
        body, name=name, grid=(ncol, rows // tb), in_specs=in_specs, out_specs=out_specs, out_shape=out_shape,
        compiler_params=_cparams(("parallel", "parallel")),
    )(*[a for a, _, _ in tiled], *[a for a, _, _ in params])


def _tile_bwd(name, fn, tiled, params, cots, gdtypes, rows, tb, ncol=1):
    tb = min(tb, rows)
    nt, npar, nc = len(tiled), len(params), len(cots)
    want = [i for i, g in enumerate(gdtypes) if g is not None]

    def body(*refs):
        vals = [r[...] for r in refs[:nt + npar]]
        cot_refs = refs[nt + npar:nt + npar + nc]
        g_refs = refs[nt + npar + nc:nt + npar + nc + len(want)]
        p_refs = refs[nt + npar + nc + len(want):]
        _, vjp = jax.vjp(fn, *vals)
        grads = vjp(tuple(c[...].astype(F32) for c in cot_refs))
        for g_ref, i in zip(g_refs, want):
            g_ref[...] = grads[i].astype(g_ref.dtype)
        jcol, irow = pl.program_id(0), pl.program_id(1)
        for p_ref, g, (_, cw, _) in zip(p_refs, grads[nt:], params):
            first = (irow == 0) if cw is not None else jnp.logical_and(irow == 0, jcol == 0)

            @pl.when(first)
            def _():
                p_ref[...] = g

            @pl.when(jnp.logical_not(first))
            def _():
                p_ref[...] += g

    in_specs = [_row_spec(tb, a.shape[1], cw, off) for a, cw, off in tiled]
    in_specs += [_par_spec(a.shape[0], a.shape[1], cw, off) for a, cw, off in params]
    in_specs += [_row_spec(tb, a.shape[1], cw, off) for a, cw, off in cots]
    out_specs, out_shape = [], []
    for i in want:
        a, cw, _ = tiled[i]
        width = a.shape[1] if cw is None else ncol * cw
        out_specs.append(_row_spec(tb, width, cw, 0))
        out_shape.append(jax.ShapeDtypeStruct((rows, width), gdtypes[i]))
    for a, cw, _ in params:
        width = a.shape[1] if cw is None else ncol * cw
        out_specs.append(_par_spec(a.shape[0], width, cw, 0))
        out_shape.append(jax.ShapeDtypeStruct((a.shape[0], width), F32))
    res = pl.pallas_call(
        body, name=name, grid=(ncol, rows // tb), in_specs=in_specs, out_specs=out_specs, out_shape=out_shape,
        compiler_params=_cparams(("arbitrary", "arbitrary")),
    )(*[a for a, _, _ in tiled], *[a for a, _, _ in params], *[a for a, _, _ in cots])
    return res[:len(want)], res[len(want):]


def _sigmoid(x):
    return 1.0 / (1.0 + jnp.exp(-x))


def _silu(x):
    return x * _sigmoid(x)


def _softplus(x):
    return jnp.maximum(x, 0.0) + jnp.log1p(jnp.exp(-jnp.abs(x)))


def _f_rms(x, w):
    return (x * lax.rsqrt(jnp.mean(x * x, axis=-1, keepdims=True) + NORM_EPS) * w,)


def _f_merge(glu_a, glu_b, br_dn, gs, gd):
    return (_sigmoid(gs) * (glu_a * _sigmoid(glu_b)) + _sigmoid(gd) * br_dn,)


def _f_ffn_gate(act, val):
    return (_silu(act) * val,)


def _l2n(x):
    return x * lax.rsqrt(jnp.sum(x * x, axis=-1, keepdims=True) + NORM_EPS)


def _f_dn_q(c):
    return (_l2n(_silu(c)) * (DN_DK ** -0.5),)


def _f_dn_k(c):
    return (_l2n(_silu(c)),)


def _f_dn_v(c):
    return (_silu(c),)


def _f_dn_gates(ba, a_log, dt_bias):
    col = lax.broadcasted_iota(jnp.int32, ba.shape, 1)
    beta = _sigmoid(ba)
    g = -jnp.exp(a_log) * _softplus(ba + dt_bias)
    return (jnp.where(col < DN_HEADS, beta, jnp.where(col < 2 * DN_HEADS, g, 0.0)),)


def _f_dn_post(o, z, w):
    return (_f_rms(o, w)[0] * _silu(z),)


def _shift_down(x, halo, s, tb):
    if s == 0:
        return x
    y = pltpu.roll(x, s, 0)
    row8 = lax.broadcasted_iota(jnp.int32, halo.shape, 0)
    top = jnp.where(row8 < s, pltpu.roll(halo, s, 0), y[0:SUBLANES])
    if tb == SUBLANES:
        return top
    return jnp.concatenate([top, y[SUBLANES:]], axis=0)


def _shift_up(x, halo, s, tb):
    if s == 0:
        return x
    y = pltpu.roll(x, tb - s, 0)
    row8 = lax.broadcasted_iota(jnp.int32, halo.shape, 0)
    bot = jnp.where(row8 >= SUBLANES - s, pltpu.roll(halo, SUBLANES - s, 0), y[tb - SUBLANES:])
    if tb == SUBLANES:
        return bot
    return jnp.concatenate([y[:tb - SUBLANES], bot], axis=0)


def _conv_fwd(x, w, kw, name, x_off=0, width=None, cw=512, tb=512):
    rows = x.shape[0]
    width = w.shape[1] if width is None else width
    tb = min(tb, rows)
    nb = tb // SUBLANES

    def body(x_ref, h_ref, w_ref, o_ref):
        i = pl.program_id(1)
        xv = x_ref[...]
        halo = jnp.where(i > 0, h_ref[...], 0.0)
        acc = w_ref[kw - 1:kw, :] * xv
        for s in range(1, kw):
            acc = acc + w_ref[kw - 1 - s:kw - s, :] * _shift_down(xv, halo, s, tb)
        o_ref[...] = acc

    return pl.pallas_call(
        body, name=name, grid=(width // cw, rows // tb),
        in_specs=[pl.BlockSpec((tb, cw), lambda j, i: (i, j + x_off)),
                  pl.BlockSpec((SUBLANES, cw), lambda j, i: (jnp.maximum(i * nb - 1, 0), j + x_off)),
                  pl.BlockSpec((kw, cw), lambda j, i: (0, j))],
        out_specs=pl.BlockSpec((tb, cw), lambda j, i: (i, j)),
        out_shape=jax.ShapeDtypeStruct((rows, width), F32),
        compiler_params=_cparams(("parallel", "parallel")),
    )(x, x, w)


def _conv_bwd(x, w, dout, kw, name, x_off=0, cw=512, tb=512):
    rows, width = dout.shape
    tb = min(tb, rows)
    nb = tb // SUBLANES
    nrow = rows // tb

    def body(x_ref, h_ref, w_ref, d_ref, dn_ref, dx_ref, dw_ref):
        i = pl.program_id(1)
        xv, dv = x_ref[...], d_ref[...]
        halo = jnp.where(i > 0, h_ref[...], 0.0)
        nxt = jnp.where(i < nrow - 1, dn_ref[...], 0.0)

        @pl.when(i == 0)
        def _():
            dw_ref[...] = jnp.zeros_like(dw_ref)

        acc = w_ref[kw - 1:kw, :] * dv
        dw_ref[kw - 1:kw, :] += jnp.sum(dv * xv, axis=0, keepdims=True)
        for s in range(1, kw):
            acc = acc + w_ref[kw - 1 - s:kw - s, :] * _shift_up(dv, nxt, s, tb)
            dw_ref[kw - 1 - s:kw - s, :] += jnp.sum(dv * _shift_down(xv, halo, s, tb), axis=0, keepdims=True)
        dx_ref[...] = acc

    return pl.pallas_call(
        body, name=name, grid=(width // cw, nrow),
        in_specs=[pl.BlockSpec((tb, cw), lambda j, i: (i, j + x_off)),
                  pl.BlockSpec((SUBLANES, cw), lambda j, i: (jnp.maximum(i * nb - 1, 0), j + x_off)),
                  pl.BlockSpec((kw, cw), lambda j, i: (0, j)),
                  pl.BlockSpec((tb, cw), lambda j, i: (i, j)),
                  pl.BlockSpec((SUBLANES, cw), lambda j, i: (jnp.minimum((i + 1) * nb, rows // SUBLANES - 1), j))],
        out_specs=[pl.BlockSpec((tb, cw), lambda j, i: (i, j)),
                   pl.BlockSpec((SUBLANES, cw), lambda j, i: (0, j))],
        out_shape=[jax.ShapeDtypeStruct((rows, width), F32), jax.ShapeDtypeStruct((SUBLANES, width), F32)],
        compiler_params=_cparams(("parallel", "arbitrary")),
    )(x, x, w, dout, dout)


def _f_s5_disc(log_dt, a_re, a_im, bt_re, bt_im):
    dt = jnp.exp(log_dt)
    mag = jnp.exp(a_re * dt)
    abar_re, abar_im = mag * jnp.cos(a_im * dt), mag * jnp.sin(a_im * dt)
    den = a_re * a_re + a_im * a_im
    nr, ni = abar_re - 1.0, abar_im
    coef_re = (nr * a_re + ni * a_im) / den
    coef_im = (ni * a_re - nr * a_im) / den
    bbar_re = coef_re[None] * bt_re - coef_im[None] * bt_im
    bbar_im = coef_re[None] * bt_im + coef_im[None] * bt_re
    return abar_re, abar_im, bbar_re, bbar_im


def _s5_disc_fwd(log_dt, a_re, a_im, bt_re, bt_im, name):
    def body(*refs):
        res = _f_s5_disc(*[r[...] for r in refs[:5]])
        for o_ref, r in zip(refs[5:], res):
            o_ref[...] = r

    shp = [a_re, a_re, bt_re, bt_re]
    return pl.pallas_call(body, name=name, out_shape=[jax.ShapeDtypeStruct(s.shape, F32) for s in shp])(
        log_dt, a_re, a_im, bt_re, bt_im)


def _s5_disc_bwd(log_dt, a_re, a_im, bt_re, bt_im, cots, name):
    def body(*refs):
        _, vjp = jax.vjp(_f_s5_disc, *[r[...] for r in refs[:5]])
        grads = vjp(tuple(r[...] for r in refs[5:9]))
        for o_ref, g in zip(refs[9:], grads):
            o_ref[...] = g

    ins = [log_dt, a_re, a_im, bt_re, bt_im]
    return pl.pallas_call(body, name=name, out_shape=[jax.ShapeDtypeStruct(s.shape, F32) for s in ins])(*ins, *cots)


def _cmul(ar, ai, br, bi):
    return ar * br - ai * bi, ar * bi + ai * br


def _scan_fwd(br, bi, ar, ai, tb):
    row = lax.broadcasted_iota(jnp.int32, (tb, 1), 0)
    pr, pi = ar, ai
    d = 1
    while d < tb:
        keep = row >= d
        sr = jnp.where(keep, pltpu.roll(br, d, 0), 0.0)
        si = jnp.where(keep, pltpu.roll(bi, d, 0), 0.0)
        mr, mi = _cmul(pr, pi, sr, si)
        br, bi = br + mr, bi + mi
        pr, pi = _cmul(pr, pi, pr, pi)
        d *= 2
    return br, bi


def _scan_rev(br, bi, ar, ai, tb):
    row = lax.broadcasted_iota(jnp.int32, (tb, 1), 0)
    pr, pi = ar, ai
    d = 1
    while d < tb:
        keep = row < tb - d
        sr = jnp.where(keep, pltpu.roll(br, tb - d, 0), 0.0)
        si = jnp.where(keep, pltpu.roll(bi, tb - d, 0), 0.0)
        mr, mi = _cmul(pr, pi, sr, si)
        br, bi = br + mr, bi + mi
        pr, pi = _cmul(pr, pi, pr, pi)
        d *= 2
    return br, bi


_INV_SQRT2 = 1.0 / math.sqrt(2.0)
_INV_SQRT2PI = 1.0 / math.sqrt(2.0 * math.pi)


def _gelu(y):
    return 0.5 * y * (1.0 + lax.erf(y * _INV_SQRT2))


def _gelu_grad(y):
    return 0.5 * (1.0 + lax.erf(y * _INV_SQRT2)) + y * jnp.exp(-0.5 * y * y) * _INV_SQRT2PI


def _s5_states(u, bd_re, bd_im, ar, ai, cr, ci, tb):
    ub = u.astype(BF16)
    br = _dot(ub, bd_re, _NN)
    bi = _dot(ub, bd_im, _NN)
    row = lax.broadcasted_iota(jnp.int32, (tb, 1), 0)
    jr, ji = _cmul(ar, ai, cr, ci)
    br = br + jnp.where(row == 0, jr, 0.0)
    bi = bi + jnp.where(row == 0, ji, 0.0)
    return _scan_fwd(br, bi, ar, ai, tb)


def _s5_specs(tb, u_off):
    slab3 = lambda r, c: pl.BlockSpec((None, r, c), lambda s, t: (s, 0, 0))
    return dict(
        u=lambda tmap: pl.BlockSpec((tb, LANES), lambda s, t: (tmap(t), s + u_off)),
        bd=slab3(LANES, S5_SLAB_STATE), cd=slab3(S5_SLAB_STATE, LANES), a=slab3(1, S5_SLAB_STATE),
        d=pl.BlockSpec((1, LANES), lambda s, t: (0, s)))


def _s5_fwd(proj, bd_re, bd_im, cd_re, cd_im, a_re, a_im, d, name):
    rows = proj.shape[0]
    tb = min(S5_TB, rows)
    nt = rows // tb
    sp = _s5_specs(tb, 0)

    def body(u_ref, bdr, bdi, cdr, cdi, ar_ref, ai_ref, d_ref, y_ref, sr_ref, si_ref, cr_s, ci_s, xr_s, xi_s):
        t = pl.program_id(1)

        @pl.when(t == 0)
        def _():
            cr_s[...] = jnp.zeros_like(cr_s)
            ci_s[...] = jnp.zeros_like(ci_s)

        cr, ci = cr_s[...], ci_s[...]
        sr_ref[...] = cr
        si_ref[...] = ci
        u = u_ref[...]
        xr, xi = _s5_states(u, bdr[...], bdi[...], ar_ref[...], ai_ref[...], cr, ci, tb)
        xr_s[...] = xr
        xi_s[...] = xi
        cr_s[...] = xr_s[tb - 1:tb, :]
        ci_s[...] = xi_s[tb - 1:tb, :]
        y = _dot(xr.astype(BF16), cdr[...], _NN) - _dot(xi.astype(BF16), cdi[...], _NN) + d_ref[...] * u
        y_ref[...] = _gelu(y).astype(BF16)

    st_spec = pl.BlockSpec((None, None, 1, S5_SLAB_STATE), lambda s, t: (s, t, 0, 0))
    st_shape = jax.ShapeDtypeStruct((S5_SLABS, nt, 1, S5_SLAB_STATE), F32)
    return pl.pallas_call(
        body, name=name, grid=(S5_SLABS, nt),
        in_specs=[sp["u"](lambda t: t), sp["bd"], sp["bd"], sp["cd"], sp["cd"], sp["a"], sp["a"], sp["d"]],
        out_specs=[pl.BlockSpec((tb, LANES), lambda s, t: (t, s)), st_spec, st_spec],
        out_shape=[jax.ShapeDtypeStruct((rows, S5_WIDTH), BF16), st_shape, st_shape],
        scratch_shapes=[pltpu.VMEM((1, S5_SLAB_STATE), F32)] * 2 + [pltpu.VMEM((tb, S5_SLAB_STATE), F32)] * 2,
        compiler_params=_cparams(("parallel", "arbitrary")),
    )(proj, bd_re, bd_im, cd_re, cd_im, a_re, a_im, d)


def _s5_bwd(proj, dy, st_re, st_im, bd_re, bd_im, cd_re, cd_im, a_re, a_im, d, name):
    rows = proj.shape[0]
    tb = min(S5_TB, rows)
    nt = rows // tb
    sp = _s5_specs(tb, 0)
    rev = lambda t: nt - 1 - t

    def body(u_ref, dy_ref, sr_ref, si_ref, bdr, bdi, cdr, cdi, ar_ref, ai_ref, d_ref,
             du_ref, gbr, gbi, gcr, gci, gar, gai, gd_ref, lr_s, li_s, tr_s, ti_s):
        t = pl.program_id(1)

        @pl.when(t == 0)
        def _():
            lr_s[...] = jnp.zeros_like(lr_s)
            li_s[...] = jnp.zeros_like(li_s)
            for r in (gbr, gbi, gcr, gci, gar, gai, gd_ref):
                r[...] = jnp.zeros_like(r)

        u = u_ref[...]
        ar, ai = ar_ref[...], ai_ref[...]
        cr, ci = sr_ref[...], si_ref[...]
        xr, xi = _s5_states(u, bdr[...], bdi[...], ar, ai, cr, ci, tb)
        xrb, xib = xr.astype(BF16), xi.astype(BF16)
        ypre = _dot(xrb, cdr[...], _NN) - _dot(xib, cdi[...], _NN) + d_ref[...] * u
        dyp = dy_ref[...] * _gelu_grad(ypre)
        dypb = dyp.astype(BF16)
        gd_ref[...] += jnp.sum(dyp * u, axis=0, keepdims=True)
        gcr[...] += _dot(xrb, dypb, _TN)
        gci[...] -= _dot(xib, dypb, _TN)
        gxr = _dot(dypb, cdr[...], _NT)
        gxi = -_dot(dypb, cdi[...], _NT)
        row = lax.broadcasted_iota(jnp.int32, (tb, 1), 0)
        jr, ji = _cmul(ar, -ai, lr_s[...], li_s[...])
        gxr = gxr + jnp.where(row == tb - 1, jr, 0.0)
        gxi = gxi + jnp.where(row == tb - 1, ji, 0.0)
        lr, li = _scan_rev(gxr, gxi, ar, -ai, tb)
        tr_s[...] = lr
        ti_s[...] = li
        lr_s[...] = tr_s[0:1, :]
        li_s[...] = ti_s[0:1, :]
        lrb, lib = lr.astype(BF16), li.astype(BF16)
        du_ref[...] = _dot(lrb, bdr[...], _NT) + _dot(lib, bdi[...], _NT) + d_ref[...] * dyp
        ub = u.astype(BF16)
        gbr[...] += _dot(ub, lrb, _TN)
        gbi[...] += _dot(ub, lib, _TN)
        xpr = jnp.where(row == 0, cr, pltpu.roll(xr, 1, 0))
        xpi = jnp.where(row == 0, ci, pltpu.roll(xi, 1, 0))
        gar[...] += jnp.sum(lr * xpr + li * xpi, axis=0, keepdims=True)
        gai[...] += jnp.sum(li * xpr - lr * xpi, axis=0, keepdims=True)

    st_spec = pl.BlockSpec((None, None, 1, S5_SLAB_STATE), lambda s, t: (s, rev(t), 0, 0))
    slab = lambda r, c: pl.BlockSpec((None, r, c), lambda s, t: (s, 0, 0))
    return pl.pallas_call(
        body, name=name, grid=(S5_SLABS, nt),
        in_specs=[sp["u"](rev), pl.BlockSpec((tb, LANES), lambda s, t: (rev(t), s)), st_spec, st_spec,
                  sp["bd"], sp["bd"], sp["cd"], sp["cd"], sp["a"], sp["a"], sp["d"]],
        out_specs=[pl.BlockSpec((tb, LANES), lambda s, t: (rev(t), s)),
                   slab(LANES, S5_SLAB_STATE), slab(LANES, S5_SLAB_STATE),
                   slab(S5_SLAB_STATE, LANES), slab(S5_SLAB_STATE, LANES),
                   slab(1, S5_SLAB_STATE), slab(1, S5_SLAB_STATE),
                   pl.BlockSpec((1, LANES), lambda s, t: (0, s))],
        out_shape=[jax.ShapeDtypeStruct((rows, S5_WIDTH), F32),
                   jax.ShapeDtypeStruct((S5_SLABS, LANES, S5_SLAB_STATE), F32),
                   jax.ShapeDtypeStruct((S5_SLABS, LANES, S5_SLAB_STATE), F32),
                   jax.ShapeDtypeStruct((S5_SLABS, S5_SLAB_STATE, LANES), F32),
                   jax.ShapeDtypeStruct((S5_SLABS, S5_SLAB_STATE, LANES), F32),
                   jax.ShapeDtypeStruct((S5_SLABS, 1, S5_SLAB_STATE), F32),
                   jax.ShapeDtypeStruct((S5_SLABS, 1, S5_SLAB_STATE), F32),
                   jax.ShapeDtypeStruct((1, S5_WIDTH), F32)],
        scratch_shapes=[pltpu.VMEM((1, S5_SLAB_STATE), F32)] * 2 + [pltpu.VMEM((tb, S5_SLAB_STATE), F32)] * 2,
        compiler_params=_cparams(("parallel", "arbitrary")),
    )(proj, dy, st_re, st_im, bd_re, bd_im, cd_re, cd_im, a_re, a_im, d)


@functools.partial(jax.custom_vjp, nondiff_argnums=(2,))
def _bdot(a, b, dims):
    return _dot(a.astype(BF16), b.astype(BF16), dims)


def _bdot_fwd(a, b, dims):
    return _bdot(a, b, dims), (a, b)


def _bdot_bwd(dims, res, ct):
    a, b = res
    if dims == _NN:
        return _bdot(ct, b, _NT), _bdot(a, ct, _TN)
    if dims == _NT:
        return _bdot(ct, b, _NN), _bdot(ct, a, _TN)
    return _bdot(b, ct, _NT), _bdot(a, ct, _NN)


_bdot.defvjp(_bdot_fwd, _bdot_bwd)

_HI = lax.Precision.HIGHEST


def _unit_lower_inverse(lm, n):
    r = lax.broadcasted_iota(jnp.int32, (n, n), 0)
    c = lax.broadcasted_iota(jnp.int32, (n, n), 1)
    p = jnp.where(r == c, 1.0, 0.0) - lm
    power = lm
    steps = int(math.log2(n)) - 1
    for _ in range(steps):
        power = _dot(power, power, _NN, _HI)
        p = p + _dot(p, power, _NN, _HI)
    return p


def _dn_chunk(q, k, v, g, b, s):
    c = q.shape[0]
    r = lax.broadcasted_iota(jnp.int32, (c, c), 0)
    col = lax.broadcasted_iota(jnp.int32, (c, c), 1)
    tril = r >= col
    strict = r > col
    ones_tril = jnp.where(tril, 1.0, 0.0)
    gl = jnp.broadcast_to(g, (c, LANES))
    gc = _dot(ones_tril, gl, _NN, _HI)
    gtot = jnp.sum(gl, axis=0, keepdims=True)
    gdiff = _dot(ones_tril, jnp.where(strict, jnp.broadcast_to(g, (c, c)), 0.0), _NN, _HI)
    decay = jnp.where(tril, jnp.exp(jnp.where(tril, gdiff, 0.0)), 0.0)
    kb = k * b
    vb = v * b
    lmat = jnp.where(strict, _bdot(kb, k, _NT) * decay, 0.0)
    tinv = _unit_lower_inverse(lmat, c)
    u = _dot(tinv, vb, _NN, _HI)
    w = _dot(tinv, kb * jnp.exp(gc), _NN, _HI)
    attn = jnp.where(tril, _bdot(q, k, _NT) * decay, 0.0)
    v_new = u - _bdot(w, s, _NN)
    o = _bdot(q * jnp.exp(gc), s, _NN) + _bdot(attn, v_new, _NN)
    s_new = s * jnp.exp(gtot) + _bdot(k * jnp.exp(gtot - gc), v_new, _TN)
    return o, s_new


def _dn_specs(tb, nt, tmap, q_off):
    blk = lambda off: pl.BlockSpec((tb, LANES), lambda h, t: (tmap(t), h + off))
    colv = pl.BlockSpec((None, tb, 1), lambda h, t: (h, tmap(t), 0))
    st = pl.BlockSpec((None, tb // DN_CHUNK, DN_DK, DN_DK), lambda h, t: (h, tmap(t), 0, 0))
    return blk, colv, st


def _dn_fwd(qkv, gcol, bcol, name):
    rows = qkv.shape[0]
    tb = min(DN_TB, rows)
    nt = rows // tb
    nch = tb // DN_CHUNK
    blk, colv, st = _dn_specs(tb, nt, lambda t: t, 0)

    def body(q_ref, k_ref, v_ref, g_ref, b_ref, o_ref, st_ref, s_scr):
        @pl.when(pl.program_id(1) == 0)
        def _():
            s_scr[...] = jnp.zeros_like(s_scr)

        def chunk(ci, carry):
            rs = pl.ds(pl.multiple_of(ci * DN_CHUNK, DN_CHUNK), DN_CHUNK)
            s_in = s_scr[...]
            st_ref[ci] = s_in
            o, s_new = _dn_chunk(q_ref[rs, :], k_ref[rs, :], v_ref[rs, :], g_ref[rs, :], b_ref[rs, :], s_in)
            o_ref[rs, :] = o
            s_scr[...] = s_new
            return carry

        lax.fori_loop(0, nch, chunk, 0)

    return pl.pallas_call(
        body, name=name, grid=(DN_HEADS, nt),
        in_specs=[blk(0), blk(DN_HEADS), blk(2 * DN_HEADS), colv, colv],
        out_specs=[blk(0), st],
        out_shape=[jax.ShapeDtypeStruct((rows, DN_HEADS * DN_DK), F32),
                   jax.ShapeDtypeStruct((DN_HEADS, rows // DN_CHUNK, DN_DK, DN_DK), F32)],
        scratch_shapes=[pltpu.VMEM((DN_DK, DN_DK), F32)],
        compiler_params=_cparams(("parallel", "arbitrary")),
    )(qkv, qkv, qkv, gcol, bcol)


def _dn_bwd(qkv, gcol, bcol, states, do, name):
    rows = qkv.shape[0]
    tb = min(DN_TB, rows)
    nt = rows // tb
    nch = tb // DN_CHUNK
    blk, colv, st = _dn_specs(tb, nt, lambda t: nt - 1 - t, 0)

    def body(q_ref, k_ref, v_ref, g_ref, b_ref, st_ref, do_ref, dq_ref, dk_ref, dv_ref, dg_ref, db_ref, ds_scr):
        @pl.when(pl.program_id(1) == 0)
        def _():
            ds_scr[...] = jnp.zeros_like(ds_scr)

        def chunk(cj, carry):
            ci = nch - 1 - cj
            rs = pl.ds(pl.multiple_of(ci * DN_CHUNK, DN_CHUNK), DN_CHUNK)
            args = (q_ref[rs, :], k_ref[rs, :], v_ref[rs, :], g_ref[rs, :], b_ref[rs, :], st_ref[ci])
            _, vjp = jax.vjp(_dn_chunk, *args)
            dq, dk, dv, dg, db, ds = vjp((do_ref[rs, :], ds_scr[...]))
            dq_ref[rs, :] = dq
            dk_ref[rs, :] = dk
            dv_ref[rs, :] = dv
            dg_ref[rs, :] = dg
            db_ref[rs, :] = db
            ds_scr[...] = ds
            return carry

        lax.fori_loop(0, nch, chunk, 0)

    wide = jax.ShapeDtypeStruct((rows, DN_HEADS * DN_DK), F32)
    narrow = jax.ShapeDtypeStruct((DN_HEADS, rows, 1), F32)
    return pl.pallas_call(
        body, name=name, grid=(DN_HEADS, nt),
        in_specs=[blk(0), blk(DN_HEADS), blk(2 * DN_HEADS), colv, colv, st, blk(0)],
        out_specs=[blk(0), blk(0), blk(0), colv, colv],
        out_shape=[wide, wide, wide, narrow, narrow],
        scratch_shapes=[pltpu.VMEM((DN_DK, DN_DK), F32)],
        compiler_params=_cparams(("parallel", "arbitrary")),
    )(qkv, qkv, qkv, gcol, bcol, states, do)


def _final_loss(x, w, target, name, tb=256):
    rows, width = x.shape
    tb = min(tb, rows)

    def body(x_ref, w_ref, t_ref, dx_ref, dw_ref, loss_ref):
        i = pl.program_id(0)
        (y,), vjp = jax.vjp(_f_rms, x_ref[...], w_ref[...])
        err = y - t_ref[...]
        part = 0.5 * jnp.sum(jnp.mean(err * err, axis=-1, keepdims=True), axis=0, keepdims=True)
        dx, dw = vjp((err * (1.0 / width),))
        dx_ref[...] = dx

        @pl.when(i == 0)
        def _():
            dw_ref[...] = dw
            loss_ref[...] = jnp.broadcast_to(part, loss_ref.shape)

        @pl.when(i > 0)
        def _():
            dw_ref[...] += dw
            loss_ref[...] += jnp.broadcast_to(part, loss_ref.shape)

    row = pl.BlockSpec((tb, width), lambda i: (i, 0))
    par = pl.BlockSpec((1, width), lambda i: (0, 0))
    return pl.pallas_call(
        body, name=name, grid=(rows // tb,), in_specs=[row, par, row],
        out_specs=[row, par, pl.BlockSpec((SUBLANES, LANES), lambda i: (0, 0))],
        out_shape=[jax.ShapeDtypeStruct((rows, width), F32), jax.ShapeDtypeStruct((1, width), F32),
                   jax.ShapeDtypeStruct((SUBLANES, LANES), F32)],
        compiler_params=_cparams(("arbitrary",)),
    )(x, w, target)


def _exchange(arrs, scatter, name):
    n = len(arrs)
    blocks = [a.shape[1:] if scatter else a.shape for a in arrs]

    def body(*refs):
        ins, outs = refs[:n], refs[n:2 * n]
        send_sems, recv_sems, loc_sems = refs[2 * n:]
        x, y, c = lax.axis_index("x"), lax.axis_index("y"), lax.axis_index("c")
        me = 4 * x + 2 * y + c
        peers = []
        for mask in range(1, N_DEV):
            px = 1 - x if mask & 4 else x
            py = 1 - y if mask & 2 else y
            pc = 1 - c if mask & 1 else c
            peers.append(((px, py, pc), 4 * px + 2 * py + pc))
        started = []
        for i in range(n):
            own = ins[i].at[me] if scatter else ins[i]
            loc = pltpu.make_async_copy(own, outs[i].at[me], loc_sems.at[i])
            loc.start()
            started.append(loc)
        sends, recvs = [], []
        for i in range(n):
            for kk, (dev, idx) in enumerate(peers):
                src = ins[i].at[idx] if scatter else ins[i]
                snd = pltpu.make_async_remote_copy(
                    src_ref=src, dst_ref=outs[i].at[me], send_sem=send_sems.at[i, kk], recv_sem=recv_sems.at[i, kk],
                    device_id=dev, device_id_type=pl.DeviceIdType.MESH)
                snd.start()
                sends.append(snd)
                recvs.append(pltpu.make_async_remote_copy(
                    src_ref=src, dst_ref=outs[i].at[idx], send_sem=send_sems.at[i, kk], recv_sem=recv_sems.at[i, kk],
                    device_id=dev, device_id_type=pl.DeviceIdType.MESH))
        for snd, rcv in zip(sends, recvs):
            snd.wait_send()
            rcv.wait_recv()
        for loc in started:
            loc.wait()

    any_spec = pl.BlockSpec(memory_space=pl.ANY)
    return pl.pallas_call(
        body, name=name, in_specs=[any_spec] * n, out_specs=[any_spec] * n,
        out_shape=[jax.ShapeDtypeStruct((N_DEV,) + tuple(b), a.dtype) for a, b in zip(arrs, blocks)],
        scratch_shapes=[pltpu.SemaphoreType.DMA((n, N_DEV - 1)), pltpu.SemaphoreType.DMA((n, N_DEV - 1)),
                        pltpu.SemaphoreType.DMA((n,))],
    )(*arrs)


def _adamw_math(w, g, m, v):
    m = ADAM_B1 * m + (1.0 - ADAM_B1) * g
    v = ADAM_B2 * v + (1.0 - ADAM_B2) * (g * g)
    m_hat = m / (1.0 - ADAM_B1 ** ADAM_STEP)
    v_hat = v / (1.0 - ADAM_B2 ** ADAM_STEP)
    delta = -ADAM_LR * (m_hat / (jnp.sqrt(v_hat) + ADAM_EPS) + ADAM_WD * w)
    return delta, m, v


def _row_tile(rows, cols, budget=128 * 1024):
    if rows * cols <= budget or rows % SUBLANES:
        return rows
    best = SUBLANES
    for t in range(SUBLANES, rows + 1, SUBLANES):
        if rows % t == 0 and t * cols <= budget:
            best = t
    return best


def _reduce_adamw(parts, w, m, v, name):
    rows, cols = w.shape
    tb = _row_tile(rows, cols)

    def body(p_ref, w_ref, m_ref, v_ref, g_ref, d_ref, nm_ref, nv_ref):
        g = p_ref[0].astype(F32)
        for s in range(1, N_DEV):
            g = g + p_ref[s].astype(F32)
        delta, nm, nv = _adamw_math(w_ref[...], g, m_ref[...], v_ref[...])
        g_ref[...] = g
        d_ref[...] = delta
        nm_ref[...] = nm
        nv_ref[...] = nv

    spec = pl.BlockSpec((tb, cols), lambda i: (i, 0))
    shp = jax.ShapeDtypeStruct((rows, cols), F32)
    return pl.pallas_call(
        body, name=name, grid=(rows // tb,),
        in_specs=[pl.BlockSpec((N_DEV, tb, cols), lambda i: (0, i, 0)), spec, spec, spec],
        out_specs=[spec] * 4, out_shape=[shp] * 4,
        compiler_params=_cparams(("parallel",)),
    )(parts, w, m, v)


def _sum_parts(parts, name):
    _, rows, cols = parts.shape
    tb = _row_tile(rows, cols)

    def body(p_ref, g_ref):
        g = p_ref[0]
        for s in range(1, N_DEV):
            g = g + p_ref[s]
        g_ref[...] = g

    return pl.pallas_call(
        body, name=name, grid=(rows // tb,),
        in_specs=[pl.BlockSpec((N_DEV, tb, cols), lambda i: (0, i, 0))],
        out_specs=pl.BlockSpec((tb, cols), lambda i: (i, 0)),
        out_shape=jax.ShapeDtypeStruct((rows, cols), F32),
        compiler_params=_cparams(("parallel",)),
    )(parts)


def _adamw(g, w, m, v, name):
    rows, cols = w.shape
    tb = _row_tile(rows, cols)

    def body(g_ref, w_ref, m_ref, v_ref, d_ref, nm_ref, nv_ref):
        delta, nm, nv = _adamw_math(w_ref[...], g_ref[...], m_ref[...], v_ref[...])
        d_ref[...] = delta
        nm_ref[...] = nm
        nv_ref[...] = nv

    spec = pl.BlockSpec((tb, cols), lambda i: (i, 0))
    shp = jax.ShapeDtypeStruct((rows, cols), F32)
    return pl.pallas_call(
        body, name=name, grid=(rows // tb,), in_specs=[spec] * 4, out_specs=[spec] * 3, out_shape=[shp] * 3,
        compiler_params=_cparams(("parallel",)),
    )(g, w, m, v)


def _pack(arrs):
    flat = jnp.concatenate([a.reshape(-1) for a in arrs])
    pad = (-flat.shape[0]) % (SUBLANES * LANES)
    return jnp.pad(flat, (0, pad)).reshape(-1, LANES)


def _unpack(packed, shapes):
    flat = packed.reshape(-1)
    out, off = [], 0
    for s in shapes:
        n = math.prod(s)
        out.append(flat[off:off + n].reshape(s))
        off += n
    return out


def _block_diag_b(bt):
    bb = bt.transpose(1, 0, 2).reshape(S5_SLABS, 8, S5_GROUP, S5_STATE)
    eye = jnp.eye(8, dtype=bt.dtype)
    return (bb[:, :, :, None, :] * eye[None, :, None, :, None]).reshape(S5_SLABS, LANES, S5_SLAB_STATE)


def _block_diag_b_grad(g):
    g5 = g.reshape(S5_SLABS, 8, S5_GROUP, 8, S5_STATE)
    diag = jnp.stack([g5[:, a, :, a, :] for a in range(8)], axis=1)
    return diag.reshape(S5_GROUPS, S5_GROUP, S5_STATE).transpose(1, 0, 2)


def _block_diag_c(cw):
    cc = cw.reshape(S5_SLABS, 8, S5_GROUP, S5_STATE).transpose(0, 1, 3, 2)
    eye = jnp.eye(8, dtype=cw.dtype)
    return (cc[:, :, :, None, :] * eye[None, :, None, :, None]).reshape(S5_SLABS, S5_SLAB_STATE, LANES)


def _block_diag_c_grad(g):
    g5 = g.reshape(S5_SLABS, 8, S5_STATE, 8, S5_GROUP)
    diag = jnp.stack([g5[:, a, :, a, :] for a in range(8)], axis=1)
    return diag.transpose(0, 1, 3, 2).reshape(S5_GROUPS, S5_GROUP, S5_STATE)


def _cols_full(gathered):
    _, k, n = gathered.shape
    return gathered.transpose(1, 0, 2).reshape(k, N_DEV * n)


def _cols_split(full):
    k, n8 = full.shape
    return full.reshape(k, N_DEV, n8 // N_DEV).transpose(1, 0, 2)


SMALL_NAMES = ("mix_norm_w", "s5_log_dt", "s5_a_re", "s5_a_im", "s5_b_re", "s5_b_im", "s5_c_re", "s5_c_im", "s5_d",
               "dn_a_log", "dn_dt_bias", "dn_norm_w", "ffn_norm_w", "final_norm_w")
CONV_NAMES = ("dn_conv_w", "ffn_conv_w")
BIG_NAMES = ("w_in", "s5_glu_w", "dn_proj_w", "w_out", "ffn_up", "ffn_down")
ROW_SHARDED = ("w_out", "ffn_down")
WEIGHT_ORDER = ("mix_norm_w", "w_in", "s5_log_dt", "s5_a_re", "s5_a_im", "s5_b_re", "s5_b_im", "s5_c_re", "s5_c_im",
                "s5_d", "s5_glu_w", "dn_conv_w", "dn_a_log", "dn_dt_bias", "dn_norm_w", "dn_proj_w", "w_out",
                "ffn_norm_w", "ffn_up", "ffn_conv_w", "ffn_down", "final_norm_w")


def _layer_forward(l, x, wts, sm):
    rows = x.shape[0]
    sv = {"x0": x}
    nm = f"l{l}_"
    mixw = sm["mix_norm_w"][l][None]
    (h,) = _tile_fwd(nm + "mix_norm", _f_rms, [(x, None, 0)], [(mixw, None, 0)], [(D_MODEL, None, BF16)], rows, 256)
    proj = _mm(h, wts["w_in_main"], "nn", F32, nm + "proj")
    ba = _mm(h, wts["w_in_ba"], "nn", F32, nm + "proj_ba")
    sv.update(h=h, proj=proj, ba=ba)
    disc = _s5_disc_fwd(sm["s5_log_dt"][l][:, None], sm["s5_a_re"][l], sm["s5_a_im"][l],
                        sm["s5_b_re"][l].transpose(2, 0, 1), sm["s5_b_im"][l].transpose(2, 0, 1), nm + "s5_disc")
    abar_re, abar_im, bbar_re, bbar_im = disc
    s5p = dict(
        bd_re=_block_diag_b(bbar_re).astype(BF16), bd_im=_block_diag_b(bbar_im).astype(BF16),
        cd_re=_block_diag_c(sm["s5_c_re"][l]).astype(BF16), cd_im=_block_diag_c(sm["s5_c_im"][l]).astype(BF16),
        a_re=abar_re.reshape(S5_SLABS, 1, S5_SLAB_STATE), a_im=abar_im.reshape(S5_SLABS, 1, S5_SLAB_STATE),
        d=sm["s5_d"][l][None])
    y_s5, st_re, st_im = _s5_fwd(proj, s5p["bd_re"], s5p["bd_im"], s5p["cd_re"], s5p["cd_im"],
                                 s5p["a_re"], s5p["a_im"], s5p["d"], nm + "s5_scan")
    glu = _mm(y_s5, wts["s5_glu_w"], "nn", F32, nm + "glu")
    sv.update(s5p=s5p, y_s5=y_s5, st_re=st_re, st_im=st_im, glu=glu)
    conv = _conv_fwd(proj, wts["dn_conv_w"], DN_CONV, nm + "dn_conv", x_off=2, width=DN_QKV, cw=512)
    q, = _tile_fwd(nm + "dn_q", _f_dn_q, [(conv, LANES, 0)], [], [(1024, LANES, F32)], rows, 512, ncol=8)
    k, = _tile_fwd(nm + "dn_k", _f_dn_k, [(conv, LANES, 8)], [], [(1024, LANES, F32)], rows, 512, ncol=8)
    v, = _tile_fwd(nm + "dn_v", _f_dn_v, [(conv, LANES, 16)], [], [(1024, LANES, F32)], rows, 512, ncol=8)
    qkv = jnp.concatenate([q, k, v], axis=1)
    pad8 = lambda a: jnp.pad(a[None], ((0, 0), (DN_HEADS, LANES - 2 * DN_HEADS)))
    alog, dtb = pad8(sm["dn_a_log"][l]), pad8(sm["dn_dt_bias"][l])
    bg, = _tile_fwd(nm + "dn_gates", _f_dn_gates, [(ba, None, 0)], [(alog, None, 0), (dtb, None, 0)],
                    [(LANES, None, F32)], rows, 512)
    bcol = bg[:, 0:DN_HEADS].T[:, :, None]
    gcol = bg[:, DN_HEADS:2 * DN_HEADS].T[:, :, None]
    o, states = _dn_fwd(qkv, gcol, bcol, nm + "dn_chunk")
    dnw = sm["dn_norm_w"][l][None]
    y_dn, = _tile_fwd(nm + "dn_post", _f_dn_post, [(o, LANES, 0), (proj, LANES, MAIN_Z // LANES)], [(dnw, None, 0)],
                      [(1024, LANES, BF16)], rows, 512, ncol=8)
    br_dn = _mm(y_dn, wts["dn_proj_w"], "nn", F32, nm + "dn_proj")
    sv.update(conv=conv, qkv=qkv, alog=alog, dtb=dtb, bcol=bcol, gcol=gcol, o=o, states=states, dnw=dnw,
              y_dn=y_dn, br_dn=br_dn)
    cw = 512
    merged, = _tile_fwd(nm + "merge", _f_merge,
                        [(glu, cw, 0), (glu, cw, D_MODEL // cw), (br_dn, cw, 0),
                         (proj, cw, MAIN_GS // cw), (proj, cw, MAIN_GD // cw)], [],
                        [(D_MODEL, cw, BF16)], rows, 512, ncol=D_MODEL // cw)
    x1 = _mm(merged, wts["w_out"], "nn", F32, nm + "w_out", res=x)
    sv.update(merged=merged, x1=x1)
    ffw = sm["ffn_norm_w"][l][None]
    (h2,) = _tile_fwd(nm + "ffn_norm", _f_rms, [(x1, None, 0)], [(ffw, None, 0)], [(D_MODEL, None, BF16)], rows, 256)
    up = _mm(h2, wts["ffn_up"], "nn", F32, nm + "ffn_up")
    upc = _conv_fwd(up, wts["ffn_conv_w"], FFN_CONV, nm + "ffn_conv", cw=512)
    hid, = _tile_fwd(nm + "ffn_gate", _f_ffn_gate, [(upc, cw, 0), (upc, cw, FFN_DIM // cw)], [],
                     [(FFN_DIM, cw, BF16)], rows, 512, ncol=FFN_DIM // cw)
    x2 = _mm(hid, wts["ffn_down"], "nn", F32, nm + "ffn_down", res=x1, tk=1408)
    sv.update(h2=h2, up=up, upc=upc, hid=hid)
    return x2, sv


def _layer_backward(l, dx2, wts, sm, sv):
    rows = dx2.shape[0]
    nm = f"l{l}_b_"
    cw = 512
    gr = {}
    dxb = dx2.astype(BF16)
    gr["ffn_down"] = _mm(sv["hid"], dxb, "tn", BF16, nm + "ffn_down_w", tm=1408)
    dhid = _mm(dxb, wts["ffn_down"], "nt", F32, nm + "ffn_down_x", tn=1408)
    (dact, dval), _ = _tile_bwd(nm + "ffn_gate", _f_ffn_gate,
                                [(sv["upc"], cw, 0), (sv["upc"], cw, FFN_DIM // cw)], [], [(dhid, cw, 0)],
                                [F32, F32], rows, 512, ncol=FFN_DIM // cw)
    dupc = jnp.concatenate([dact, dval], axis=1)
    dup, dconv = _conv_bwd(sv["up"], wts["ffn_conv_w"], dupc, FFN_CONV, nm + "ffn_conv", cw=512)
    gr["ffn_conv_w"] = dconv[:FFN_CONV]
    dupb = dup.astype(BF16)
    gr["ffn_up"] = _mm(sv["h2"], dupb, "tn", BF16, nm + "ffn_up_w", tk=1024)
    dh2 = _mm(dupb, wts["ffn_up"], "nt", F32, nm + "ffn_up_x", tk=1024)
    ffw = sm["ffn_norm_w"][l][None]
    (dx1n,), (dffw,) = _tile_bwd(nm + "ffn_norm", _f_rms, [(sv["x1"], None, 0)], [(ffw, None, 0)],
                                 [(dh2, None, 0)], [F32], rows, 256)
    gr["ffn_norm_w"] = dffw[0]
    dx1 = dx2 + dx1n
    dx1b = dx1.astype(BF16)
    gr["w_out"] = _mm(sv["merged"], dx1b, "tn", BF16, nm + "w_out_w", tk=1024)
    dmerged = _mm(dx1b, wts["w_out"], "nt", F32, nm + "w_out_x")
    (dga, dgb, dbr, dgs, dgd), _ = _tile_bwd(
        nm + "merge", _f_merge,
        [(sv["glu"], cw, 0), (sv["glu"], cw, D_MODEL // cw), (sv["br_dn"], cw, 0),
         (sv["proj"], cw, MAIN_GS // cw), (sv["proj"], cw, MAIN_GD // cw)], [], [(dmerged, cw, 0)],
        [BF16, BF16, BF16, BF16, BF16], rows, 512, ncol=D_MODEL // cw)
    dglu = jnp.concatenate([dga, dgb], axis=1)
    gr["s5_glu_w"] = _mm(sv["y_s5"], dglu, "tn", BF16, nm + "glu_w", tk=1024)
    dy_s5 = _mm(dglu, wts["s5_glu_w"], "nt", F32, nm + "glu_x")
    gr["dn_proj_w"] = _mm(sv["y_dn"], dbr, "tn", BF16, nm + "dn_proj_w", tk=1024)
    dy_dn = _mm(dbr, wts["dn_proj_w"], "nt", F32, nm + "dn_proj_x")
    s5p = sv["s5p"]
    du, gbr, gbi, gcr, gci, gar, gai, gd = _s5_bwd(
        sv["proj"], dy_s5, sv["st_re"], sv["st_im"], s5p["bd_re"], s5p["bd_im"], s5p["cd_re"], s5p["cd_im"],
        s5p["a_re"], s5p["a_im"], s5p["d"], nm + "s5_scan")
    gr["s5_d"] = gd[0]
    gr["s5_c_re"] = _block_diag_c_grad(gcr)
    gr["s5_c_im"] = _block_diag_c_grad(gci)
    bt_re, bt_im = sm["s5_b_re"][l].transpose(2, 0, 1), sm["s5_b_im"][l].transpose(2, 0, 1)
    dldt, dare, daim, dbtr, dbti = _s5_disc_bwd(
        sm["s5_log_dt"][l][:, None], sm["s5_a_re"][l], sm["s5_a_im"][l], bt_re, bt_im,
        [gar.reshape(S5_GROUPS, S5_STATE), gai.reshape(S5_GROUPS, S5_STATE),
         _block_diag_b_grad(gbr), _block_diag_b_grad(gbi)], nm + "s5_disc")
    gr.update(s5_log_dt=dldt[:, 0], s5_a_re=dare, s5_a_im=daim,
              s5_b_re=dbtr.transpose(1, 2, 0), s5_b_im=dbti.transpose(1, 2, 0))
    (do, dz), (ddnw,) = _tile_bwd(nm + "dn_post", _f_dn_post,
                                  [(sv["o"], LANES, 0), (sv["proj"], LANES, MAIN_Z // LANES)],
                                  [(sv["dnw"], None, 0)], [(dy_dn, LANES, 0)], [F32, BF16], rows, 512, ncol=8)
    gr["dn_norm_w"] = ddnw[0]
    dq, dk, dv, dgc, dbc = _dn_bwd(sv["qkv"], sv["gcol"], sv["bcol"], sv["states"], do, nm + "dn_chunk")
    dbg = jnp.pad(jnp.concatenate([dbc[:, :, 0].T, dgc[:, :, 0].T], axis=1), ((0, 0), (0, LANES - 2 * DN_HEADS)))
    (dba,), (dalog, ddtb) = _tile_bwd(nm + "dn_gates", _f_dn_gates, [(sv["ba"], None, 0)],
                                      [(sv["alog"], None, 0), (sv["dtb"], None, 0)], [(dbg, None, 0)],
                                      [BF16], rows, 512)
    gr["dn_a_log"] = dalog[0, DN_HEADS:2 * DN_HEADS]
    gr["dn_dt_bias"] = ddtb[0, DN_HEADS:2 * DN_HEADS]
    (dcq,), _ = _tile_bwd(nm + "dn_q", _f_dn_q, [(sv["conv"], LANES, 0)], [], [(dq, LANES, 0)], [F32], rows, 512, 8)
    (dck,), _ = _tile_bwd(nm + "dn_k", _f_dn_k, [(sv["conv"], LANES, 8)], [], [(dk, LANES, 0)], [F32], rows, 512, 8)
    (dcv,), _ = _tile_bwd(nm + "dn_v", _f_dn_v, [(sv["conv"], LANES, 16)], [], [(dv, LANES, 0)], [F32], rows, 512, 8)
    dconv_out = jnp.concatenate([dcq, dck, dcv], axis=1)
    dqkv, ddnconv = _conv_bwd(sv["proj"], wts["dn_conv_w"], dconv_out, DN_CONV, nm + "dn_conv", x_off=2, cw=512)
    gr["dn_conv_w"] = ddnconv[:DN_CONV]
    dproj = jnp.concatenate([du.astype(BF16), dqkv.astype(BF16), dz, dgs, dgd], axis=1)
    gmain = _mm(sv["h"], dproj, "tn", BF16, nm + "proj_w", tk=1024)
    gba = _mm(sv["h"], dba, "tn", BF16, nm + "proj_ba_w", tk=1024)
    gr["w_in"] = jnp.concatenate([gmain[:, :OFF_BA], gba[:, :2 * DN_HEADS], gmain[:, OFF_BA:]], axis=1)
    dh = _mm(dproj, wts["w_in_main"], "nt", F32, nm + "proj_x", tk=1024)
    dh = _mm(dba, wts["w_in_ba"], "nt", F32, nm + "proj_ba_x", res=dh)
    mixw = sm["mix_norm_w"][l][None]
    (dx0n,), (dmixw,) = _tile_bwd(nm + "mix_norm", _f_rms, [(sv["x0"], None, 0)], [(mixw, None, 0)],
                                  [(dh, None, 0)], [F32], rows, 256)
    gr["mix_norm_w"] = dmixw[0]
    return dx1 + dx0n, gr


def kernel(x, mix_norm_w, w_in, s5_log_dt, s5_a_re, s5_a_im, s5_b_re, s5_b_im, s5_c_re, s5_c_im, s5_d, s5_glu_w, dn_conv_w, dn_a_log, dn_dt_bias, dn_norm_w, dn_proj_w, w_out, ffn_norm_w, ffn_up, ffn_conv_w, ffn_down, final_norm_w, loss_target, m_mix_norm_w, m_w_in, m_s5_log_dt, m_s5_a_re, m_s5_a_im, m_s5_b_re, m_s5_b_im, m_s5_c_re, m_s5_c_im, m_s5_d, m_s5_glu_w, m_dn_conv_w, m_dn_a_log, m_dn_dt_bias, m_dn_norm_w, m_dn_proj_w, m_w_out, m_ffn_norm_w, m_ffn_up, m_ffn_conv_w, m_ffn_down, m_final_norm_w, v_mix_norm_w, v_w_in, v_s5_log_dt, v_s5_a_re, v_s5_a_im, v_s5_b_re, v_s5_b_im, v_s5_c_re, v_s5_c_im, v_s5_d, v_s5_glu_w, v_dn_conv_w, v_dn_a_log, v_dn_dt_bias, v_dn_norm_w, v_dn_proj_w, v_w_out, v_ffn_norm_w, v_ffn_up, v_ffn_conv_w, v_ffn_down, v_final_norm_w):
    w = dict(mix_norm_w=mix_norm_w, w_in=w_in, s5_log_dt=s5_log_dt, s5_a_re=s5_a_re, s5_a_im=s5_a_im, s5_b_re=s5_b_re, s5_b_im=s5_b_im, s5_c_re=s5_c_re, s5_c_im=s5_c_im, s5_d=s5_d, s5_glu_w=s5_glu_w, dn_conv_w=dn_conv_w, dn_a_log=dn_a_log, dn_dt_bias=dn_dt_bias, dn_norm_w=dn_norm_w, dn_proj_w=dn_proj_w, w_out=w_out, ffn_norm_w=ffn_norm_w, ffn_up=ffn_up, ffn_conv_w=ffn_conv_w, ffn_down=ffn_down, final_norm_w=final_norm_w)
    mo = dict(mix_norm_w=m_mix_norm_w, w_in=m_w_in, s5_log_dt=m_s5_log_dt, s5_a_re=m_s5_a_re, s5_a_im=m_s5_a_im, s5_b_re=m_s5_b_re, s5_b_im=m_s5_b_im, s5_c_re=m_s5_c_re, s5_c_im=m_s5_c_im, s5_d=m_s5_d, s5_glu_w=m_s5_glu_w, dn_conv_w=m_dn_conv_w, dn_a_log=m_dn_a_log, dn_dt_bias=m_dn_dt_bias, dn_norm_w=m_dn_norm_w, dn_proj_w=m_dn_proj_w, w_out=m_w_out, ffn_norm_w=m_ffn_norm_w, ffn_up=m_ffn_up, ffn_conv_w=m_ffn_conv_w, ffn_down=m_ffn_down, final_norm_w=m_final_norm_w)
    vo = dict(mix_norm_w=v_mix_norm_w, w_in=v_w_in, s5_log_dt=v_s5_log_dt, s5_a_re=v_s5_a_re, s5_a_im=v_s5_a_im, s5_b_re=v_s5_b_re, s5_b_im=v_s5_b_im, s5_c_re=v_s5_c_re, s5_c_im=v_s5_c_im, s5_d=v_s5_d, s5_glu_w=v_s5_glu_w, dn_conv_w=v_dn_conv_w, dn_a_log=v_dn_a_log, dn_dt_bias=v_dn_dt_bias, dn_norm_w=v_dn_norm_w, dn_proj_w=v_dn_proj_w, w_out=v_w_out, ffn_norm_w=v_ffn_norm_w, ffn_up=v_ffn_up, ffn_conv_w=v_ffn_conv_w, ffn_down=v_ffn_down, final_norm_w=v_final_norm_w)
    depth = w_in.shape[0]
    me = 4 * lax.axis_index("x") + 2 * lax.axis_index("y") + lax.axis_index("c")
    xs = x[0]
    target = loss_target[0]

    gather_names = BIG_NAMES + CONV_NAMES
    gathered = _exchange([w[n].astype(BF16) if n in BIG_NAMES else w[n] for n in gather_names], False, "gather_weights")
    full = {}
    for n, g in zip(gather_names, gathered):
        if n in ROW_SHARDED:
            full[n] = g.transpose(1, 0, 2, 3).reshape(depth, -1, g.shape[-1])
        else:
            full[n] = g.transpose(1, 2, 0, 3).reshape(depth, g.shape[2], -1)
    layer_w = []
    for l in range(depth):
        wi = full["w_in"][l]
        lw = {n: full[n][l] for n in gather_names if n != "w_in"}
        lw["w_in_main"] = jnp.concatenate([wi[:, :OFF_BA], wi[:, OFF_GS:]], axis=1)
        lw["w_in_ba"] = jnp.pad(wi[:, OFF_BA:OFF_GS], ((0, 0), (0, LANES - 2 * DN_HEADS)))
        layer_w.append(lw)

    saved = []
    h = xs
    for l in range(depth):
        h, sv = _layer_forward(l, h, layer_w[l], w)
        saved.append(sv)
    dx, dfinal, loss_tile = _final_loss(h, final_norm_w[None], target, "final_loss")

    grads = [None] * depth
    for l in reversed(range(depth)):
        dx, grads[l] = _layer_backward(l, dx, layer_w[l], w, saved[l])

    def stacked(n):
        return jnp.stack([grads[l][n] for l in range(depth)])

    big_send = []
    for n in BIG_NAMES:
        g = stacked(n)
        if n in ROW_SHARDED:
            big_send.append(g.reshape(depth, N_DEV, g.shape[1] // N_DEV, g.shape[2]).transpose(1, 0, 2, 3))
        else:
            big_send.append(g.reshape(depth, g.shape[1], N_DEV, g.shape[2] // N_DEV).transpose(2, 0, 1, 3))
    big_recv = _exchange(big_send, True, "scatter_grads")
    small_list = [stacked(n) for n in SMALL_NAMES if n != "final_norm_w"] + [dfinal[0]]
    small_list += [stacked(n) for n in CONV_NAMES] + [loss_tile[0, 0:1]]
    small_shapes = [a.shape for a in small_list]
    (small_recv,) = _exchange([_pack(small_list)], False, "gather_small")
    small_sum = _unpack(_sum_parts(small_recv, "sum_small"), small_shapes)
    loss = small_sum[-1][0]
    small_names = [n for n in SMALL_NAMES if n != "final_norm_w"] + ["final_norm_w"]
    g_out = dict(zip(small_names, small_sum[:len(small_names)]))
    for n, gfull in zip(CONV_NAMES, small_sum[len(small_names):len(small_names) + 2]):
        shard = w[n].shape[-1]
        g_out[n] = lax.dynamic_slice_in_dim(gfull, me * shard, shard, axis=2)

    d_out, m_out, v_out = {}, {}, {}
    for n, parts in zip(BIG_NAMES, big_recv):
        shp = w[n].shape
        two = lambda a: a.reshape(-1, shp[-1])
        g2, d2, m2, v2 = _reduce_adamw(parts.reshape(N_DEV, -1, shp[-1]), two(w[n]), two(mo[n]), two(vo[n]),
                                       "adamw_" + n)
        g_out[n], d_out[n], m_out[n], v_out[n] = (a.reshape(shp) for a in (g2, d2, m2, v2))
    rest = list(small_names) + list(CONV_NAMES)
    rest_shapes = [w[n].shape for n in rest]
    d2, m2, v2 = _adamw(_pack([g_out[n] for n in rest]), _pack([w[n] for n in rest]), _pack([mo[n] for n in rest]),
                        _pack([vo[n] for n in rest]), "adamw_small")
    for n, d, m_, v_ in zip(rest, _unpack(d2, rest_shapes), _unpack(m2, rest_shapes), _unpack(v2, rest_shapes)):
        d_out[n], m_out[n], v_out[n] = d, m_, v_

    return (loss, dx[None], *[g_out[n] for n in WEIGHT_ORDER], *[d_out[n] for n in WEIGHT_ORDER],
            *[m_out[n] for n in WEIGHT_ORDER], *[v_out[n] for n in WEIGHT_ORDER])
```

```python
import functools
import math

import jax
import jax.numpy as jnp
from jax import lax
from jax.experimental import pallas as pl
from jax.experimental.pallas import tpu as pltpu

F32 = jnp.float32
BF16 = jnp.bfloat16

D_MODEL = 2048
DEPTH = 4
S5_WIDTH = 1024
S5_GROUP = 16
S5_GROUPS = 64
S5_STATE = 64
DN_HEADS = 8
DN_DK = 128
DN_QKV = 3072
DN_CONV = 4
DN_CHUNK = 64
FFN_DIM = 5632
FFN_CONV = 3
NORM_EPS = 1e-6
N_IN = 9232
OFF_Z = 4096
OFF_BA = 5120
OFF_GS = 5136
N_MAIN = 9216
MAIN_Z = 4096
MAIN_GS = 5120
MAIN_GD = 7168

ADAM_LR = 0.001
ADAM_B1 = 0.9
ADAM_B2 = 0.999
ADAM_EPS = 1e-08
ADAM_WD = 0.01
ADAM_STEP = 10

N_DEV = 8
LANES = 128
SUBLANES = 8
VMEM_LIMIT_BYTES = 48 * 1024 * 1024

S5_SLABS = 8
S5_SLAB_STATE = 512
S5_TB = 256
DN_TB = 512
PACK_ROWS = 512


def _cparams(sem):
    return pltpu.CompilerParams(dimension_semantics=sem, vmem_limit_bytes=VMEM_LIMIT_BYTES)


def _dot(a, b, dims, precision=None):
    return lax.dot_general(a, b, (dims, ((), ())), precision=precision, preferred_element_type=F32)


_NN = ((1,), (0,))
_NT = ((1,), (1,))
_TN = ((0,), (0,))


def _mm(a, b, mode, out_dtype, name, res=None, tm=1024, tn=1024, tk=2048):
    if mode == "nn":
        (m, k), (_, n) = a.shape, b.shape
    elif mode == "nt":
        (m, k), (n, _) = a.shape, b.shape
    else:
        (k, m), (_, n) = a.shape, b.shape
    tm, tn, tk = min(tm, m), min(tn, n), min(tk, k)
    assert m % tm == 0 and n % tn == 0 and k % tk == 0, (name, a.shape, b.shape)
    nk = k // tk
    if mode == "tn":
        a_spec = pl.BlockSpec((tk, tm), lambda i, j, kk: (kk, i))
    else:
        a_spec = pl.BlockSpec((tm, tk), lambda i, j, kk: (i, kk))
    if mode == "nt":
        b_spec = pl.BlockSpec((tn, tk), lambda i, j, kk: (j, kk))
    else:
        b_spec = pl.BlockSpec((tk, tn), lambda i, j, kk: (kk, j))
    dims = {"nn": _NN, "nt": _NT, "tn": _TN}[mode]
    o_spec = pl.BlockSpec((tm, tn), lambda i, j, kk: (i, j))
    has_res = res is not None

    def body(*refs):
        if has_res:
            a_ref, b_ref, r_ref, o_ref, acc = refs
        else:
            a_ref, b_ref, o_ref, acc = refs
        p = _dot(a_ref[...], b_ref[...], dims)

        def finish(total):
            if has_res:
                total = total + r_ref[...]
            o_ref[...] = total.astype(out_dtype)

        if nk == 1:
            finish(p)
        else:
            kk = pl.program_id(2)

            @pl.when(kk == 0)
            def _():
                acc[...] = p

            @pl.when(jnp.logical_and(kk > 0, kk < nk - 1))
            def _():
                acc[...] += p

            @pl.when(kk == nk - 1)
            def _():
                finish(acc[...] + p)

    in_specs = [a_spec, b_spec] + ([o_spec] if has_res else [])
    args = (a, b) + ((res,) if has_res else ())
    return pl.pallas_call(
        body, name=name, grid=(m // tm, n // tn, nk), in_specs=in_specs, out_specs=o_spec,
        out_shape=jax.ShapeDtypeStruct((m, n), out_dtype),
        scratch_shapes=[pltpu.VMEM((tm, tn) if nk > 1 else (SUBLANES, LANES), F32)],
        compiler_params=_cparams(("parallel", "parallel", "arbitrary")),
    )(*args)


def _row_spec(tb, width, cw, off):
    if cw is None:
        return pl.BlockSpec((tb, width), lambda j, i: (i, 0))
    return pl.BlockSpec((tb, cw), lambda j, i: (i, j + off))


def _par_spec(rows, width, cw, off):
    if cw is None:
        return pl.BlockSpec((rows, width), lambda j, i: (0, 0))
    return pl.BlockSpec((rows, cw), lambda j, i: (0, j + off))


def _tile_fwd(name, fn, tiled, params, outs, rows, tb, ncol=1):
    tb = min(tb, rows)
    nt, npar = len(tiled), len(params)

    def body(*refs):
        vals = [r[...] for r in refs[:nt + npar]]
        res = fn(*vals)
        for o_ref, r in zip(refs[nt + npar:], res):
            o_ref[...] = r.astype(o_ref.dtype)

    in_specs = [_row_spec(tb, a.shape[1], cw, off) for a, cw, off in tiled]
    in_specs += [_par_spec(a.shape[0], a.shape[1], cw, off) for a, cw, off in params]
    out_specs = [_row_spec(tb, w, cw, 0) for w, cw, _ in outs]
    out_shape = [jax.ShapeDtypeStruct((rows, w), dt) for w, _, dt in outs]
    return pl.pallas_call(
        body, name=name, grid=(ncol, rows // tb), in_specs=in_specs, out_specs=out_specs, out_shape=out_shape,
        compiler_params=_cparams(("parallel", "parallel")),
    )(*[a for a, _, _ in tiled], *[a for a, _, _ in params])


def _tile_bwd(name, fn, tiled, params, cots, gdtypes, rows, tb, ncol=1):
    tb = min(tb, rows)
    nt, npar, nc = len(tiled), len(params), len(cots)
    want = [i for i, g in enumerate(gdtypes) if g is not None]

    def body(*refs):
        vals = [r[...] for r in refs[:nt + npar]]
        cot_refs = refs[nt + npar:nt + npar + nc]
        g_refs = refs[nt + npar + nc:nt + npar + nc + len(want)]
        p_refs = refs[nt + npar + nc + len(want):]
        _, vjp = jax.vjp(fn, *vals)
        grads = vjp(tuple(c[...].astype(F32) for c in cot_refs))
        for g_ref, i in zip(g_refs, want):
            g_ref[...] = grads[i].astype(g_ref.dtype)
        jcol, irow = pl.program_id(0), pl.program_id(1)
        for p_ref, g, (_, cw, _) in zip(p_refs, grads[nt:], params):
            first = (irow == 0) if cw is not None else jnp.logical_and(irow == 0, jcol == 0)

            @pl.when(first)
            def _():
                p_ref[...] = g

            @pl.when(jnp.logical_not(first))
            def _():
                p_ref[...] += g

    in_specs = [_row_spec(tb, a.shape[1], cw, off) for a, cw, off in tiled]
    in_specs += [_par_spec(a.shape[0], a.shape[1], cw, off) for a, cw, off in params]
    in_specs += [_row_spec(tb, a.shape[1], cw, off) for a, cw, off in cots]
    out_specs, out_shape = [], []
    for i in want:
        a, cw, _ = tiled[i]
        width = a.shape[1] if cw is None else ncol * cw
        out_specs.append(_row_spec(tb, width, cw, 0))
        out_shape.append(jax.ShapeDtypeStruct((rows, width), gdtypes[i]))
    for a, cw, _ in params:
        width = a.shape[1] if cw is None else ncol * cw
        out_specs.append(_par_spec(a.shape[0], width, cw, 0))
        out_shape.append(jax.ShapeDtypeStruct((a.shape[0], width), F32))
    res = pl.pallas_call(
        body, name=name, grid=(ncol, rows // tb), in_specs=in_specs, out_specs=out_specs, out_shape=out_shape,
        compiler_params=_cparams(("arbitrary", "arbitrary")),
    )(*[a for a, _, _ in tiled], *[a for a, _, _ in params], *[a for a, _, _ in cots])
    return res[:len(want)], res[len(want):]


def _sigmoid(x):
    return 1.0 / (1.0 + jnp.exp(-x))


def _silu(x):
    return x * _sigmoid(x)


def _softplus(x):
    return jnp.maximum(x, 0.0) + jnp.log1p(jnp.exp(-jnp.abs(x)))


def _f_rms(x, w):
    return (x * lax.rsqrt(jnp.mean(x * x, axis=-1, keepdims=True) + NORM_EPS) * w,)


def _f_merge(glu_a, glu_b, br_dn, gs, gd):
    return (_sigmoid(gs) * (glu_a * _sigmoid(glu_b)) + _sigmoid(gd) * br_dn,)


def _f_ffn_gate(act, val):
    return (_silu(act) * val,)


def _l2n(x):
    return x * lax.rsqrt(jnp.sum(x * x, axis=-1, keepdims=True) + NORM_EPS)


def _f_dn_q(c):
    return (_l2n(_silu(c)) * (DN_DK ** -0.5),)


def _f_dn_k(c):
    return (_l2n(_silu(c)),)


def _f_dn_v(c):
    return (_silu(c),)


def _f_dn_gates(ba, a_log, dt_bias):
    col = lax.broadcasted_iota(jnp.int32, ba.shape, 1)
    beta = _sigmoid(ba)
    g = -jnp.exp(a_log) * _softplus(ba + dt_bias)
    return (jnp.where(col < DN_HEADS, beta, jnp.where(col < 2 * DN_HEADS, g, 0.0)),)


def _f_dn_post(o, z, w):
    return (_f_rms(o, w)[0] * _silu(z),)


def _shift_down(x, halo, s, tb):
    if s == 0:
        return x
    y = pltpu.roll(x, s, 0)
    row8 = lax.broadcasted_iota(jnp.int32, halo.shape, 0)
    top = jnp.where(row8 < s, pltpu.roll(halo, s, 0), y[0:SUBLANES])
    if tb == SUBLANES:
        return top
    return jnp.concatenate([top, y[SUBLANES:]], axis=0)


def _shift_up(x, halo, s, tb):
    if s == 0:
        return x
    y = pltpu.roll(x, tb - s, 0)
    row8 = lax.broadcasted_iota(jnp.int32, halo.shape, 0)
    bot = jnp.where(row8 >= SUBLANES - s, pltpu.roll(halo, SUBLANES - s, 0), y[tb - SUBLANES:])
    if tb == SUBLANES:
        return bot
    return jnp.concatenate([y[:tb - SUBLANES], bot], axis=0)


def _conv_fwd(x, w, kw, name, x_off=0, width=None, cw=512, tb=512):
    rows = x.shape[0]
    width = w.shape[1] if width is None else width
    tb = min(tb, rows)
    nb = tb // SUBLANES

    def body(x_ref, h_ref, w_ref, o_ref):
        i = pl.program_id(1)
        xv = x_ref[...]
        halo = jnp.where(i > 0, h_ref[...], 0.0)
        acc = w_ref[kw - 1:kw, :] * xv
        for s in range(1, kw):
            acc = acc + w_ref[kw - 1 - s:kw - s, :] * _shift_down(xv, halo, s, tb)
        o_ref[...] = acc

    return pl.pallas_call(
        body, name=name, grid=(width // cw, rows // tb),
        in_specs=[pl.BlockSpec((tb, cw), lambda j, i: (i, j + x_off)),
                  pl.BlockSpec((SUBLANES, cw), lambda j, i: (jnp.maximum(i * nb - 1, 0), j + x_off)),
                  pl.BlockSpec((kw, cw), lambda j, i: (0, j))],
        out_specs=pl.BlockSpec((tb, cw), lambda j, i: (i, j)),
        out_shape=jax.ShapeDtypeStruct((rows, width), F32),
        compiler_params=_cparams(("parallel", "parallel")),
    )(x, x, w)


def _conv_bwd(x, w, dout, kw, name, x_off=0, cw=512, tb=512):
    rows, width = dout.shape
    tb = min(tb, rows)
    nb = tb // SUBLANES
    nrow = rows // tb

    def body(x_ref, h_ref, w_ref, d_ref, dn_ref, dx_ref, dw_ref):
        i = pl.program_id(1)
        xv, dv = x_ref[...], d_ref[...]
        halo = jnp.where(i > 0, h_ref[...], 0.0)
        nxt = jnp.where(i < nrow - 1, dn_ref[...], 0.0)

        @pl.when(i == 0)
        def _():
            dw_ref[...] = jnp.zeros_like(dw_ref)

        acc = w_ref[kw - 1:kw, :] * dv
        dw_ref[kw - 1:kw, :] += jnp.sum(dv * xv, axis=0, keepdims=True)
        for s in range(1, kw):
            acc = acc + w_ref[kw - 1 - s:kw - s, :] * _shift_up(dv, nxt, s, tb)
            dw_ref[kw - 1 - s:kw - s, :] += jnp.sum(dv * _shift_down(xv, halo, s, tb), axis=0, keepdims=True)
        dx_ref[...] = acc

    return pl.pallas_call(
        body, name=name, grid=(width // cw, nrow),
        in_specs=[pl.BlockSpec((tb, cw), lambda j, i: (i, j + x_off)),
                  pl.BlockSpec((SUBLANES, cw), lambda j, i: (jnp.maximum(i * nb - 1, 0), j + x_off)),
                  pl.BlockSpec((kw, cw), lambda j, i: (0, j)),
                  pl.BlockSpec((tb, cw), lambda j, i: (i, j)),
                  pl.BlockSpec((SUBLANES, cw), lambda j, i: (jnp.minimum((i + 1) * nb, rows // SUBLANES - 1), j))],
        out_specs=[pl.BlockSpec((tb, cw), lambda j, i: (i, j)),
                   pl.BlockSpec((SUBLANES, cw), lambda j, i: (0, j))],
        out_shape=[jax.ShapeDtypeStruct((rows, width), F32), jax.ShapeDtypeStruct((SUBLANES, width), F32)],
        compiler_params=_cparams(("parallel", "arbitrary")),
    )(x, x, w, dout, dout)


def _f_s5_disc(log_dt, a_re, a_im, bt_re, bt_im):
    dt = jnp.exp(log_dt)
    mag = jnp.exp(a_re * dt)
    abar_re, abar_im = mag * jnp.cos(a_im * dt), mag * jnp.sin(a_im * dt)
    den = a_re * a_re + a_im * a_im
    nr, ni = abar_re - 1.0, abar_im
    coef_re = (nr * a_re + ni * a_im) / den
    coef_im = (ni * a_re - nr * a_im) / den
    bbar_re = coef_re[None] * bt_re - coef_im[None] * bt_im
    bbar_im = coef_re[None] * bt_im + coef_im[None] * bt_re
    return abar_re, abar_im, bbar_re, bbar_im


def _s5_disc_fwd(log_dt, a_re, a_im, bt_re, bt_im, name):
    def body(*refs):
        res = _f_s5_disc(*[r[...] for r in refs[:5]])
        for o_ref, r in zip(refs[5:], res):
            o_ref[...] = r

    shp = [a_re, a_re, bt_re, bt_re]
    return pl.pallas_call(body, name=name, out_shape=[jax.ShapeDtypeStruct(s.shape, F32) for s in shp])(
        log_dt, a_re, a_im, bt_re, bt_im)


def _s5_disc_bwd(log_dt, a_re, a_im, bt_re, bt_im, cots, name):
    def body(*refs):
        _, vjp = jax.vjp(_f_s5_disc, *[r[...] for r in refs[:5]])
        grads = vjp(tuple(r[...] for r in refs[5:9]))
        for o_ref, g in zip(refs[9:], grads):
            o_ref[...] = g

    ins = [log_dt, a_re, a_im, bt_re, bt_im]
    return pl.pallas_call(body, name=name, out_shape=[jax.ShapeDtypeStruct(s.shape, F32) for s in ins])(*ins, *cots)


def _cmul(ar, ai, br, bi):
    return ar * br - ai * bi, ar * bi + ai * br


def _seg_scan(xr_ref, xi_ref, ar, ai, cr, ci, pr_ref, pi_ref, tb, reverse):
    sl = tb // SUBLANES
    groups = xr_ref.shape[0]
    lane = lambda a, k: a[:, k * LANES:(k + 1) * LANES]
    outs_r, outs_i = [], []
    a_k = [(lane(ar, k), lane(ai, k)) for k in range(groups)]

    def step(i, carry):
        t = sl - 1 - i if reverse else i
        rows = pl.ds(t, SUBLANES, stride=sl)
        nxt = []
        for k, (xr, xi, pr, pi) in enumerate(carry):
            akr, aki = a_k[k]
            mr, mi = _cmul(akr, aki, xr, xi)
            xr, xi = mr + xr_ref[k, rows, :], mi + xi_ref[k, rows, :]
            xr_ref[k, rows, :] = xr
            xi_ref[k, rows, :] = xi
            pr_ref[k, pl.ds(t, 1), :] = pr
            pi_ref[k, pl.ds(t, 1), :] = pi
            nr, ni = _cmul(akr, aki, pr, pi)
            nxt.append((xr, xi, nr, ni))
        return tuple(nxt)

    zero = jnp.zeros((SUBLANES, LANES), F32)
    lax.fori_loop(0, sl, step, tuple((zero, zero, akr, aki) for akr, aki in a_k))
    last = 0 if reverse else sl - 1
    for k in range(groups):
        qr, qi = pr_ref[k, last:last + 1, :], pi_ref[k, last:last + 1, :]
        tr, ti = pr_ref[k], pi_ref[k]
        ckr, cki = lane(cr, k), lane(ci, k)
        order = range(SUBLANES - 1, -1, -1) if reverse else range(SUBLANES)
        for j in order:
            rows = slice(j * sl, (j + 1) * sl)
            edge = j * sl if reverse else (j + 1) * sl - 1
            er, ei = xr_ref[k, edge:edge + 1, :], xi_ref[k, edge:edge + 1, :]
            mr, mi = _cmul(tr, ti, ckr, cki)
            xr_ref[k, rows, :] += mr
            xi_ref[k, rows, :] += mi
            mr, mi = _cmul(qr, qi, ckr, cki)
            ckr, cki = er + mr, ei + mi
        outs_r.append(ckr)
        outs_i.append(cki)
    return jnp.concatenate(outs_r, axis=1), jnp.concatenate(outs_i, axis=1)


def _to_groups(ref, val):
    for k in range(ref.shape[0]):
        ref[k] = val[:, k * LANES:(k + 1) * LANES]


def _from_groups(ref):
    return jnp.concatenate([ref[k] for k in range(ref.shape[0])], axis=1)


_INV_SQRT2 = 1.0 / math.sqrt(2.0)
_INV_SQRT2PI = 1.0 / math.sqrt(2.0 * math.pi)


def _gelu(y):
    return 0.5 * y * (1.0 + lax.erf(y * _INV_SQRT2))


def _gelu_grad(y):
    return 0.5 * (1.0 + lax.erf(y * _INV_SQRT2)) + y * jnp.exp(-0.5 * y * y) * _INV_SQRT2PI


def _s5_states(u, bd_re, bd_im, ar, ai, cr, ci, xr_ref, xi_ref, pr_ref, pi_ref, tb):
    ub = u.astype(BF16)
    _to_groups(xr_ref, _dot(ub, bd_re, _NN))
    _to_groups(xi_ref, _dot(ub, bd_im, _NN))
    return _seg_scan(xr_ref, xi_ref, ar, ai, cr, ci, pr_ref, pi_ref, tb, reverse=False)


def _s5_scratch(tb, nbuf):
    groups = S5_SLAB_STATE // LANES
    return ([pltpu.VMEM((1, S5_SLAB_STATE), F32)] * 2 + [pltpu.VMEM((groups, tb, LANES), F32)] * nbuf
            + [pltpu.VMEM((groups, tb // SUBLANES, LANES), F32)] * 2)


def _s5_specs(tb, u_off):
    slab3 = lambda r, c: pl.BlockSpec((None, r, c), lambda s, t: (s, 0, 0))
    return dict(
        u=lambda tmap: pl.BlockSpec((tb, LANES), lambda s, t: (tmap(t), s + u_off)),
        bd=slab3(LANES, S5_SLAB_STATE), cd=slab3(S5_SLAB_STATE, LANES), a=slab3(1, S5_SLAB_STATE),
        d=pl.BlockSpec((1, LANES), lambda s, t: (0, s)))


def _s5_fwd(proj, bd_re, bd_im, cd_re, cd_im, a_re, a_im, d, name):
    rows = proj.shape[0]
    tb = min(S5_TB, rows)
    nt = rows // tb
    sp = _s5_specs(tb, 0)

    def body(u_ref, bdr, bdi, cdr, cdi, ar_ref, ai_ref, d_ref, y_ref, sr_ref, si_ref,
             cr_s, ci_s, xr_s, xi_s, pr_s, pi_s):
        t = pl.program_id(1)

        @pl.when(t == 0)
        def _():
            cr_s[...] = jnp.zeros_like(cr_s)
            ci_s[...] = jnp.zeros_like(ci_s)

        cr, ci = cr_s[...], ci_s[...]
        sr_ref[...] = cr
        si_ref[...] = ci
        u = u_ref[...]
        cr, ci = _s5_states(u, bdr[...], bdi[...], ar_ref[...], ai_ref[...], cr, ci, xr_s, xi_s, pr_s, pi_s, tb)
        cr_s[...] = cr
        ci_s[...] = ci
        y = (_dot(_from_groups(xr_s).astype(BF16), cdr[...], _NN)
             - _dot(_from_groups(xi_s).astype(BF16), cdi[...], _NN) + d_ref[...] * u)
        y_ref[...] = _gelu(y).astype(BF16)

    st_spec = pl.BlockSpec((None, None, 1, S5_SLAB_STATE), lambda s, t: (s, t, 0, 0))
    st_shape = jax.ShapeDtypeStruct((S5_SLABS, nt, 1, S5_SLAB_STATE), F32)
    return pl.pallas_call(
        body, name=name, grid=(S5_SLABS, nt),
        in_specs=[sp["u"](lambda t: t), sp["bd"], sp["bd"], sp["cd"], sp["cd"], sp["a"], sp["a"], sp["d"]],
        out_specs=[pl.BlockSpec((tb, LANES), lambda s, t: (t, s)), st_spec, st_spec],
        out_shape=[jax.ShapeDtypeStruct((rows, S5_WIDTH), BF16), st_shape, st_shape],
        scratch_shapes=_s5_scratch(tb, 2),
        compiler_params=_cparams(("parallel", "arbitrary")),
    )(proj, bd_re, bd_im, cd_re, cd_im, a_re, a_im, d)


def _s5_bwd(proj, dy, st_re, st_im, bd_re, bd_im, cd_re, cd_im, a_re, a_im, d, name):
    rows = proj.shape[0]
    tb = min(S5_TB, rows)
    nt = rows // tb
    sp = _s5_specs(tb, 0)
    rev = lambda t: nt - 1 - t

    def body(u_ref, dy_ref, sr_ref, si_ref, bdr, bdi, cdr, cdi, ar_ref, ai_ref, d_ref,
             du_ref, gbr, gbi, gcr, gci, gar, gai, gd_ref, lr_s, li_s, tr_s, ti_s, xr_s, xi_s, pr_s, pi_s):
        t = pl.program_id(1)

        @pl.when(t == 0)
        def _():
            lr_s[...] = jnp.zeros_like(lr_s)
            li_s[...] = jnp.zeros_like(li_s)
            for r in (gbr, gbi, gcr, gci, gar, gai, gd_ref):
                r[...] = jnp.zeros_like(r)

        u = u_ref[...]
        ar, ai = ar_ref[...], ai_ref[...]
        cr, ci = sr_ref[...], si_ref[...]
        _s5_states(u, bdr[...], bdi[...], ar, ai, cr, ci, xr_s, xi_s, pr_s, pi_s, tb)
        xr, xi = _from_groups(xr_s), _from_groups(xi_s)
        xrb, xib = xr.astype(BF16), xi.astype(BF16)
        ypre = _dot(xrb, cdr[...], _NN) - _dot(xib, cdi[...], _NN) + d_ref[...] * u
        dyp = dy_ref[...] * _gelu_grad(ypre)
        dypb = dyp.astype(BF16)
        gd_ref[...] += jnp.sum(dyp * u, axis=0, keepdims=True)
        gcr[...] += _dot(xrb, dypb, _TN)
        gci[...] -= _dot(xib, dypb, _TN)
        _to_groups(tr_s, _dot(dypb, cdr[...], _NT))
        _to_groups(ti_s, -_dot(dypb, cdi[...], _NT))
        nr, ni = _seg_scan(tr_s, ti_s, ar, -ai, lr_s[...], li_s[...], pr_s, pi_s, tb, reverse=True)
        lr_s[...] = nr
        li_s[...] = ni
        lr, li = _from_groups(tr_s), _from_groups(ti_s)
        row = lax.broadcasted_iota(jnp.int32, (tb, 1), 0)
        lrb, lib = lr.astype(BF16), li.astype(BF16)
        du_ref[...] = _dot(lrb, bdr[...], _NT) + _dot(lib, bdi[...], _NT) + d_ref[...] * dyp
        ub = u.astype(BF16)
        gbr[...] += _dot(ub, lrb, _TN)
        gbi[...] += _dot(ub, lib, _TN)
        xpr = jnp.where(row == 0, cr, pltpu.roll(xr, 1, 0))
        xpi = jnp.where(row == 0, ci, pltpu.roll(xi, 1, 0))
        gar[...] += jnp.sum(lr * xpr + li * xpi, axis=0, keepdims=True)
        gai[...] += jnp.sum(li * xpr - lr * xpi, axis=0, keepdims=True)

    st_spec = pl.BlockSpec((None, None, 1, S5_SLAB_STATE), lambda s, t: (s, rev(t), 0, 0))
    slab = lambda r, c: pl.BlockSpec((None, r, c), lambda s, t: (s, 0, 0))
    return pl.pallas_call(
        body, name=name, grid=(S5_SLABS, nt),
        in_specs=[sp["u"](rev), pl.BlockSpec((tb, LANES), lambda s, t: (rev(t), s)), st_spec, st_spec,
                  sp["bd"], sp["bd"], sp["cd"], sp["cd"], sp["a"], sp["a"], sp["d"]],
        out_specs=[pl.BlockSpec((tb, LANES), lambda s, t: (rev(t), s)),
                   slab(LANES, S5_SLAB_STATE), slab(LANES, S5_SLAB_STATE),
                   slab(S5_SLAB_STATE, LANES), slab(S5_SLAB_STATE, LANES),
                   slab(1, S5_SLAB_STATE), slab(1, S5_SLAB_STATE),
                   pl.BlockSpec((1, LANES), lambda s, t: (0, s))],
        out_shape=[jax.ShapeDtypeStruct((rows, S5_WIDTH), F32),
                   jax.ShapeDtypeStruct((S5_SLABS, LANES, S5_SLAB_STATE), F32),
                   jax.ShapeDtypeStruct((S5_SLABS, LANES, S5_SLAB_STATE), F32),
                   jax.ShapeDtypeStruct((S5_SLABS, S5_SLAB_STATE, LANES), F32),
                   jax.ShapeDtypeStruct((S5_SLABS, S5_SLAB_STATE, LANES), F32),
                   jax.ShapeDtypeStruct((S5_SLABS, 1, S5_SLAB_STATE), F32),
                   jax.ShapeDtypeStruct((S5_SLABS, 1, S5_SLAB_STATE), F32),
                   jax.ShapeDtypeStruct((1, S5_WIDTH), F32)],
        scratch_shapes=_s5_scratch(tb, 4),
        compiler_params=_cparams(("parallel", "arbitrary")),
    )(proj, dy, st_re, st_im, bd_re, bd_im, cd_re, cd_im, a_re, a_im, d)


@functools.partial(jax.custom_vjp, nondiff_argnums=(2,))
def _bdot(a, b, dims):
    return _dot(a.astype(BF16), b.astype(BF16), dims)


def _bdot_fwd(a, b, dims):
    return _bdot(a, b, dims), (a, b)


def _bdot_bwd(dims, res, ct):
    a, b = res
    if dims == _NN:
        return _bdot(ct, b, _NT), _bdot(a, ct, _TN)
    if dims == _NT:
        return _bdot(ct, b, _NN), _bdot(ct, a, _TN)
    return _bdot(b, ct, _NT), _bdot(a, ct, _NN)


_bdot.defvjp(_bdot_fwd, _bdot_bwd)


def _split_bf16(a):
    hi = a.astype(BF16)
    return hi, (a - hi.astype(F32)).astype(BF16)


@functools.partial(jax.custom_vjp, nondiff_argnums=(2,))
def _dot3(a, b, dims):
    ah, al = _split_bf16(a)
    bh, bl = _split_bf16(b)
    return _dot(ah, bh, dims) + (_dot(ah, bl, dims) + _dot(al, bh, dims))


def _dot3_fwd(a, b, dims):
    return _dot3(a, b, dims), (a, b)


def _dot3_bwd(dims, res, ct):
    a, b = res
    if dims == _NN:
        return _dot3(ct, b, _NT), _dot3(a, ct, _TN)
    if dims == _NT:
        return _dot3(ct, b, _NN), _dot3(ct, a, _TN)
    return _dot3(b, ct, _NT), _dot3(a, ct, _NN)


_dot3.defvjp(_dot3_fwd, _dot3_bwd)


def _tril_ones(c):
    r = lax.broadcasted_iota(jnp.int32, (c, c), 0)
    col = lax.broadcasted_iota(jnp.int32, (c, c), 1)
    return jnp.where(r >= col, 1.0, 0.0).astype(BF16)


@jax.custom_vjp
def _chunk_cumsum(x):
    xh, xl = _split_bf16(x)
    t = _tril_ones(x.shape[0])
    return _dot(t, xh, _NN) + _dot(t, xl, _NN)


def _chunk_cumsum_fwd(x):
    return _chunk_cumsum(x), None


def _chunk_cumsum_bwd(_, ct):
    ch, cl = _split_bf16(ct)
    t = _tril_ones(ct.shape[0])
    return (_dot(t, ch, _TN) + _dot(t, cl, _TN),)


_chunk_cumsum.defvjp(_chunk_cumsum_fwd, _chunk_cumsum_bwd)


def _unit_lower_inverse(lm, n):
    r = lax.broadcasted_iota(jnp.int32, (n, n), 0)
    c = lax.broadcasted_iota(jnp.int32, (n, n), 1)
    p = jnp.where(r == c, 1.0, 0.0) - lm
    power = lm
    steps = int(math.log2(n)) - 1
    for _ in range(steps):
        power = _dot3(power, power, _NN)
        p = p + _dot3(p, power, _NN)
    return p


def _dn_chunk(q, k, v, g, b, s):
    c = q.shape[0]
    r = lax.broadcasted_iota(jnp.int32, (c, c), 0)
    col = lax.broadcasted_iota(jnp.int32, (c, c), 1)
    tril = r >= col
    strict = r > col
    gl = jnp.broadcast_to(g, (c, LANES))
    gc = _chunk_cumsum(gl)
    gtot = jnp.sum(gl, axis=0, keepdims=True)
    gdiff = _chunk_cumsum(jnp.where(strict, jnp.broadcast_to(g, (c, c)), 0.0))
    decay = jnp.where(tril, jnp.exp(jnp.where(tril, gdiff, 0.0)), 0.0)
    kb = k * b
    vb = v * b
    lmat = jnp.where(strict, _bdot(kb, k, _NT) * decay, 0.0)
    tinv = _unit_lower_inverse(lmat, c)
    u = _dot3(tinv, vb, _NN)
    w = _dot3(tinv, kb * jnp.exp(gc), _NN)
    attn = jnp.where(tril, _bdot(q, k, _NT) * decay, 0.0)
    v_new = u - _bdot(w, s, _NN)
    o = _bdot(q * jnp.exp(gc), s, _NN) + _bdot(attn, v_new, _NN)
    s_new = s * jnp.exp(gtot) + _bdot(k * jnp.exp(gtot - gc), v_new, _TN)
    return o, s_new


def _dn_specs(tb, hb, tmap):
    groups = DN_HEADS // hb
    blk = lambda part: pl.BlockSpec((tb, hb * LANES), lambda hg, t: (tmap(t), hg + part * groups))
    colv = pl.BlockSpec((hb, tb, 1), lambda hg, t: (hg, tmap(t), 0))
    st = pl.BlockSpec((hb, tb // DN_CHUNK, DN_DK, DN_DK), lambda hg, t: (hg, tmap(t), 0, 0))
    return blk, colv, st


def _dn_fwd(qkv, gcol, bcol, name, hb=DN_HEADS):
    rows = qkv.shape[0]
    tb = min(DN_TB, rows)
    nt = rows // tb
    nch = tb // DN_CHUNK
    blk, colv, st = _dn_specs(tb, hb, lambda t: t)

    def body(q_ref, k_ref, v_ref, g_ref, b_ref, o_ref, st_ref, s_scr):
        @pl.when(pl.program_id(1) == 0)
        def _():
            s_scr[...] = jnp.zeros_like(s_scr)

        def chunk(ci, carry):
            rs = pl.ds(pl.multiple_of(ci * DN_CHUNK, DN_CHUNK), DN_CHUNK)
            for j in range(hb):
                cs = slice(j * LANES, (j + 1) * LANES)
                s_in = s_scr[j]
                st_ref[j, ci] = s_in
                o, s_new = _dn_chunk(q_ref[rs, cs], k_ref[rs, cs], v_ref[rs, cs], g_ref[j, rs, :], b_ref[j, rs, :],
                                     s_in)
                o_ref[rs, cs] = o
                s_scr[j] = s_new
            return carry

        lax.fori_loop(0, nch, chunk, 0)

    return pl.pallas_call(
        body, name=name, grid=(DN_HEADS // hb, nt),
        in_specs=[blk(0), blk(1), blk(2), colv, colv],
        out_specs=[blk(0), st],
        out_shape=[jax.ShapeDtypeStruct((rows, DN_HEADS * DN_DK), F32),
                   jax.ShapeDtypeStruct((DN_HEADS, rows // DN_CHUNK, DN_DK, DN_DK), F32)],
        scratch_shapes=[pltpu.VMEM((hb, DN_DK, DN_DK), F32)],
        compiler_params=_cparams(("parallel", "arbitrary")),
    )(qkv, qkv, qkv, gcol, bcol)


def _dn_bwd(qkv, gcol, bcol, states, do, name, hb=DN_HEADS // 2):
    rows = qkv.shape[0]
    tb = min(DN_TB, rows)
    nt = rows // tb
    nch = tb // DN_CHUNK
    blk, colv, st = _dn_specs(tb, hb, lambda t: nt - 1 - t)

    def body(q_ref, k_ref, v_ref, g_ref, b_ref, st_ref, do_ref, dq_ref, dk_ref, dv_ref, dg_ref, db_ref, ds_scr):
        @pl.when(pl.program_id(1) == 0)
        def _():
            ds_scr[...] = jnp.zeros_like(ds_scr)

        def chunk(cj, carry):
            ci = nch - 1 - cj
            rs = pl.ds(pl.multiple_of(ci * DN_CHUNK, DN_CHUNK), DN_CHUNK)
            for j in range(hb):
                cs = slice(j * LANES, (j + 1) * LANES)
                args = (q_ref[rs, cs], k_ref[rs, cs], v_ref[rs, cs], g_ref[j, rs, :], b_ref[j, rs, :], st_ref[j, ci])
                _, vjp = jax.vjp(_dn_chunk, *args)
                dq, dk, dv, dg, db, ds = vjp((do_ref[rs, cs], ds_scr[j]))
                dq_ref[rs, cs] = dq
                dk_ref[rs, cs] = dk
                dv_ref[rs, cs] = dv
                dg_ref[j, rs, :] = dg
                db_ref[j, rs, :] = db
                ds_scr[j] = ds
            return carry

        lax.fori_loop(0, nch, chunk, 0)

    wide = jax.ShapeDtypeStruct((rows, DN_HEADS * DN_DK), F32)
    narrow = jax.ShapeDtypeStruct((DN_HEADS, rows, 1), F32)
    return pl.pallas_call(
        body, name=name, grid=(DN_HEADS // hb, nt),
        in_specs=[blk(0), blk(1), blk(2), colv, colv, st, blk(0)],
        out_specs=[blk(0), blk(0), blk(0), colv, colv],
        out_shape=[wide, wide, wide, narrow, narrow],
        scratch_shapes=[pltpu.VMEM((hb, DN_DK, DN_DK), F32)],
        compiler_params=_cparams(("parallel", "arbitrary")),
    )(qkv, qkv, qkv, gcol, bcol, states, do)


def _final_loss(x, w, target, name, tb=256):
    rows, width = x.shape
    tb = min(tb, rows)

    def body(x_ref, w_ref, t_ref, dx_ref, dw_ref, loss_ref):
        i = pl.program_id(0)
        (y,), vjp = jax.vjp(_f_rms, x_ref[...], w_ref[...])
        err = y - t_ref[...]
        part = 0.5 * jnp.sum(jnp.mean(err * err, axis=-1, keepdims=True), axis=0, keepdims=True)
        dx, dw = vjp((err * (1.0 / width),))
        dx_ref[...] = dx

        @pl.when(i == 0)
        def _():
            dw_ref[...] = dw
            loss_ref[...] = jnp.broadcast_to(part, loss_ref.shape)

        @pl.when(i > 0)
        def _():
            dw_ref[...] += dw
            loss_ref[...] += jnp.broadcast_to(part, loss_ref.shape)

    row = pl.BlockSpec((tb, width), lambda i: (i, 0))
    par = pl.BlockSpec((1, width), lambda i: (0, 0))
    return pl.pallas_call(
        body, name=name, grid=(rows // tb,), in_specs=[row, par, row],
        out_specs=[row, par, pl.BlockSpec((SUBLANES, LANES), lambda i: (0, 0))],
        out_shape=[jax.ShapeDtypeStruct((rows, width), F32), jax.ShapeDtypeStruct((1, width), F32),
                   jax.ShapeDtypeStruct((SUBLANES, LANES), F32)],
        compiler_params=_cparams(("arbitrary",)),
    )(x, w, target)


_ANY = pl.BlockSpec(memory_space=pl.ANY)
N_CHIPS = 4


def _mesh_place():
    x, y, c = lax.axis_index("x"), lax.axis_index("y"), lax.axis_index("c")
    other_chips = [(1 - x, y), (x, 1 - y), (1 - x, 1 - y)]
    return x, y, c, other_chips


def _remote(src, dst, send_sem, recv_sem, dev):
    return pltpu.make_async_remote_copy(src_ref=src, dst_ref=dst, send_sem=send_sem, recv_sem=recv_sem,
                                        device_id=dev, device_id_type=pl.DeviceIdType.MESH)


def _all_gather(arrs, name):
    n = len(arrs)

    def body(*refs):
        ins, outs = refs[:n], refs[n:2 * n]
        send_sems, recv_sems, loc_sems = refs[2 * n:]
        x, y, c, chips = _mesh_place()
        me, sibling = (x, y, c), (x, y, 1 - c)
        slot = lambda px, py, pc: 4 * px + 2 * py + pc
        local = []
        for i in range(n):
            cp = pltpu.make_async_copy(ins[i], outs[i].at[slot(*me)], loc_sems.at[i])
            cp.start()
            local.append(cp)
        sends = []
        for i in range(n):
            cp = _remote(ins[i], outs[i].at[slot(*me)], send_sems.at[i, 0], recv_sems.at[i, 0], sibling)
            cp.start()
            sends.append(cp)
            for j, chip in enumerate(chips):
                cp = _remote(ins[i], outs[i].at[slot(*me)], send_sems.at[i, 1 + j], recv_sems.at[i, 1 + j], (*chip, c))
                cp.start()
                sends.append(cp)
        for j, chip in enumerate(chips):
            for i in range(n):
                got = outs[i].at[slot(*chip, c)]
                _remote(got, got, send_sems.at[i, 1 + j], recv_sems.at[i, 1 + j], (*chip, c)).wait_recv()
                cp = _remote(got, got, send_sems.at[i, 4 + j], recv_sems.at[i, 4 + j], sibling)
                cp.start()
                sends.append(cp)
        for i in range(n):
            got = outs[i].at[slot(*sibling)]
            _remote(got, got, send_sems.at[i, 0], recv_sems.at[i, 0], sibling).wait_recv()
            for j, chip in enumerate(chips):
                got = outs[i].at[slot(*chip, 1 - c)]
                _remote(got, got, send_sems.at[i, 4 + j], recv_sems.at[i, 4 + j], sibling).wait_recv()
        for cp in sends:
            cp.wait_send()
        for cp in local:
            cp.wait()

    return pl.pallas_call(
        body, name=name, in_specs=[_ANY] * n, out_specs=[_ANY] * n,
        out_shape=[jax.ShapeDtypeStruct((N_DEV,) + a.shape, a.dtype) for a in arrs],
        scratch_shapes=[pltpu.SemaphoreType.DMA((n, N_DEV - 1)), pltpu.SemaphoreType.DMA((n, N_DEV - 1)),
                        pltpu.SemaphoreType.DMA((n,))],
    )(*arrs)


def _sibling_swap(arrs, name):
    n = len(arrs)

    def body(*refs):
        ins, outs = refs[:n], refs[n:2 * n]
        send_sems, recv_sems = refs[2 * n:]
        x, y, c, _ = _mesh_place()
        sibling = (x, y, 1 - c)
        cps = [_remote(ins[i].at[1 - c], outs[i], send_sems.at[i], recv_sems.at[i], sibling) for i in range(n)]
        for cp in cps:
            cp.start()
        for cp in cps:
            cp.wait()

    return pl.pallas_call(
        body, name=name, in_specs=[_ANY] * n, out_specs=[_ANY] * n,
        out_shape=[jax.ShapeDtypeStruct(a.shape[1:], a.dtype) for a in arrs],
        scratch_shapes=[pltpu.SemaphoreType.DMA((n,)), pltpu.SemaphoreType.DMA((n,))],
    )(*arrs)


def _chip_scatter(arrs, name):
    n = len(arrs)

    def body(*refs):
        ins, outs = refs[:n], refs[n:2 * n]
        send_sems, recv_sems, loc_sems = refs[2 * n:]
        x, y, c, chips = _mesh_place()
        mine = 2 * x + y
        local = []
        for i in range(n):
            cp = pltpu.make_async_copy(ins[i].at[mine], outs[i].at[mine], loc_sems.at[i])
            cp.start()
            local.append(cp)
        sends = []
        for i in range(n):
            for j, (px, py) in enumerate(chips):
                cp = _remote(ins[i].at[2 * px + py], outs[i].at[mine], send_sems.at[i, j], recv_sems.at[i, j],
                             (px, py, c))
                cp.start()
                sends.append(cp)
        for i in range(n):
            for j, (px, py) in enumerate(chips):
                got = outs[i].at[2 * px + py]
                _remote(got, got, send_sems.at[i, j], recv_sems.at[i, j], (px, py, c)).wait_recv()
        for cp in sends:
            cp.wait_send()
        for cp in local:
            cp.wait()

    return pl.pallas_call(
        body, name=name, in_specs=[_ANY] * n, out_specs=[_ANY] * n,
        out_shape=[jax.ShapeDtypeStruct(a.shape, a.dtype) for a in arrs],
        scratch_shapes=[pltpu.SemaphoreType.DMA((n, N_CHIPS - 1)), pltpu.SemaphoreType.DMA((n, N_CHIPS - 1)),
                        pltpu.SemaphoreType.DMA((n,))],
    )(*arrs)


def _pair_sum(own, got, name):
    _, _, rows, cols = own.shape
    tb = _row_tile(rows, cols, budget=512 * 1024)

    def body(c_ref, a_ref, b_ref, o_ref):
        o_ref[...] = (a_ref[...].astype(F32) + b_ref[...].astype(F32)).astype(o_ref.dtype)

    core = lax.axis_index("c").astype(jnp.int32).reshape(1)
    grid_spec = pltpu.PrefetchScalarGridSpec(
        num_scalar_prefetch=1, grid=(N_CHIPS, rows // tb),
        in_specs=[pl.BlockSpec((None, None, tb, cols), lambda p, i, c_ref: (c_ref[0], p, i, 0)),
                  pl.BlockSpec((None, tb, cols), lambda p, i, c_ref: (p, i, 0))],
        out_specs=pl.BlockSpec((None, tb, cols), lambda p, i, c_ref: (p, i, 0)))
    return pl.pallas_call(
        body, name=name, grid_spec=grid_spec, out_shape=jax.ShapeDtypeStruct(got.shape, got.dtype),
        compiler_params=_cparams(("parallel", "parallel")),
    )(core, own, got)


def _adamw_math(w, g, m, v):
    m = ADAM_B1 * m + (1.0 - ADAM_B1) * g
    v = ADAM_B2 * v + (1.0 - ADAM_B2) * (g * g)
    m_hat = m / (1.0 - ADAM_B1 ** ADAM_STEP)
    v_hat = v / (1.0 - ADAM_B2 ** ADAM_STEP)
    delta = -ADAM_LR * (m_hat / (jnp.sqrt(v_hat) + ADAM_EPS) + ADAM_WD * w)
    return delta, m, v


def _row_tile(rows, cols, budget=128 * 1024):
    if rows * cols <= budget or rows % SUBLANES:
        return rows
    best = SUBLANES
    for t in range(SUBLANES, rows + 1, SUBLANES):
        if rows % t == 0 and t * cols <= budget:
            best = t
    return best


def _reduce_adamw(parts, w, m, v, name):
    rows, cols = w.shape
    nparts = parts.shape[0]
    tb = _row_tile(rows, cols)

    def body(p_ref, w_ref, m_ref, v_ref, g_ref, d_ref, nm_ref, nv_ref):
        g = p_ref[0].astype(F32)
        for s in range(1, nparts):
            g = g + p_ref[s].astype(F32)
        delta, nm, nv = _adamw_math(w_ref[...], g, m_ref[...], v_ref[...])
        g_ref[...] = g
        d_ref[...] = delta
        nm_ref[...] = nm
        nv_ref[...] = nv

    spec = pl.BlockSpec((tb, cols), lambda i: (i, 0))
    shp = jax.ShapeDtypeStruct((rows, cols), F32)
    return pl.pallas_call(
        body, name=name, grid=(rows // tb,),
        in_specs=[pl.BlockSpec((nparts, tb, cols), lambda i: (0, i, 0)), spec, spec, spec],
        out_specs=[spec] * 4, out_shape=[shp] * 4,
        compiler_params=_cparams(("parallel",)),
    )(parts, w, m, v)


def _sum_parts(parts, name):
    nparts, rows, cols = parts.shape
    tb = _row_tile(rows, cols)

    def body(p_ref, g_ref):
        g = p_ref[0]
        for s in range(1, nparts):
            g = g + p_ref[s]
        g_ref[...] = g

    return pl.pallas_call(
        body, name=name, grid=(rows // tb,),
        in_specs=[pl.BlockSpec((nparts, tb, cols), lambda i: (0, i, 0))],
        out_specs=pl.BlockSpec((tb, cols), lambda i: (i, 0)),
        out_shape=jax.ShapeDtypeStruct((rows, cols), F32),
        compiler_params=_cparams(("parallel",)),
    )(parts)


def _adamw(g, w, m, v, name):
    rows, cols = w.shape
    tb = _row_tile(rows, cols)

    def body(g_ref, w_ref, m_ref, v_ref, d_ref, nm_ref, nv_ref):
        delta, nm, nv = _adamw_math(w_ref[...], g_ref[...], m_ref[...], v_ref[...])
        d_ref[...] = delta
        nm_ref[...] = nm
        nv_ref[...] = nv

    spec = pl.BlockSpec((tb, cols), lambda i: (i, 0))
    shp = jax.ShapeDtypeStruct((rows, cols), F32)
    return pl.pallas_call(
        body, name=name, grid=(rows // tb,), in_specs=[spec] * 4, out_specs=[spec] * 3, out_shape=[shp] * 3,
        compiler_params=_cparams(("parallel",)),
    )(g, w, m, v)


def _pack(arrs):
    flat = jnp.concatenate([a.reshape(-1) for a in arrs])
    pad = (-flat.shape[0]) % (PACK_ROWS * LANES)
    return jnp.pad(flat, (0, pad)).reshape(-1, LANES)


def _unpack(packed, shapes):
    flat = packed.reshape(-1)
    out, off = [], 0
    for s in shapes:
        n = math.prod(s)
        out.append(flat[off:off + n].reshape(s))
        off += n
    return out


def _block_diag_b(bt):
    bb = bt.transpose(1, 0, 2).reshape(S5_SLABS, 8, S5_GROUP, S5_STATE)
    eye = jnp.eye(8, dtype=bt.dtype)
    return (bb[:, :, :, None, :] * eye[None, :, None, :, None]).reshape(S5_SLABS, LANES, S5_SLAB_STATE)


def _block_diag_b_grad(g):
    g5 = g.reshape(S5_SLABS, 8, S5_GROUP, 8, S5_STATE)
    diag = jnp.stack([g5[:, a, :, a, :] for a in range(8)], axis=1)
    return diag.reshape(S5_GROUPS, S5_GROUP, S5_STATE).transpose(1, 0, 2)


def _block_diag_c(cw):
    cc = cw.reshape(S5_SLABS, 8, S5_GROUP, S5_STATE).transpose(0, 1, 3, 2)
    eye = jnp.eye(8, dtype=cw.dtype)
    return (cc[:, :, :, None, :] * eye[None, :, None, :, None]).reshape(S5_SLABS, S5_SLAB_STATE, LANES)


def _block_diag_c_grad(g):
    g5 = g.reshape(S5_SLABS, 8, S5_STATE, 8, S5_GROUP)
    diag = jnp.stack([g5[:, a, :, a, :] for a in range(8)], axis=1)
    return diag.transpose(0, 1, 3, 2).reshape(S5_GROUPS, S5_GROUP, S5_STATE)


def _cols_full(gathered):
    _, k, n = gathered.shape
    return gathered.transpose(1, 0, 2).reshape(k, N_DEV * n)


def _cols_split(full):
    k, n8 = full.shape
    return full.reshape(k, N_DEV, n8 // N_DEV).transpose(1, 0, 2)


SMALL_NAMES = ("mix_norm_w", "s5_log_dt", "s5_a_re", "s5_a_im", "s5_b_re", "s5_b_im", "s5_c_re", "s5_c_im", "s5_d",
               "dn_a_log", "dn_dt_bias", "dn_norm_w", "ffn_norm_w", "final_norm_w")
CONV_NAMES = ("dn_conv_w", "ffn_conv_w")
BIG_NAMES = ("w_in", "s5_glu_w", "dn_proj_w", "w_out", "ffn_up", "ffn_down")
ROW_SHARDED = ("w_out", "ffn_down")
WEIGHT_ORDER = ("mix_norm_w", "w_in", "s5_log_dt", "s5_a_re", "s5_a_im", "s5_b_re", "s5_b_im", "s5_c_re", "s5_c_im",
                "s5_d", "s5_glu_w", "dn_conv_w", "dn_a_log", "dn_dt_bias", "dn_norm_w", "dn_proj_w", "w_out",
                "ffn_norm_w", "ffn_up", "ffn_conv_w", "ffn_down", "final_norm_w")


def _layer_forward(l, x, wts, sm):
    rows = x.shape[0]
    sv = {"x0": x}
    nm = f"l{l}_"
    mixw = sm["mix_norm_w"][l][None]
    (h,) = _tile_fwd(nm + "mix_norm", _f_rms, [(x, None, 0)], [(mixw, None, 0)], [(D_MODEL, None, BF16)], rows, 256)
    proj = _mm(h, wts["w_in_main"], "nn", F32, nm + "proj")
    ba = _mm(h, wts["w_in_ba"], "nn", F32, nm + "proj_ba")
    sv.update(h=h, proj=proj, ba=ba)
    disc = _s5_disc_fwd(sm["s5_log_dt"][l][:, None], sm["s5_a_re"][l], sm["s5_a_im"][l],
                        sm["s5_b_re"][l].transpose(2, 0, 1), sm["s5_b_im"][l].transpose(2, 0, 1), nm + "s5_disc")
    abar_re, abar_im, bbar_re, bbar_im = disc
    s5p = dict(
        bd_re=_block_diag_b(bbar_re).astype(BF16), bd_im=_block_diag_b(bbar_im).astype(BF16),
        cd_re=_block_diag_c(sm["s5_c_re"][l]).astype(BF16), cd_im=_block_diag_c(sm["s5_c_im"][l]).astype(BF16),
        a_re=abar_re.reshape(S5_SLABS, 1, S5_SLAB_STATE), a_im=abar_im.reshape(S5_SLABS, 1, S5_SLAB_STATE),
        d=sm["s5_d"][l][None])
    y_s5, st_re, st_im = _s5_fwd(proj, s5p["bd_re"], s5p["bd_im"], s5p["cd_re"], s5p["cd_im"],
                                 s5p["a_re"], s5p["a_im"], s5p["d"], nm + "s5_scan")
    glu = _mm(y_s5, wts["s5_glu_w"], "nn", F32, nm + "glu")
    sv.update(s5p=s5p, y_s5=y_s5, st_re=st_re, st_im=st_im, glu=glu)
    conv = _conv_fwd(proj, wts["dn_conv_w"], DN_CONV, nm + "dn_conv", x_off=2, width=DN_QKV, cw=512)
    q, = _tile_fwd(nm + "dn_q", _f_dn_q, [(conv, LANES, 0)], [], [(1024, LANES, F32)], rows, 512, ncol=8)
    k, = _tile_fwd(nm + "dn_k", _f_dn_k, [(conv, LANES, 8)], [], [(1024, LANES, F32)], rows, 512, ncol=8)
    v, = _tile_fwd(nm + "dn_v", _f_dn_v, [(conv, LANES, 16)], [], [(1024, LANES, F32)], rows, 512, ncol=8)
    qkv = jnp.concatenate([q, k, v], axis=1)
    pad8 = lambda a: jnp.pad(a[None], ((0, 0), (DN_HEADS, LANES - 2 * DN_HEADS)))
    alog, dtb = pad8(sm["dn_a_log"][l]), pad8(sm["dn_dt_bias"][l])
    bg, = _tile_fwd(nm + "dn_gates", _f_dn_gates, [(ba, None, 0)], [(alog, None, 0), (dtb, None, 0)],
                    [(LANES, None, F32)], rows, 512)
    bcol = bg[:, 0:DN_HEADS].T[:, :, None]
    gcol = bg[:, DN_HEADS:2 * DN_HEADS].T[:, :, None]
    o, states = _dn_fwd(qkv, gcol, bcol, nm + "dn_chunk")
    dnw = sm["dn_norm_w"][l][None]
    y_dn, = _tile_fwd(nm + "dn_post", _f_dn_post, [(o, LANES, 0), (proj, LANES, MAIN_Z // LANES)], [(dnw, None, 0)],
                      [(1024, LANES, BF16)], rows, 512, ncol=8)
    br_dn = _mm(y_dn, wts["dn_proj_w"], "nn", F32, nm + "dn_proj")
    sv.update(conv=conv, qkv=qkv, alog=alog, dtb=dtb, bcol=bcol, gcol=gcol, o=o, states=states, dnw=dnw,
              y_dn=y_dn, br_dn=br_dn)
    cw = 512
    merged, = _tile_fwd(nm + "merge", _f_merge,
                        [(glu, cw, 0), (glu, cw, D_MODEL // cw), (br_dn, cw, 0),
                         (proj, cw, MAIN_GS // cw), (proj, cw, MAIN_GD // cw)], [],
                        [(D_MODEL, cw, BF16)], rows, 512, ncol=D_MODEL // cw)
    x1 = _mm(merged, wts["w_out"], "nn", F32, nm + "w_out", res=x)
    sv.update(merged=merged, x1=x1)
    ffw = sm["ffn_norm_w"][l][None]
    (h2,) = _tile_fwd(nm + "ffn_norm", _f_rms, [(x1, None, 0)], [(ffw, None, 0)], [(D_MODEL, None, BF16)], rows, 256)
    up = _mm(h2, wts["ffn_up"], "nn", F32, nm + "ffn_up")
    upc = _conv_fwd(up, wts["ffn_conv_w"], FFN_CONV, nm + "ffn_conv", cw=512)
    hid, = _tile_fwd(nm + "ffn_gate", _f_ffn_gate, [(upc, cw, 0), (upc, cw, FFN_DIM // cw)], [],
                     [(FFN_DIM, cw, BF16)], rows, 512, ncol=FFN_DIM // cw)
    x2 = _mm(hid, wts["ffn_down"], "nn", F32, nm + "ffn_down", res=x1, tk=1408)
    sv.update(h2=h2, up=up, upc=upc, hid=hid)
    return x2, sv


def _layer_backward(l, dx2, wts, sm, sv):
    rows = dx2.shape[0]
    nm = f"l{l}_b_"
    cw = 512
    gr = {}
    dxb = dx2.astype(BF16)
    gr["ffn_down"] = _mm(sv["hid"], dxb, "tn", BF16, nm + "ffn_down_w", tm=1408)
    dhid = _mm(dxb, wts["ffn_down"], "nt", F32, nm + "ffn_down_x", tn=1408)
    (dact, dval), _ = _tile_bwd(nm + "ffn_gate", _f_ffn_gate,
                                [(sv["upc"], cw, 0), (sv["upc"], cw, FFN_DIM // cw)], [], [(dhid, cw, 0)],
                                [F32, F32], rows, 512, ncol=FFN_DIM // cw)
    dupc = jnp.concatenate([dact, dval], axis=1)
    dup, dconv = _conv_bwd(sv["up"], wts["ffn_conv_w"], dupc, FFN_CONV, nm + "ffn_conv", cw=512)
    gr["ffn_conv_w"] = dconv[:FFN_CONV]
    dupb = dup.astype(BF16)
    gr["ffn_up"] = _mm(sv["h2"], dupb, "tn", BF16, nm + "ffn_up_w", tk=1024)
    dh2 = _mm(dupb, wts["ffn_up"], "nt", F32, nm + "ffn_up_x", tk=1024)
    ffw = sm["ffn_norm_w"][l][None]
    (dx1n,), (dffw,) = _tile_bwd(nm + "ffn_norm", _f_rms, [(sv["x1"], None, 0)], [(ffw, None, 0)],
                                 [(dh2, None, 0)], [F32], rows, 256)
    gr["ffn_norm_w"] = dffw[0]
    dx1 = dx2 + dx1n
    dx1b = dx1.astype(BF16)
    gr["w_out"] = _mm(sv["merged"], dx1b, "tn", BF16, nm + "w_out_w", tk=1024)
    dmerged = _mm(dx1b, wts["w_out"], "nt", F32, nm + "w_out_x")
    (dga, dgb, dbr, dgs, dgd), _ = _tile_bwd(
        nm + "merge", _f_merge,
        [(sv["glu"], cw, 0), (sv["glu"], cw, D_MODEL // cw), (sv["br_dn"], cw, 0),
         (sv["proj"], cw, MAIN_GS // cw), (sv["proj"], cw, MAIN_GD // cw)], [], [(dmerged, cw, 0)],
        [BF16, BF16, BF16, BF16, BF16], rows, 512, ncol=D_MODEL // cw)
    dglu = jnp.concatenate([dga, dgb], axis=1)
    gr["s5_glu_w"] = _mm(sv["y_s5"], dglu, "tn", BF16, nm + "glu_w", tk=1024)
    dy_s5 = _mm(dglu, wts["s5_glu_w"], "nt", F32, nm + "glu_x")
    gr["dn_proj_w"] = _mm(sv["y_dn"], dbr, "tn", BF16, nm + "dn_proj_w", tk=1024)
    dy_dn = _mm(dbr, wts["dn_proj_w"], "nt", F32, nm + "dn_proj_x")
    s5p = sv["s5p"]
    du, gbr, gbi, gcr, gci, gar, gai, gd = _s5_bwd(
        sv["proj"], dy_s5, sv["st_re"], sv["st_im"], s5p["bd_re"], s5p["bd_im"], s5p["cd_re"], s5p["cd_im"],
        s5p["a_re"], s5p["a_im"], s5p["d"], nm + "s5_scan")
    gr["s5_d"] = gd[0]
    gr["s5_c_re"] = _block_diag_c_grad(gcr)
    gr["s5_c_im"] = _block_diag_c_grad(gci)
    bt_re, bt_im = sm["s5_b_re"][l].transpose(2, 0, 1), sm["s5_b_im"][l].transpose(2, 0, 1)
    dldt, dare, daim, dbtr, dbti = _s5_disc_bwd(
        sm["s5_log_dt"][l][:, None], sm["s5_a_re"][l], sm["s5_a_im"][l], bt_re, bt_im,
        [gar.reshape(S5_GROUPS, S5_STATE), gai.reshape(S5_GROUPS, S5_STATE),
         _block_diag_b_grad(gbr), _block_diag_b_grad(gbi)], nm + "s5_disc")
    gr.update(s5_log_dt=dldt[:, 0], s5_a_re=dare, s5_a_im=daim,
              s5_b_re=dbtr.transpose(1, 2, 0), s5_b_im=dbti.transpose(1, 2, 0))
    (do, dz), (ddnw,) = _tile_bwd(nm + "dn_post", _f_dn_post,
                                  [(sv["o"], LANES, 0), (sv["proj"], LANES, MAIN_Z // LANES)],
                                  [(sv["dnw"], None, 0)], [(dy_dn, LANES, 0)], [F32, BF16], rows, 512, ncol=8)
    gr["dn_norm_w"] = ddnw[0]
    dq, dk, dv, dgc, dbc = _dn_bwd(sv["qkv"], sv["gcol"], sv["bcol"], sv["states"], do, nm + "dn_chunk")
    dbg = jnp.pad(jnp.concatenate([dbc[:, :, 0].T, dgc[:, :, 0].T], axis=1), ((0, 0), (0, LANES - 2 * DN_HEADS)))
    (dba,), (dalog, ddtb) = _tile_bwd(nm + "dn_gates", _f_dn_gates, [(sv["ba"], None, 0)],
                                      [(sv["alog"], None, 0), (sv["dtb"], None, 0)], [(dbg, None, 0)],
                                      [BF16], rows, 512)
    gr["dn_a_log"] = dalog[0, DN_HEADS:2 * DN_HEADS]
    gr["dn_dt_bias"] = ddtb[0, DN_HEADS:2 * DN_HEADS]
    (dcq,), _ = _tile_bwd(nm + "dn_q", _f_dn_q, [(sv["conv"], LANES, 0)], [], [(dq, LANES, 0)], [F32], rows, 512, 8)
    (dck,), _ = _tile_bwd(nm + "dn_k", _f_dn_k, [(sv["conv"], LANES, 8)], [], [(dk, LANES, 0)], [F32], rows, 512, 8)
    (dcv,), _ = _tile_bwd(nm + "dn_v", _f_dn_v, [(sv["conv"], LANES, 16)], [], [(dv, LANES, 0)], [F32], rows, 512, 8)
    dconv_out = jnp.concatenate([dcq, dck, dcv], axis=1)
    dqkv, ddnconv = _conv_bwd(sv["proj"], wts["dn_conv_w"], dconv_out, DN_CONV, nm + "dn_conv", x_off=2, cw=512)
    gr["dn_conv_w"] = ddnconv[:DN_CONV]
    dproj = jnp.concatenate([du.astype(BF16), dqkv.astype(BF16), dz, dgs, dgd], axis=1)
    gmain = _mm(sv["h"], dproj, "tn", BF16, nm + "proj_w", tk=1024)
    gba = _mm(sv["h"], dba, "tn", BF16, nm + "proj_ba_w", tk=1024)
    gr["w_in"] = jnp.concatenate([gmain[:, :OFF_BA], gba[:, :2 * DN_HEADS], gmain[:, OFF_BA:]], axis=1)
    dh = _mm(dproj, wts["w_in_main"], "nt", F32, nm + "proj_x", tk=1024)
    dh = _mm(dba, wts["w_in_ba"], "nt", F32, nm + "proj_ba_x", res=dh)
    mixw = sm["mix_norm_w"][l][None]
    (dx0n,), (dmixw,) = _tile_bwd(nm + "mix_norm", _f_rms, [(sv["x0"], None, 0)], [(mixw, None, 0)],
                                  [(dh, None, 0)], [F32], rows, 256)
    gr["mix_norm_w"] = dmixw[0]
    return dx1 + dx0n, gr


def kernel(x, mix_norm_w, w_in, s5_log_dt, s5_a_re, s5_a_im, s5_b_re, s5_b_im, s5_c_re, s5_c_im, s5_d, s5_glu_w, dn_conv_w, dn_a_log, dn_dt_bias, dn_norm_w, dn_proj_w, w_out, ffn_norm_w, ffn_up, ffn_conv_w, ffn_down, final_norm_w, loss_target, m_mix_norm_w, m_w_in, m_s5_log_dt, m_s5_a_re, m_s5_a_im, m_s5_b_re, m_s5_b_im, m_s5_c_re, m_s5_c_im, m_s5_d, m_s5_glu_w, m_dn_conv_w, m_dn_a_log, m_dn_dt_bias, m_dn_norm_w, m_dn_proj_w, m_w_out, m_ffn_norm_w, m_ffn_up, m_ffn_conv_w, m_ffn_down, m_final_norm_w, v_mix_norm_w, v_w_in, v_s5_log_dt, v_s5_a_re, v_s5_a_im, v_s5_b_re, v_s5_b_im, v_s5_c_re, v_s5_c_im, v_s5_d, v_s5_glu_w, v_dn_conv_w, v_dn_a_log, v_dn_dt_bias, v_dn_norm_w, v_dn_proj_w, v_w_out, v_ffn_norm_w, v_ffn_up, v_ffn_conv_w, v_ffn_down, v_final_norm_w):
    w = dict(mix_norm_w=mix_norm_w, w_in=w_in, s5_log_dt=s5_log_dt, s5_a_re=s5_a_re, s5_a_im=s5_a_im, s5_b_re=s5_b_re, s5_b_im=s5_b_im, s5_c_re=s5_c_re, s5_c_im=s5_c_im, s5_d=s5_d, s5_glu_w=s5_glu_w, dn_conv_w=dn_conv_w, dn_a_log=dn_a_log, dn_dt_bias=dn_dt_bias, dn_norm_w=dn_norm_w, dn_proj_w=dn_proj_w, w_out=w_out, ffn_norm_w=ffn_norm_w, ffn_up=ffn_up, ffn_conv_w=ffn_conv_w, ffn_down=ffn_down, final_norm_w=final_norm_w)
    mo = dict(mix_norm_w=m_mix_norm_w, w_in=m_w_in, s5_log_dt=m_s5_log_dt, s5_a_re=m_s5_a_re, s5_a_im=m_s5_a_im, s5_b_re=m_s5_b_re, s5_b_im=m_s5_b_im, s5_c_re=m_s5_c_re, s5_c_im=m_s5_c_im, s5_d=m_s5_d, s5_glu_w=m_s5_glu_w, dn_conv_w=m_dn_conv_w, dn_a_log=m_dn_a_log, dn_dt_bias=m_dn_dt_bias, dn_norm_w=m_dn_norm_w, dn_proj_w=m_dn_proj_w, w_out=m_w_out, ffn_norm_w=m_ffn_norm_w, ffn_up=m_ffn_up, ffn_conv_w=m_ffn_conv_w, ffn_down=m_ffn_down, final_norm_w=m_final_norm_w)
    vo = dict(mix_norm_w=v_mix_norm_w, w_in=v_w_in, s5_log_dt=v_s5_log_dt, s5_a_re=v_s5_a_re, s5_a_im=v_s5_a_im, s5_b_re=v_s5_b_re, s5_b_im=v_s5_b_im, s5_c_re=v_s5_c_re, s5_c_im=v_s5_c_im, s5_d=v_s5_d, s5_glu_w=v_s5_glu_w, dn_conv_w=v_dn_conv_w, dn_a_log=v_dn_a_log, dn_dt_bias=v_dn_dt_bias, dn_norm_w=v_dn_norm_w, dn_proj_w=v_dn_proj_w, w_out=v_w_out, ffn_norm_w=v_ffn_norm_w, ffn_up=v_ffn_up, ffn_conv_w=v_ffn_conv_w, ffn_down=v_ffn_down, final_norm_w=v_final_norm_w)
    depth = w_in.shape[0]
    me = 4 * lax.axis_index("x") + 2 * lax.axis_index("y") + lax.axis_index("c")
    xs = x[0]
    target = loss_target[0]

    gather_names = BIG_NAMES + CONV_NAMES
    gathered = _all_gather([w[n].astype(BF16) if n in BIG_NAMES else w[n] for n in gather_names], "gather_weights")
    full = {}
    for n, g in zip(gather_names, gathered):
        if n in ROW_SHARDED:
            full[n] = g.transpose(1, 0, 2, 3).reshape(depth, -1, g.shape[-1])
        else:
            full[n] = g.transpose(1, 2, 0, 3).reshape(depth, g.shape[2], -1)
    layer_w = []
    for l in range(depth):
        wi = full["w_in"][l]
        lw = {n: full[n][l] for n in gather_names if n != "w_in"}
        lw["w_in_main"] = jnp.concatenate([wi[:, :OFF_BA], wi[:, OFF_GS:]], axis=1)
        lw["w_in_ba"] = jnp.pad(wi[:, OFF_BA:OFF_GS], ((0, 0), (0, LANES - 2 * DN_HEADS)))
        layer_w.append(lw)

    saved = []
    h = xs
    for l in range(depth):
        h, sv = _layer_forward(l, h, layer_w[l], w)
        saved.append(sv)
    dx, dfinal, loss_tile = _final_loss(h, final_norm_w[None], target, "final_loss")

    grads = [None] * depth
    for l in reversed(range(depth)):
        dx, grads[l] = _layer_backward(l, dx, layer_w[l], w, saved[l])

    def stacked(n):
        return jnp.stack([grads[l][n] for l in range(depth)])

    big_send = []
    for n in BIG_NAMES:
        g = stacked(n)
        if n in ROW_SHARDED:
            g = g.reshape(depth, N_CHIPS, 2, g.shape[1] // N_DEV, g.shape[2]).transpose(2, 1, 0, 3, 4)
        else:
            g = g.reshape(depth, g.shape[1], N_CHIPS, 2, g.shape[2] // N_DEV).transpose(3, 2, 0, 1, 4)
        big_send.append(g.reshape(2, N_CHIPS, -1, g.shape[-1]))
    from_sibling = _sibling_swap(big_send, "swap_grads")
    chip_partials = [_pair_sum(own, got, "pair_sum_" + n) for n, own, got in zip(BIG_NAMES, big_send, from_sibling)]
    big_recv = _chip_scatter(chip_partials, "scatter_grads")
    small_list = [stacked(n) for n in SMALL_NAMES if n != "final_norm_w"] + [dfinal[0]]
    small_list += [stacked(n) for n in CONV_NAMES] + [loss_tile[0, 0:1]]
    small_shapes = [a.shape for a in small_list]
    (small_recv,) = _all_gather([_pack(small_list)], "gather_small")
    small_sum = _unpack(_sum_parts(small_recv, "sum_small"), small_shapes)
    loss = small_sum[-1][0]
    small_names = [n for n in SMALL_NAMES if n != "final_norm_w"] + ["final_norm_w"]
    g_out = dict(zip(small_names, small_sum[:len(small_names)]))
    for n, gfull in zip(CONV_NAMES, small_sum[len(small_names):len(small_names) + 2]):
        shard = w[n].shape[-1]
        g_out[n] = lax.dynamic_slice_in_dim(gfull, me * shard, shard, axis=2)

    d_out, m_out, v_out = {}, {}, {}
    for n, parts in zip(BIG_NAMES, big_recv):
        shp = w[n].shape
        two = lambda a: a.reshape(-1, shp[-1])
        g2, d2, m2, v2 = _reduce_adamw(parts, two(w[n]), two(mo[n]), two(vo[n]), "adamw_" + n)
        g_out[n], d_out[n], m_out[n], v_out[n] = (a.reshape(shp) for a in (g2, d2, m2, v2))
    rest = list(small_names) + list(CONV_NAMES)
    rest_shapes = [w[n].shape for n in rest]
    d2, m2, v2 = _adamw(_pack([g_out[n] for n in rest]), _pack([w[n] for n in rest]), _pack([mo[n] for n in rest]),
                        _pack([vo[n] for n in rest]), "adamw_small")
    for n, d, m_, v_ in zip(rest, _unpack(d2, rest_shapes), _unpack(m2, rest_shapes), _unpack(v2, rest_shapes)):
        d_out[n], m_out[n], v_out[n] = d, m_, v_

    return (loss, dx[None], *[g_out[n] for n in WEIGHT_ORDER], *[d_out[n] for n in WEIGHT_ORDER],
            *[m_out[n] for n in WEIGHT_ORDER], *[v_out[n] for n in WEIGHT_ORDER])
```

```python
import functools
import math

import jax
import jax.numpy as jnp
from jax import lax
from jax.experimental import pallas as pl
from jax.experimental.pallas import tpu as pltpu

F32 = jnp.float32
BF16 = jnp.bfloat16

D_MODEL = 2048
DEPTH = 4
S5_WIDTH = 1024
S5_GROUP = 16
S5_GROUPS = 64
S5_STATE = 64
DN_HEADS = 8
DN_DK = 128
DN_QKV = 3072
DN_CONV = 4
DN_CHUNK = 64
FFN_DIM = 5632
FFN_CONV = 3
NORM_EPS = 1e-6
N_IN = 9232
OFF_Z = 4096
OFF_BA = 5120
OFF_GS = 5136
N_MAIN = 9216
MAIN_Z = 4096
MAIN_GS = 5120
MAIN_GD = 7168

ADAM_LR = 0.001
ADAM_B1 = 0.9
ADAM_B2 = 0.999
ADAM_EPS = 1e-08
ADAM_WD = 0.01
ADAM_STEP = 10

N_DEV = 8
LANES = 128
SUBLANES = 8
VMEM_LIMIT_BYTES = 48 * 1024 * 1024

S5_SLABS = 8
S5_SLAB_STATE = 512
S5_TB = 256
DN_TB = 512
PACK_ROWS = 512


def _cparams(sem):
    return pltpu.CompilerParams(dimension_semantics=sem, vmem_limit_bytes=VMEM_LIMIT_BYTES)


def _dot(a, b, dims, precision=None):
    return lax.dot_general(a, b, (dims, ((), ())), precision=precision, preferred_element_type=F32)


_NN = ((1,), (0,))
_NT = ((1,), (1,))
_TN = ((0,), (0,))


def _mm(a, b, mode, out_dtype, name, res=None, tm=1024, tn=1024, tk=2048):
    if mode == "nn":
        (m, k), (_, n) = a.shape, b.shape
    elif mode == "nt":
        (m, k), (n, _) = a.shape, b.shape
    else:
        (k, m), (_, n) = a.shape, b.shape
    tm, tn, tk = min(tm, m), min(tn, n), min(tk, k)
    assert m % tm == 0 and n % tn == 0 and k % tk == 0, (name, a.shape, b.shape)
    nk = k // tk
    if mode == "tn":
        a_spec = pl.BlockSpec((tk, tm), lambda i, j, kk: (kk, i))
    else:
        a_spec = pl.BlockSpec((tm, tk), lambda i, j, kk: (i, kk))
    if mode == "nt":
        b_spec = pl.BlockSpec((tn, tk), lambda i, j, kk: (j, kk))
    else:
        b_spec = pl.BlockSpec((tk, tn), lambda i, j, kk: (kk, j))
    dims = {"nn": _NN, "nt": _NT, "tn": _TN}[mode]
    o_spec = pl.BlockSpec((tm, tn), lambda i, j, kk: (i, j))
    has_res = res is not None

    def body(*refs):
        if has_res:
            a_ref, b_ref, r_ref, o_ref, acc = refs
        else:
            a_ref, b_ref, o_ref, acc = refs
        p = _dot(a_ref[...], b_ref[...], dims)

        def finish(total):
            if has_res:
                total = total + r_ref[...]
            o_ref[...] = total.astype(out_dtype)

        if nk == 1:
            finish(p)
        else:
            kk = pl.program_id(2)

            @pl.when(kk == 0)
            def _():
                acc[...] = p

            @pl.when(jnp.logical_and(kk > 0, kk < nk - 1))
            def _():
                acc[...] += p

            @pl.when(kk == nk - 1)
            def _():
                finish(acc[...] + p)

    in_specs = [a_spec, b_spec] + ([o_spec] if has_res else [])
    args = (a, b) + ((res,) if has_res else ())
    return pl.pallas_call(
        body, name=name, grid=(m // tm, n // tn, nk), in_specs=in_specs, out_specs=o_spec,
        out_shape=jax.ShapeDtypeStruct((m, n), out_dtype),
        scratch_shapes=[pltpu.VMEM((tm, tn) if nk > 1 else (SUBLANES, LANES), F32)],
        compiler_params=_cparams(("parallel", "parallel", "arbitrary")),
    )(*args)


def _row_spec(tb, width, cw, off):
    if cw is None:
        return pl.BlockSpec((tb, width), lambda j, i: (i, 0))
    return pl.BlockSpec((tb, cw), lambda j, i: (i, j + off))


def _par_spec(rows, width, cw, off):
    if cw is None:
        return pl.BlockSpec((rows, width), lambda j, i: (0, 0))
    return pl.BlockSpec((rows, cw), lambda j, i: (0, j + off))


def _tile_fwd(name, fn, tiled, params, outs, rows, tb, ncol=1):
    tb = min(tb, rows)
    nt, npar = len(tiled), len(params)

    def body(*refs):
        vals = [r[...] for r in refs[:nt + npar]]
        res = fn(*vals)
        for o_ref, r in zip(refs[nt + npar:], res):
            o_ref[...] = r.astype(o_ref.dtype)

    in_specs = [_row_spec(tb, a.shape[1], cw, off) for a, cw, off in tiled]
    in_specs += [_par_spec(a.shape[0], a.shape[1], cw, off) for a, cw, off in params]
    out_specs = [_row_spec(tb, w, cw, 0) for w, cw, _ in outs]
    out_shape = [jax.ShapeDtypeStruct((rows, w), dt) for w, _, dt in outs]
    return pl.pallas_call(
        body, name=name, grid=(ncol, rows // tb), in_specs=in_specs, out_specs=out_specs, out_shape=out_shape,
        compiler_params=_cparams(("parallel", "parallel")),
    )(*[a for a, _, _ in tiled], *[a for a, _, _ in params])


def _tile_bwd(name, fn, tiled, params, cots, gdtypes, rows, tb, ncol=1):
    tb = min(tb, rows)
    nt, npar, nc = len(tiled), len(params), len(cots)
    want = [i for i, g in enumerate(gdtypes) if g is not None]

    def body(*refs):
        vals = [r[...] for r in refs[:nt + npar]]
        cot_refs = refs[nt + npar:nt + npar + nc]
        g_refs = refs[nt + npar + nc:nt + npar + nc + len(want)]
        p_refs = refs[nt + npar + nc + len(want):]
        _, vjp = jax.vjp(fn, *vals)
        grads = vjp(tuple(c[...].astype(F32) for c in cot_refs))
        for g_ref, i in zip(g_refs, want):
            g_ref[...] = grads[i].astype(g_ref.dtype)
        jcol, irow = pl.program_id(0), pl.program_id(1)
        for p_ref, g, (_, cw, _) in zip(p_refs, grads[nt:], params):
            first = (irow == 0) if cw is not None else jnp.logical_and(irow == 0, jcol == 0)

            @pl.when(first)
            def _():
                p_ref[...] = g

            @pl.when(jnp.logical_not(first))
            def _():
                p_ref[...] += g

    in_specs = [_row_spec(tb, a.shape[1], cw, off) for a, cw, off in tiled]
    in_specs += [_par_spec(a.shape[0], a.shape[1], cw, off) for a, cw, off in params]
    in_specs += [_row_spec(tb, a.shape[1], cw, off) for a, cw, off in cots]
    out_specs, out_shape = [], []
    for i in want:
        a, cw, _ = tiled[i]
        width = a.shape[1] if cw is None else ncol * cw
        out_specs.append(_row_spec(tb, width, cw, 0))
        out_shape.append(jax.ShapeDtypeStruct((rows, width), gdtypes[i]))
    for a, cw, _ in params:
        width = a.shape[1] if cw is None else ncol * cw
        out_specs.append(_par_spec(a.shape[0], width, cw, 0))
        out_shape.append(jax.ShapeDtypeStruct((a.shape[0], width), F32))
    res = pl.pallas_call(
        body, name=name, grid=(ncol, rows // tb), in_specs=in_specs, out_specs=out_specs, out_shape=out_shape,
        compiler_params=_cparams(("arbitrary", "arbitrary")),
    )(*[a for a, _, _ in tiled], *[a for a, _, _ in params], *[a for a, _, _ in cots])
    return res[:len(want)], res[len(want):]


def _sigmoid(x):
    return 1.0 / (1.0 + jnp.exp(-x))


def _silu(x):
    return x * _sigmoid(x)


def _softplus(x):
    return jnp.maximum(x, 0.0) + jnp.log1p(jnp.exp(-jnp.abs(x)))


def _f_rms(x, w):
    return (x * lax.rsqrt(jnp.mean(x * x, axis=-1, keepdims=True) + NORM_EPS) * w,)


def _f_merge(glu_a, glu_b, br_dn, gs, gd):
    return (_sigmoid(gs) * (glu_a * _sigmoid(glu_b)) + _sigmoid(gd) * br_dn,)


def _f_ffn_gate(act, val):
    return (_silu(act) * val,)


def _l2n(x):
    return x * lax.rsqrt(jnp.sum(x * x, axis=-1, keepdims=True) + NORM_EPS)


def _f_dn_q(c):
    return (_l2n(_silu(c)) * (DN_DK ** -0.5),)


def _f_dn_k(c):
    return (_l2n(_silu(c)),)


def _f_dn_v(c):
    return (_silu(c),)


def _f_dn_gates(ba, a_log, dt_bias):
    col = lax.broadcasted_iota(jnp.int32, ba.shape, 1)
    beta = _sigmoid(ba)
    g = -jnp.exp(a_log) * _softplus(ba + dt_bias)
    return (jnp.where(col < DN_HEADS, beta, jnp.where(col < 2 * DN_HEADS, g, 0.0)),)


def _f_dn_post(o, z, w):
    return (_f_rms(o, w)[0] * _silu(z),)


def _shift_down(x, halo, s, tb):
    if s == 0:
        return x
    y = pltpu.roll(x, s, 0)
    row8 = lax.broadcasted_iota(jnp.int32, halo.shape, 0)
    top = jnp.where(row8 < s, pltpu.roll(halo, s, 0), y[0:SUBLANES])
    if tb == SUBLANES:
        return top
    return jnp.concatenate([top, y[SUBLANES:]], axis=0)


def _shift_up(x, halo, s, tb):
    if s == 0:
        return x
    y = pltpu.roll(x, tb - s, 0)
    row8 = lax.broadcasted_iota(jnp.int32, halo.shape, 0)
    bot = jnp.where(row8 >= SUBLANES - s, pltpu.roll(halo, SUBLANES - s, 0), y[tb - SUBLANES:])
    if tb == SUBLANES:
        return bot
    return jnp.concatenate([y[:tb - SUBLANES], bot], axis=0)


def _conv_fwd(x, w, kw, name, x_off=0, width=None, cw=512, tb=512):
    rows = x.shape[0]
    width = w.shape[1] if width is None else width
    tb = min(tb, rows)
    nb = tb // SUBLANES

    def body(x_ref, h_ref, w_ref, o_ref):
        i = pl.program_id(1)
        xv = x_ref[...]
        halo = jnp.where(i > 0, h_ref[...], 0.0)
        acc = w_ref[kw - 1:kw, :] * xv
        for s in range(1, kw):
            acc = acc + w_ref[kw - 1 - s:kw - s, :] * _shift_down(xv, halo, s, tb)
        o_ref[...] = acc

    return pl.pallas_call(
        body, name=name, grid=(width // cw, rows // tb),
        in_specs=[pl.BlockSpec((tb, cw), lambda j, i: (i, j + x_off)),
                  pl.BlockSpec((SUBLANES, cw), lambda j, i: (jnp.maximum(i * nb - 1, 0), j + x_off)),
                  pl.BlockSpec((kw, cw), lambda j, i: (0, j))],
        out_specs=pl.BlockSpec((tb, cw), lambda j, i: (i, j)),
        out_shape=jax.ShapeDtypeStruct((rows, width), F32),
        compiler_params=_cparams(("parallel", "parallel")),
    )(x, x, w)


def _conv_bwd(x, w, dout, kw, name, x_off=0, cw=512, tb=512):
    rows, width = dout.shape
    tb = min(tb, rows)
    nb = tb // SUBLANES
    nrow = rows // tb

    def body(x_ref, h_ref, w_ref, d_ref, dn_ref, dx_ref, dw_ref):
        i = pl.program_id(1)
        xv, dv = x_ref[...], d_ref[...]
        halo = jnp.where(i > 0, h_ref[...], 0.0)
        nxt = jnp.where(i < nrow - 1, dn_ref[...], 0.0)

        @pl.when(i == 0)
        def _():
            dw_ref[...] = jnp.zeros_like(dw_ref)

        acc = w_ref[kw - 1:kw, :] * dv
        dw_ref[kw - 1:kw, :] += jnp.sum(dv * xv, axis=0, keepdims=True)
        for s in range(1, kw):
            acc = acc + w_ref[kw - 1 - s:kw - s, :] * _shift_up(dv, nxt, s, tb)
            dw_ref[kw - 1 - s:kw - s, :] += jnp.sum(dv * _shift_down(xv, halo, s, tb), axis=0, keepdims=True)
        dx_ref[...] = acc

    return pl.pallas_call(
        body, name=name, grid=(width // cw, nrow),
        in_specs=[pl.BlockSpec((tb, cw), lambda j, i: (i, j + x_off)),
                  pl.BlockSpec((SUBLANES, cw), lambda j, i: (jnp.maximum(i * nb - 1, 0), j + x_off)),
                  pl.BlockSpec((kw, cw), lambda j, i: (0, j)),
                  pl.BlockSpec((tb, cw), lambda j, i: (i, j)),
                  pl.BlockSpec((SUBLANES, cw), lambda j, i: (jnp.minimum((i + 1) * nb, rows // SUBLANES - 1), j))],
        out_specs=[pl.BlockSpec((tb, cw), lambda j, i: (i, j)),
                   pl.BlockSpec((SUBLANES, cw), lambda j, i: (0, j))],
        out_shape=[jax.ShapeDtypeStruct((rows, width), F32), jax.ShapeDtypeStruct((SUBLANES, width), F32)],
        compiler_params=_cparams(("parallel", "arbitrary")),
    )(x, x, w, dout, dout)


def _conv_taps(xv, halo, w_ref, kw, tb):
    acc = w_ref[kw - 1:kw, :] * xv
    for s in range(1, kw):
        acc = acc + w_ref[kw - 1 - s:kw - s, :] * _shift_down(xv, halo, s, tb)
    return acc


def _ffn_mix_specs(rows, f, kw, cw, tb):
    nj, nb, last8 = f // cw, tb // SUBLANES, rows // SUBLANES - 1
    blk = lambda off: pl.BlockSpec((tb, cw), lambda j, i: (i, j + off))
    prev = lambda off: pl.BlockSpec((SUBLANES, cw), lambda j, i: (jnp.maximum(i * nb - 1, 0), j + off))
    nxt = lambda off: pl.BlockSpec((SUBLANES, cw), lambda j, i: (jnp.minimum((i + 1) * nb, last8), j + off))
    wsp = lambda off: pl.BlockSpec((kw, cw), lambda j, i: (0, j + off))
    return nj, blk, prev, nxt, wsp


def _ffn_mix_fwd(up, w, name, cw=512, tb=512):
    rows, f, kw = up.shape[0], up.shape[1] // 2, w.shape[0]
    tb = min(tb, rows)
    nj, blk, prev, _, wsp = _ffn_mix_specs(rows, f, kw, cw, tb)

    def body(xa_ref, xv_ref, ha_ref, hv_ref, wa_ref, wv_ref, o_ref):
        first = pl.program_id(1) == 0
        ca = _conv_taps(xa_ref[...], jnp.where(first, 0.0, ha_ref[...]), wa_ref, kw, tb)
        cv = _conv_taps(xv_ref[...], jnp.where(first, 0.0, hv_ref[...]), wv_ref, kw, tb)
        o_ref[...] = _f_ffn_gate(ca, cv)[0].astype(BF16)

    return pl.pallas_call(
        body, name=name, grid=(nj, rows // tb),
        in_specs=[blk(0), blk(nj), prev(0), prev(nj), wsp(0), wsp(nj)],
        out_specs=blk(0), out_shape=jax.ShapeDtypeStruct((rows, f), BF16),
        compiler_params=_cparams(("parallel", "parallel")),
    )(up, up, up, up, w, w)


def _ffn_mix_bwd(up, w, dhid, name, cw=512, tb=512):
    rows, f, kw = up.shape[0], up.shape[1] // 2, w.shape[0]
    tb = min(tb, rows)
    te = tb + SUBLANES
    nrow = rows // tb
    nj, blk, prev, nxt, wsp = _ffn_mix_specs(rows, f, kw, cw, tb)

    def body(xa_ref, xv_ref, ha_ref, hv_ref, na_ref, nv_ref, wa_ref, wv_ref, d_ref, dn_ref,
             da_ref, dv_ref, dwa_ref, dwv_ref):
        i = pl.program_id(1)
        first, last = i == 0, i == nrow - 1

        @pl.when(first)
        def _():
            dwa_ref[...] = jnp.zeros_like(dwa_ref)
            dwv_ref[...] = jnp.zeros_like(dwv_ref)

        def conv_ext(x_ref, h_ref, n_ref, w_ref):
            xe = jnp.concatenate([x_ref[...], jnp.where(last, 0.0, n_ref[...])], axis=0)
            return _conv_taps(xe, jnp.where(first, 0.0, h_ref[...]), w_ref, kw, te)

        dh = jnp.concatenate([d_ref[...], jnp.where(last, 0.0, dn_ref[...])], axis=0)
        _, vjp = jax.vjp(_f_ffn_gate, conv_ext(xa_ref, ha_ref, na_ref, wa_ref), conv_ext(xv_ref, hv_ref, nv_ref, wv_ref))
        dca, dcv = vjp((dh,))

        def back(dc_ext, x_ref, h_ref, w_ref, dx_ref, dw_ref):
            dc = dc_ext[0:tb]
            xv = x_ref[...]
            halo = jnp.where(first, 0.0, h_ref[...])
            acc = w_ref[kw - 1:kw, :] * dc
            dw_ref[kw - 1:kw, :] += jnp.sum(dc * xv, axis=0, keepdims=True)
            for s in range(1, kw):
                acc = acc + w_ref[kw - 1 - s:kw - s, :] * pltpu.roll(dc_ext, te - s, 0)[0:tb]
                dw_ref[kw - 1 - s:kw - s, :] += jnp.sum(dc * _shift_down(xv, halo, s, tb), axis=0, keepdims=True)
            dx_ref[...] = acc.astype(dx_ref.dtype)

        back(dca, xa_ref, ha_ref, wa_ref, da_ref, dwa_ref)
        back(dcv, xv_ref, hv_ref, wv_ref, dv_ref, dwv_ref)

    wide = jax.ShapeDtypeStruct((rows, f), BF16)
    taps = jax.ShapeDtypeStruct((SUBLANES, f), F32)
    tap_spec = pl.BlockSpec((SUBLANES, cw), lambda j, i: (0, j))
    return pl.pallas_call(
        body, name=name, grid=(nj, nrow),
        in_specs=[blk(0), blk(nj), prev(0), prev(nj), nxt(0), nxt(nj), wsp(0), wsp(nj), blk(0), nxt(0)],
        out_specs=[blk(0), blk(0), tap_spec, tap_spec], out_shape=[wide, wide, taps, taps],
        compiler_params=_cparams(("parallel", "arbitrary")),
    )(up, up, up, up, up, up, w, w, dhid, dhid)


def _f_s5_disc(log_dt, a_re, a_im, bt_re, bt_im):
    dt = jnp.exp(log_dt)
    mag = jnp.exp(a_re * dt)
    abar_re, abar_im = mag * jnp.cos(a_im * dt), mag * jnp.sin(a_im * dt)
    den = a_re * a_re + a_im * a_im
    nr, ni = abar_re - 1.0, abar_im
    coef_re = (nr * a_re + ni * a_im) / den
    coef_im = (ni * a_re - nr * a_im) / den
    bbar_re = coef_re[None] * bt_re - coef_im[None] * bt_im
    bbar_im = coef_re[None] * bt_im + coef_im[None] * bt_re
    return abar_re, abar_im, bbar_re, bbar_im


def _s5_disc_fwd(log_dt, a_re, a_im, bt_re, bt_im, name):
    def body(*refs):
        res = _f_s5_disc(*[r[...] for r in refs[:5]])
        for o_ref, r in zip(refs[5:], res):
            o_ref[...] = r

    shp = [a_re, a_re, bt_re, bt_re]
    return pl.pallas_call(body, name=name, out_shape=[jax.ShapeDtypeStruct(s.shape, F32) for s in shp])(
        log_dt, a_re, a_im, bt_re, bt_im)


def _s5_disc_bwd(log_dt, a_re, a_im, bt_re, bt_im, cots, name):
    def body(*refs):
        _, vjp = jax.vjp(_f_s5_disc, *[r[...] for r in refs[:5]])
        grads = vjp(tuple(r[...] for r in refs[5:9]))
        for o_ref, g in zip(refs[9:], grads):
            o_ref[...] = g

    ins = [log_dt, a_re, a_im, bt_re, bt_im]
    return pl.pallas_call(body, name=name, out_shape=[jax.ShapeDtypeStruct(s.shape, F32) for s in ins])(*ins, *cots)


def _cmul(ar, ai, br, bi):
    return ar * br - ai * bi, ar * bi + ai * br


def _seg_scan(xr_ref, xi_ref, ar, ai, cr, ci, pr_ref, pi_ref, tb, reverse):
    sl = tb // SUBLANES
    pitch = _seg_pitch(sl)
    groups = xr_ref.shape[0]
    lane = lambda a, k: a[:, k * LANES:(k + 1) * LANES]
    outs_r, outs_i = [], []
    a_k = [(lane(ar, k), lane(ai, k)) for k in range(groups)]

    def step(i, carry):
        t = sl - 1 - i if reverse else i
        rows = pl.ds(t, SUBLANES, stride=pitch)
        nxt = []
        for k, (xr, xi, pr, pi) in enumerate(carry):
            akr, aki = a_k[k]
            mr, mi = _cmul(akr, aki, xr, xi)
            xr, xi = mr + xr_ref[k, rows, :], mi + xi_ref[k, rows, :]
            xr_ref[k, rows, :] = xr
            xi_ref[k, rows, :] = xi
            pr_ref[k, pl.ds(t, 1), :] = pr
            pi_ref[k, pl.ds(t, 1), :] = pi
            nr, ni = _cmul(akr, aki, pr, pi)
            nxt.append((xr, xi, nr, ni))
        return tuple(nxt)

    zero = jnp.zeros((SUBLANES, LANES), F32)
    lax.fori_loop(0, sl, step, tuple((zero, zero, akr, aki) for akr, aki in a_k))
    last = 0 if reverse else sl - 1
    for k in range(groups):
        qr, qi = pr_ref[k, last:last + 1, :], pi_ref[k, last:last + 1, :]
        tr, ti = pr_ref[k], pi_ref[k]
        ckr, cki = lane(cr, k), lane(ci, k)
        order = range(SUBLANES - 1, -1, -1) if reverse else range(SUBLANES)
        for j in order:
            rows = slice(j * pitch, j * pitch + sl)
            edge = j * pitch if reverse else j * pitch + sl - 1
            er, ei = xr_ref[k, edge:edge + 1, :], xi_ref[k, edge:edge + 1, :]
            mr, mi = _cmul(tr, ti, ckr, cki)
            xr_ref[k, rows, :] += mr
            xi_ref[k, rows, :] += mi
            mr, mi = _cmul(qr, qi, ckr, cki)
            ckr, cki = er + mr, ei + mi
        outs_r.append(ckr)
        outs_i.append(cki)
    return jnp.concatenate(outs_r, axis=1), jnp.concatenate(outs_i, axis=1)


def _seg_pitch(sl):
    return sl + SUBLANES


def _to_groups(ref, val):
    sl = val.shape[0] // SUBLANES
    pitch = _seg_pitch(sl)
    for k in range(ref.shape[0]):
        for j in range(SUBLANES):
            ref[k, j * pitch:j * pitch + sl, :] = val[j * sl:(j + 1) * sl, k * LANES:(k + 1) * LANES]


def _from_groups(ref):
    pitch = ref.shape[1] // SUBLANES
    sl = pitch - SUBLANES
    return jnp.concatenate(
        [jnp.concatenate([ref[k, j * pitch:j * pitch + sl, :] for j in range(SUBLANES)], axis=0)
         for k in range(ref.shape[0])], axis=1)


_INV_SQRT2 = 1.0 / math.sqrt(2.0)
_INV_SQRT2PI = 1.0 / math.sqrt(2.0 * math.pi)


def _gelu(y):
    return 0.5 * y * (1.0 + lax.erf(y * _INV_SQRT2))


def _gelu_grad(y):
    return 0.5 * (1.0 + lax.erf(y * _INV_SQRT2)) + y * jnp.exp(-0.5 * y * y) * _INV_SQRT2PI


def _s5_states(u, bd_re, bd_im, ar, ai, cr, ci, xr_ref, xi_ref, pr_ref, pi_ref, tb):
    ub = u.astype(BF16)
    _to_groups(xr_ref, _dot(ub, bd_re, _NN))
    _to_groups(xi_ref, _dot(ub, bd_im, _NN))
    return _seg_scan(xr_ref, xi_ref, ar, ai, cr, ci, pr_ref, pi_ref, tb, reverse=False)


def _s5_scratch(tb, nbuf):
    groups = S5_SLAB_STATE // LANES
    sl = tb // SUBLANES
    return ([pltpu.VMEM((1, S5_SLAB_STATE), F32)] * 2
            + [pltpu.VMEM((groups, SUBLANES * _seg_pitch(sl), LANES), F32)] * nbuf
            + [pltpu.VMEM((groups, sl, LANES), F32)] * 2)


def _s5_specs(tb, u_off):
    slab3 = lambda r, c: pl.BlockSpec((None, r, c), lambda s, t: (s, 0, 0))
    return dict(
        u=lambda tmap: pl.BlockSpec((tb, LANES), lambda s, t: (tmap(t), s + u_off)),
        bd=slab3(LANES, S5_SLAB_STATE), cd=slab3(S5_SLAB_STATE, LANES), a=slab3(1, S5_SLAB_STATE),
        d=pl.BlockSpec((1, LANES), lambda s, t: (0, s)))


def _s5_fwd(proj, bd_re, bd_im, cd_re, cd_im, a_re, a_im, d, name):
    rows = proj.shape[0]
    tb = min(S5_TB, rows)
    nt = rows // tb
    sp = _s5_specs(tb, 0)

    def body(u_ref, bdr, bdi, cdr, cdi, ar_ref, ai_ref, d_ref, y_ref, sr_ref, si_ref,
             cr_s, ci_s, xr_s, xi_s, pr_s, pi_s):
        t = pl.program_id(1)

        @pl.when(t == 0)
        def _():
            cr_s[...] = jnp.zeros_like(cr_s)
            ci_s[...] = jnp.zeros_like(ci_s)

        cr, ci = cr_s[...], ci_s[...]
        sr_ref[...] = cr
        si_ref[...] = ci
        u = u_ref[...]
        cr, ci = _s5_states(u, bdr[...], bdi[...], ar_ref[...], ai_ref[...], cr, ci, xr_s, xi_s, pr_s, pi_s, tb)
        cr_s[...] = cr
        ci_s[...] = ci
        y = (_dot(_from_groups(xr_s).astype(BF16), cdr[...], _NN)
             - _dot(_from_groups(xi_s).astype(BF16), cdi[...], _NN) + d_ref[...] * u)
        y_ref[...] = _gelu(y).astype(BF16)

    st_spec = pl.BlockSpec((None, None, 1, S5_SLAB_STATE), lambda s, t: (s, t, 0, 0))
    st_shape = jax.ShapeDtypeStruct((S5_SLABS, nt, 1, S5_SLAB_STATE), F32)
    return pl.pallas_call(
        body, name=name, grid=(S5_SLABS, nt),
        in_specs=[sp["u"](lambda t: t), sp["bd"], sp["bd"], sp["cd"], sp["cd"], sp["a"], sp["a"], sp["d"]],
        out_specs=[pl.BlockSpec((tb, LANES), lambda s, t: (t, s)), st_spec, st_spec],
        out_shape=[jax.ShapeDtypeStruct((rows, S5_WIDTH), BF16), st_shape, st_shape],
        scratch_shapes=_s5_scratch(tb, 2),
        compiler_params=_cparams(("parallel", "arbitrary")),
    )(proj, bd_re, bd_im, cd_re, cd_im, a_re, a_im, d)


def _s5_bwd(proj, dy, st_re, st_im, bd_re, bd_im, cd_re, cd_im, a_re, a_im, d, name):
    rows = proj.shape[0]
    tb = min(S5_TB, rows)
    nt = rows // tb
    sp = _s5_specs(tb, 0)
    rev = lambda t: nt - 1 - t

    def body(u_ref, dy_ref, sr_ref, si_ref, bdr, bdi, cdr, cdi, ar_ref, ai_ref, d_ref,
             du_ref, gbr, gbi, gcr, gci, gar, gai, gd_ref, lr_s, li_s, tr_s, ti_s, xr_s, xi_s, pr_s, pi_s):
        t = pl.program_id(1)

        @pl.when(t == 0)
        def _():
            lr_s[...] = jnp.zeros_like(lr_s)
            li_s[...] = jnp.zeros_like(li_s)
            for r in (gbr, gbi, gcr, gci, gar, gai, gd_ref):
                r[...] = jnp.zeros_like(r)

        u = u_ref[...]
        ar, ai = ar_ref[...], ai_ref[...]
        cr, ci = sr_ref[...], si_ref[...]
        _s5_states(u, bdr[...], bdi[...], ar, ai, cr, ci, xr_s, xi_s, pr_s, pi_s, tb)
        xr, xi = _from_groups(xr_s), _from_groups(xi_s)
        xrb, xib = xr.astype(BF16), xi.astype(BF16)
        ypre = _dot(xrb, cdr[...], _NN) - _dot(xib, cdi[...], _NN) + d_ref[...] * u
        dyp = dy_ref[...] * _gelu_grad(ypre)
        dypb = dyp.astype(BF16)
        gd_ref[...] += jnp.sum(dyp * u, axis=0, keepdims=True)
        gcr[...] += _dot(xrb, dypb, _TN)
        gci[...] -= _dot(xib, dypb, _TN)
        _to_groups(tr_s, _dot(dypb, cdr[...], _NT))
        _to_groups(ti_s, -_dot(dypb, cdi[...], _NT))
        nr, ni = _seg_scan(tr_s, ti_s, ar, -ai, lr_s[...], li_s[...], pr_s, pi_s, tb, reverse=True)
        lr_s[...] = nr
        li_s[...] = ni
        lr, li = _from_groups(tr_s), _from_groups(ti_s)
        row = lax.broadcasted_iota(jnp.int32, (tb, 1), 0)
        lrb, lib = lr.astype(BF16), li.astype(BF16)
        du_ref[...] = _dot(lrb, bdr[...], _NT) + _dot(lib, bdi[...], _NT) + d_ref[...] * dyp
        ub = u.astype(BF16)
        gbr[...] += _dot(ub, lrb, _TN)
        gbi[...] += _dot(ub, lib, _TN)
        xpr = jnp.where(row == 0, cr, pltpu.roll(xr, 1, 0))
        xpi = jnp.where(row == 0, ci, pltpu.roll(xi, 1, 0))
        gar[...] += jnp.sum(lr * xpr + li * xpi, axis=0, keepdims=True)
        gai[...] += jnp.sum(li * xpr - lr * xpi, axis=0, keepdims=True)

    st_spec = pl.BlockSpec((None, None, 1, S5_SLAB_STATE), lambda s, t: (s, rev(t), 0, 0))
    slab = lambda r, c: pl.BlockSpec((None, r, c), lambda s, t: (s, 0, 0))
    return pl.pallas_call(
        body, name=name, grid=(S5_SLABS, nt),
        in_specs=[sp["u"](rev), pl.BlockSpec((tb, LANES), lambda s, t: (rev(t), s)), st_spec, st_spec,
                  sp["bd"], sp["bd"], sp["cd"], sp["cd"], sp["a"], sp["a"], sp["d"]],
        out_specs=[pl.BlockSpec((tb, LANES), lambda s, t: (rev(t), s)),
                   slab(LANES, S5_SLAB_STATE), slab(LANES, S5_SLAB_STATE),
                   slab(S5_SLAB_STATE, LANES), slab(S5_SLAB_STATE, LANES),
                   slab(1, S5_SLAB_STATE), slab(1, S5_SLAB_STATE),
                   pl.BlockSpec((1, LANES), lambda s, t: (0, s))],
        out_shape=[jax.ShapeDtypeStruct((rows, S5_WIDTH), F32),
                   jax.ShapeDtypeStruct((S5_SLABS, LANES, S5_SLAB_STATE), F32),
                   jax.ShapeDtypeStruct((S5_SLABS, LANES, S5_SLAB_STATE), F32),
                   jax.ShapeDtypeStruct((S5_SLABS, S5_SLAB_STATE, LANES), F32),
                   jax.ShapeDtypeStruct((S5_SLABS, S5_SLAB_STATE, LANES), F32),
                   jax.ShapeDtypeStruct((S5_SLABS, 1, S5_SLAB_STATE), F32),
                   jax.ShapeDtypeStruct((S5_SLABS, 1, S5_SLAB_STATE), F32),
                   jax.ShapeDtypeStruct((1, S5_WIDTH), F32)],
        scratch_shapes=_s5_scratch(tb, 4),
        compiler_params=_cparams(("parallel", "arbitrary")),
    )(proj, dy, st_re, st_im, bd_re, bd_im, cd_re, cd_im, a_re, a_im, d)


@functools.partial(jax.custom_vjp, nondiff_argnums=(2,))
def _bdot(a, b, dims):
    return _dot(a.astype(BF16), b.astype(BF16), dims)


def _bdot_fwd(a, b, dims):
    return _bdot(a, b, dims), (a, b)


def _bdot_bwd(dims, res, ct):
    a, b = res
    if dims == _NN:
        return _bdot(ct, b, _NT), _bdot(a, ct, _TN)
    if dims == _NT:
        return _bdot(ct, b, _NN), _bdot(ct, a, _TN)
    return _bdot(b, ct, _NT), _bdot(a, ct, _NN)


_bdot.defvjp(_bdot_fwd, _bdot_bwd)


def _split_bf16(a):
    hi = a.astype(BF16)
    return hi, (a - hi.astype(F32)).astype(BF16)


@functools.partial(jax.custom_vjp, nondiff_argnums=(2,))
def _dot3(a, b, dims):
    ah, al = _split_bf16(a)
    bh, bl = _split_bf16(b)
    return _dot(ah, bh, dims) + (_dot(ah, bl, dims) + _dot(al, bh, dims))


def _dot3_fwd(a, b, dims):
    return _dot3(a, b, dims), (a, b)


def _dot3_bwd(dims, res, ct):
    a, b = res
    if dims == _NN:
        return _dot3(ct, b, _NT), _dot3(a, ct, _TN)
    if dims == _NT:
        return _dot3(ct, b, _NN), _dot3(ct, a, _TN)
    return _dot3(b, ct, _NT), _dot3(a, ct, _NN)


_dot3.defvjp(_dot3_fwd, _dot3_bwd)


def _tril_ones(c):
    r = lax.broadcasted_iota(jnp.int32, (c, c), 0)
    col = lax.broadcasted_iota(jnp.int32, (c, c), 1)
    return jnp.where(r >= col, 1.0, 0.0).astype(BF16)


@jax.custom_vjp
def _chunk_cumsum(x):
    xh, xl = _split_bf16(x)
    t = _tril_ones(x.shape[0])
    return _dot(t, xh, _NN) + _dot(t, xl, _NN)


def _chunk_cumsum_fwd(x):
    return _chunk_cumsum(x), None


def _chunk_cumsum_bwd(_, ct):
    ch, cl = _split_bf16(ct)
    t = _tril_ones(ct.shape[0])
    return (_dot(t, ch, _TN) + _dot(t, cl, _TN),)


_chunk_cumsum.defvjp(_chunk_cumsum_fwd, _chunk_cumsum_bwd)


def _unit_lower_inverses(lms, n):
    r = lax.broadcasted_iota(jnp.int32, (n, n), 0)
    c = lax.broadcasted_iota(jnp.int32, (n, n), 1)
    eye = jnp.where(r == c, 1.0, 0.0)
    ps = [eye - lm for lm in lms]
    powers = list(lms)
    steps = int(math.log2(n)) - 1
    for _ in range(steps):
        powers = [_dot3(x, x, _NN) for x in powers]
        ps = [p + _dot3(p, x, _NN) for p, x in zip(ps, powers)]
    return ps


def _dn_chunk(qs, ks, vs, gs, bs, ss):
    c = qs[0].shape[0]
    r = lax.broadcasted_iota(jnp.int32, (c, c), 0)
    col = lax.broadcasted_iota(jnp.int32, (c, c), 1)
    tril = r >= col
    strict = r > col
    gls = [jnp.broadcast_to(g, (c, LANES)) for g in gs]
    gcs = [_chunk_cumsum(gl) for gl in gls]
    gtots = [jnp.sum(gl, axis=0, keepdims=True) for gl in gls]
    gdiffs = [_chunk_cumsum(jnp.where(strict, jnp.broadcast_to(g, (c, c)), 0.0)) for g in gs]
    decays = [jnp.where(tril, jnp.exp(jnp.where(tril, gd, 0.0)), 0.0) for gd in gdiffs]
    kbs = [k * b for k, b in zip(ks, bs)]
    vbs = [v * b for v, b in zip(vs, bs)]
    lmats = [jnp.where(strict, _bdot(kb, k, _NT) * d, 0.0) for kb, k, d in zip(kbs, ks, decays)]
    attns = [jnp.where(tril, _bdot(q, k, _NT) * d, 0.0) for q, k, d in zip(qs, ks, decays)]
    tinvs = _unit_lower_inverses(lmats, c)
    us = [_dot3(t, vb, _NN) for t, vb in zip(tinvs, vbs)]
    ws = [_dot3(t, kb * jnp.exp(gc), _NN) for t, kb, gc in zip(tinvs, kbs, gcs)]
    ws_s = [_bdot(w, s, _NN) for w, s in zip(ws, ss)]
    qs_s = [_bdot(q * jnp.exp(gc), s, _NN) for q, gc, s in zip(qs, gcs, ss)]
    v_news = [u - x for u, x in zip(us, ws_s)]
    os_ = [x + _bdot(a, vn, _NN) for x, a, vn in zip(qs_s, attns, v_news)]
    s_news = [s * jnp.exp(gt) + _bdot(k * jnp.exp(gt - gc), vn, _TN)
              for s, gt, k, gc, vn in zip(ss, gtots, ks, gcs, v_news)]
    return tuple(os_), tuple(s_news)


def _dn_specs(tb, hb, tmap):
    groups = DN_HEADS // hb
    blk = lambda part: pl.BlockSpec((tb, hb * LANES), lambda hg, t: (tmap(t), hg + part * groups))
    colv = pl.BlockSpec((hb, tb, 1), lambda hg, t: (hg, tmap(t), 0))
    st = pl.BlockSpec((hb, tb // DN_CHUNK, DN_DK, DN_DK), lambda hg, t: (hg, tmap(t), 0, 0))
    return blk, colv, st


def _dn_fwd(qkv, gcol, bcol, name, hb=DN_HEADS):
    rows = qkv.shape[0]
    tb = min(DN_TB, rows)
    nt = rows // tb
    nch = tb // DN_CHUNK
    blk, colv, st = _dn_specs(tb, hb, lambda t: t)

    def body(q_ref, k_ref, v_ref, g_ref, b_ref, o_ref, st_ref, s_scr):
        @pl.when(pl.program_id(1) == 0)
        def _():
            s_scr[...] = jnp.zeros_like(s_scr)

        def chunk(ci, carry):
            rs = pl.ds(pl.multiple_of(ci * DN_CHUNK, DN_CHUNK), DN_CHUNK)
            cols = [slice(j * LANES, (j + 1) * LANES) for j in range(hb)]
            s_in = tuple(s_scr[j] for j in range(hb))
            for j in range(hb):
                st_ref[j, ci] = s_in[j]
            os_, s_new = _dn_chunk(tuple(q_ref[rs, cs] for cs in cols), tuple(k_ref[rs, cs] for cs in cols),
                                   tuple(v_ref[rs, cs] for cs in cols), tuple(g_ref[j, rs, :] for j in range(hb)),
                                   tuple(b_ref[j, rs, :] for j in range(hb)), s_in)
            for j in range(hb):
                o_ref[rs, cols[j]] = os_[j]
                s_scr[j] = s_new[j]
            return carry

        lax.fori_loop(0, nch, chunk, 0)

    return pl.pallas_call(
        body, name=name, grid=(DN_HEADS // hb, nt),
        in_specs=[blk(0), blk(1), blk(2), colv, colv],
        out_specs=[blk(0), st],
        out_shape=[jax.ShapeDtypeStruct((rows, DN_HEADS * DN_DK), F32),
                   jax.ShapeDtypeStruct((DN_HEADS, rows // DN_CHUNK, DN_DK, DN_DK), F32)],
        scratch_shapes=[pltpu.VMEM((hb, DN_DK, DN_DK), F32)],
        compiler_params=_cparams(("parallel", "arbitrary")),
    )(qkv, qkv, qkv, gcol, bcol)


def _dn_bwd(qkv, gcol, bcol, states, do, name, hb=DN_HEADS // 2):
    rows = qkv.shape[0]
    tb = min(DN_TB, rows)
    nt = rows // tb
    nch = tb // DN_CHUNK
    blk, colv, st = _dn_specs(tb, hb, lambda t: nt - 1 - t)

    def body(q_ref, k_ref, v_ref, g_ref, b_ref, st_ref, do_ref, dq_ref, dk_ref, dv_ref, dg_ref, db_ref, ds_scr):
        @pl.when(pl.program_id(1) == 0)
        def _():
            ds_scr[...] = jnp.zeros_like(ds_scr)

        def chunk(cj, carry):
            ci = nch - 1 - cj
            rs = pl.ds(pl.multiple_of(ci * DN_CHUNK, DN_CHUNK), DN_CHUNK)
            cols = [slice(j * LANES, (j + 1) * LANES) for j in range(hb)]
            heads = range(hb)
            args = (tuple(q_ref[rs, cs] for cs in cols), tuple(k_ref[rs, cs] for cs in cols),
                    tuple(v_ref[rs, cs] for cs in cols), tuple(g_ref[j, rs, :] for j in heads),
                    tuple(b_ref[j, rs, :] for j in heads), tuple(st_ref[j, ci] for j in heads))
            _, vjp = jax.vjp(_dn_chunk, *args)
            dq, dk, dv, dg, db, ds = vjp((tuple(do_ref[rs, cs] for cs in cols), tuple(ds_scr[j] for j in heads)))
            for j in heads:
                dq_ref[rs, cols[j]] = dq[j]
                dk_ref[rs, cols[j]] = dk[j]
                dv_ref[rs, cols[j]] = dv[j]
                dg_ref[j, rs, :] = dg[j]
                db_ref[j, rs, :] = db[j]
                ds_scr[j] = ds[j]
            return carry

        lax.fori_loop(0, nch, chunk, 0)

    wide = jax.ShapeDtypeStruct((rows, DN_HEADS * DN_DK), F32)
    narrow = jax.ShapeDtypeStruct((DN_HEADS, rows, 1), F32)
    return pl.pallas_call(
        body, name=name, grid=(DN_HEADS // hb, nt),
        in_specs=[blk(0), blk(1), blk(2), colv, colv, st, blk(0)],
        out_specs=[blk(0), blk(0), blk(0), colv, colv],
        out_shape=[wide, wide, wide, narrow, narrow],
        scratch_shapes=[pltpu.VMEM((hb, DN_DK, DN_DK), F32)],
        compiler_params=_cparams(("parallel", "arbitrary")),
    )(qkv, qkv, qkv, gcol, bcol, states, do)


def _final_loss(x, w, target, name, tb=256):
    rows, width = x.shape
    tb = min(tb, rows)

    def body(x_ref, w_ref, t_ref, dx_ref, dw_ref, loss_ref):
        i = pl.program_id(0)
        (y,), vjp = jax.vjp(_f_rms, x_ref[...], w_ref[...])
        err = y - t_ref[...]
        part = 0.5 * jnp.sum(jnp.mean(err * err, axis=-1, keepdims=True), axis=0, keepdims=True)
        dx, dw = vjp((err * (1.0 / width),))
        dx_ref[...] = dx

        @pl.when(i == 0)
        def _():
            dw_ref[...] = dw
            loss_ref[...] = jnp.broadcast_to(part, loss_ref.shape)

        @pl.when(i > 0)
        def _():
            dw_ref[...] += dw
            loss_ref[...] += jnp.broadcast_to(part, loss_ref.shape)

    row = pl.BlockSpec((tb, width), lambda i: (i, 0))
    par = pl.BlockSpec((1, width), lambda i: (0, 0))
    return pl.pallas_call(
        body, name=name, grid=(rows // tb,), in_specs=[row, par, row],
        out_specs=[row, par, pl.BlockSpec((SUBLANES, LANES), lambda i: (0, 0))],
        out_shape=[jax.ShapeDtypeStruct((rows, width), F32), jax.ShapeDtypeStruct((1, width), F32),
                   jax.ShapeDtypeStruct((SUBLANES, LANES), F32)],
        compiler_params=_cparams(("arbitrary",)),
    )(x, w, target)


_ANY = pl.BlockSpec(memory_space=pl.ANY)
N_CHIPS = 4


def _mesh_place():
    x, y, c = lax.axis_index("x"), lax.axis_index("y"), lax.axis_index("c")
    other_chips = [(1 - x, y), (x, 1 - y), (1 - x, 1 - y)]
    return x, y, c, other_chips


def _remote(src, dst, send_sem, recv_sem, dev):
    return pltpu.make_async_remote_copy(src_ref=src, dst_ref=dst, send_sem=send_sem, recv_sem=recv_sem,
                                        device_id=dev, device_id_type=pl.DeviceIdType.MESH)


def _all_gather(arrs, name):
    n = len(arrs)

    def body(*refs):
        ins, outs = refs[:n], refs[n:2 * n]
        send_sems, recv_sems, loc_sems = refs[2 * n:]
        x, y, c, chips = _mesh_place()
        me, sibling = (x, y, c), (x, y, 1 - c)
        slot = lambda px, py, pc: 4 * px + 2 * py + pc
        local = []
        for i in range(n):
            cp = pltpu.make_async_copy(ins[i], outs[i].at[slot(*me)], loc_sems.at[i])
            cp.start()
            local.append(cp)
        sends = []
        for i in range(n):
            cp = _remote(ins[i], outs[i].at[slot(*me)], send_sems.at[i, 0], recv_sems.at[i, 0], sibling)
            cp.start()
            sends.append(cp)
            for j, chip in enumerate(chips):
                cp = _remote(ins[i], outs[i].at[slot(*me)], send_sems.at[i, 1 + j], recv_sems.at[i, 1 + j], (*chip, c))
                cp.start()
                sends.append(cp)
        for j, chip in enumerate(chips):
            for i in range(n):
                got = outs[i].at[slot(*chip, c)]
                _remote(got, got, send_sems.at[i, 1 + j], recv_sems.at[i, 1 + j], (*chip, c)).wait_recv()
                cp = _remote(got, got, send_sems.at[i, 4 + j], recv_sems.at[i, 4 + j], sibling)
                cp.start()
                sends.append(cp)
        for i in range(n):
            got = outs[i].at[slot(*sibling)]
            _remote(got, got, send_sems.at[i, 0], recv_sems.at[i, 0], sibling).wait_recv()
            for j, chip in enumerate(chips):
                got = outs[i].at[slot(*chip, 1 - c)]
                _remote(got, got, send_sems.at[i, 4 + j], recv_sems.at[i, 4 + j], sibling).wait_recv()
        for cp in sends:
            cp.wait_send()
        for cp in local:
            cp.wait()

    return pl.pallas_call(
        body, name=name, in_specs=[_ANY] * n, out_specs=[_ANY] * n,
        out_shape=[jax.ShapeDtypeStruct((N_DEV,) + a.shape, a.dtype) for a in arrs],
        scratch_shapes=[pltpu.SemaphoreType.DMA((n, N_DEV - 1)), pltpu.SemaphoreType.DMA((n, N_DEV - 1)),
                        pltpu.SemaphoreType.DMA((n,))],
    )(*arrs)


def _sibling_swap(arrs, name):
    n = len(arrs)

    def body(*refs):
        ins, outs = refs[:n], refs[n:2 * n]
        send_sems, recv_sems = refs[2 * n:]
        x, y, c, _ = _mesh_place()
        sibling = (x, y, 1 - c)
        cps = [_remote(ins[i].at[1 - c], outs[i], send_sems.at[i], recv_sems.at[i], sibling) for i in range(n)]
        for cp in cps:
            cp.start()
        for cp in cps:
            cp.wait()

    return pl.pallas_call(
        body, name=name, in_specs=[_ANY] * n, out_specs=[_ANY] * n,
        out_shape=[jax.ShapeDtypeStruct(a.shape[1:], a.dtype) for a in arrs],
        scratch_shapes=[pltpu.SemaphoreType.DMA((n,)), pltpu.SemaphoreType.DMA((n,))],
    )(*arrs)


def _chip_scatter(arrs, name):
    n = len(arrs)

    def body(*refs):
        ins, outs = refs[:n], refs[n:2 * n]
        send_sems, recv_sems, loc_sems = refs[2 * n:]
        x, y, c, chips = _mesh_place()
        mine = 2 * x + y
        local = []
        for i in range(n):
            cp = pltpu.make_async_copy(ins[i].at[mine], outs[i].at[mine], loc_sems.at[i])
            cp.start()
            local.append(cp)
        sends = []
        for i in range(n):
            for j, (px, py) in enumerate(chips):
                cp = _remote(ins[i].at[2 * px + py], outs[i].at[mine], send_sems.at[i, j], recv_sems.at[i, j],
                             (px, py, c))
                cp.start()
                sends.append(cp)
        for i in range(n):
            for j, (px, py) in enumerate(chips):
                got = outs[i].at[2 * px + py]
                _remote(got, got, send_sems.at[i, j], recv_sems.at[i, j], (px, py, c)).wait_recv()
        for cp in sends:
            cp.wait_send()
        for cp in local:
            cp.wait()

    return pl.pallas_call(
        body, name=name, in_specs=[_ANY] * n, out_specs=[_ANY] * n,
        out_shape=[jax.ShapeDtypeStruct(a.shape, a.dtype) for a in arrs],
        scratch_shapes=[pltpu.SemaphoreType.DMA((n, N_CHIPS - 1)), pltpu.SemaphoreType.DMA((n, N_CHIPS - 1)),
                        pltpu.SemaphoreType.DMA((n,))],
    )(*arrs)


def _pair_sum(own, got, name):
    _, _, rows, cols = own.shape
    tb = _row_tile(rows, cols, budget=512 * 1024)

    def body(c_ref, a_ref, b_ref, o_ref):
        o_ref[...] = (a_ref[...].astype(F32) + b_ref[...].astype(F32)).astype(o_ref.dtype)

    core = lax.axis_index("c").astype(jnp.int32).reshape(1)
    grid_spec = pltpu.PrefetchScalarGridSpec(
        num_scalar_prefetch=1, grid=(N_CHIPS, rows // tb),
        in_specs=[pl.BlockSpec((None, None, tb, cols), lambda p, i, c_ref: (c_ref[0], p, i, 0)),
                  pl.BlockSpec((None, tb, cols), lambda p, i, c_ref: (p, i, 0))],
        out_specs=pl.BlockSpec((None, tb, cols), lambda p, i, c_ref: (p, i, 0)))
    return pl.pallas_call(
        body, name=name, grid_spec=grid_spec, out_shape=jax.ShapeDtypeStruct(got.shape, got.dtype),
        compiler_params=_cparams(("parallel", "parallel")),
    )(core, own, got)


def _adamw_math(w, g, m, v):
    m = ADAM_B1 * m + (1.0 - ADAM_B1) * g
    v = ADAM_B2 * v + (1.0 - ADAM_B2) * (g * g)
    m_hat = m / (1.0 - ADAM_B1 ** ADAM_STEP)
    v_hat = v / (1.0 - ADAM_B2 ** ADAM_STEP)
    delta = -ADAM_LR * (m_hat / (jnp.sqrt(v_hat) + ADAM_EPS) + ADAM_WD * w)
    return delta, m, v


def _row_tile(rows, cols, budget=128 * 1024):
    if rows * cols <= budget or rows % SUBLANES:
        return rows
    best = SUBLANES
    for t in range(SUBLANES, rows + 1, SUBLANES):
        if rows % t == 0 and t * cols <= budget:
            best = t
    return best


def _reduce_adamw(parts, w, m, v, name):
    rows, cols = w.shape
    nparts = parts.shape[0]
    tb = _row_tile(rows, cols)

    def body(p_ref, w_ref, m_ref, v_ref, g_ref, d_ref, nm_ref, nv_ref):
        g = p_ref[0].astype(F32)
        for s in range(1, nparts):
            g = g + p_ref[s].astype(F32)
        delta, nm, nv = _adamw_math(w_ref[...], g, m_ref[...], v_ref[...])
        g_ref[...] = g
        d_ref[...] = delta
        nm_ref[...] = nm
        nv_ref[...] = nv

    spec = pl.BlockSpec((tb, cols), lambda i: (i, 0))
    shp = jax.ShapeDtypeStruct((rows, cols), F32)
    return pl.pallas_call(
        body, name=name, grid=(rows // tb,),
        in_specs=[pl.BlockSpec((nparts, tb, cols), lambda i: (0, i, 0)), spec, spec, spec],
        out_specs=[spec] * 4, out_shape=[shp] * 4,
        compiler_params=_cparams(("parallel",)),
    )(parts, w, m, v)


def _sum_parts(parts, name):
    nparts, rows, cols = parts.shape
    tb = _row_tile(rows, cols)

    def body(p_ref, g_ref):
        g = p_ref[0]
        for s in range(1, nparts):
            g = g + p_ref[s]
        g_ref[...] = g

    return pl.pallas_call(
        body, name=name, grid=(rows // tb,),
        in_specs=[pl.BlockSpec((nparts, tb, cols), lambda i: (0, i, 0))],
        out_specs=pl.BlockSpec((tb, cols), lambda i: (i, 0)),
        out_shape=jax.ShapeDtypeStruct((rows, cols), F32),
        compiler_params=_cparams(("parallel",)),
    )(parts)


def _adamw(g, w, m, v, name):
    rows, cols = w.shape
    tb = _row_tile(rows, cols)

    def body(g_ref, w_ref, m_ref, v_ref, d_ref, nm_ref, nv_ref):
        delta, nm, nv = _adamw_math(w_ref[...], g_ref[...], m_ref[...], v_ref[...])
        d_ref[...] = delta
        nm_ref[...] = nm
        nv_ref[...] = nv

    spec = pl.BlockSpec((tb, cols), lambda i: (i, 0))
    shp = jax.ShapeDtypeStruct((rows, cols), F32)
    return pl.pallas_call(
        body, name=name, grid=(rows // tb,), in_specs=[spec] * 4, out_specs=[spec] * 3, out_shape=[shp] * 3,
        compiler_params=_cparams(("parallel",)),
    )(g, w, m, v)


def _pack(arrs):
    flat = jnp.concatenate([a.reshape(-1) for a in arrs])
    pad = (-flat.shape[0]) % (PACK_ROWS * LANES)
    return jnp.pad(flat, (0, pad)).reshape(-1, LANES)


def _unpack(packed, shapes):
    flat = packed.reshape(-1)
    out, off = [], 0
    for s in shapes:
        n = math.prod(s)
        out.append(flat[off:off + n].reshape(s))
        off += n
    return out


def _block_diag_b(bt):
    bb = bt.transpose(1, 0, 2).reshape(S5_SLABS, 8, S5_GROUP, S5_STATE)
    eye = jnp.eye(8, dtype=bt.dtype)
    return (bb[:, :, :, None, :] * eye[None, :, None, :, None]).reshape(S5_SLABS, LANES, S5_SLAB_STATE)


def _block_diag_b_grad(g):
    g5 = g.reshape(S5_SLABS, 8, S5_GROUP, 8, S5_STATE)
    diag = jnp.stack([g5[:, a, :, a, :] for a in range(8)], axis=1)
    return diag.reshape(S5_GROUPS, S5_GROUP, S5_STATE).transpose(1, 0, 2)


def _block_diag_c(cw):
    cc = cw.reshape(S5_SLABS, 8, S5_GROUP, S5_STATE).transpose(0, 1, 3, 2)
    eye = jnp.eye(8, dtype=cw.dtype)
    return (cc[:, :, :, None, :] * eye[None, :, None, :, None]).reshape(S5_SLABS, S5_SLAB_STATE, LANES)


def _block_diag_c_grad(g):
    g5 = g.reshape(S5_SLABS, 8, S5_STATE, 8, S5_GROUP)
    diag = jnp.stack([g5[:, a, :, a, :] for a in range(8)], axis=1)
    return diag.transpose(0, 1, 3, 2).reshape(S5_GROUPS, S5_GROUP, S5_STATE)


def _cols_full(gathered):
    _, k, n = gathered.shape
    return gathered.transpose(1, 0, 2).reshape(k, N_DEV * n)


def _cols_split(full):
    k, n8 = full.shape
    return full.reshape(k, N_DEV, n8 // N_DEV).transpose(1, 0, 2)


SMALL_NAMES = ("mix_norm_w", "s5_log_dt", "s5_a_re", "s5_a_im", "s5_b_re", "s5_b_im", "s5_c_re", "s5_c_im", "s5_d",
               "dn_a_log", "dn_dt_bias", "dn_norm_w", "ffn_norm_w", "final_norm_w")
CONV_NAMES = ("dn_conv_w", "ffn_conv_w")
BIG_NAMES = ("w_in", "s5_glu_w", "dn_proj_w", "w_out", "ffn_up", "ffn_down")
ROW_SHARDED = ("w_out", "ffn_down")
WEIGHT_ORDER = ("mix_norm_w", "w_in", "s5_log_dt", "s5_a_re", "s5_a_im", "s5_b_re", "s5_b_im", "s5_c_re", "s5_c_im",
                "s5_d", "s5_glu_w", "dn_conv_w", "dn_a_log", "dn_dt_bias", "dn_norm_w", "dn_proj_w", "w_out",
                "ffn_norm_w", "ffn_up", "ffn_conv_w", "ffn_down", "final_norm_w")


def _layer_forward(l, x, wts, sm):
    rows = x.shape[0]
    sv = {"x0": x}
    nm = f"l{l}_"
    mixw = sm["mix_norm_w"][l][None]
    (h,) = _tile_fwd(nm + "mix_norm", _f_rms, [(x, None, 0)], [(mixw, None, 0)], [(D_MODEL, None, BF16)], rows, 256)
    proj = _mm(h, wts["w_in_main"], "nn", F32, nm + "proj")
    ba = _mm(h, wts["w_in_ba"], "nn", F32, nm + "proj_ba")
    sv.update(h=h, proj=proj, ba=ba)
    disc = _s5_disc_fwd(sm["s5_log_dt"][l][:, None], sm["s5_a_re"][l], sm["s5_a_im"][l],
                        sm["s5_b_re"][l].transpose(2, 0, 1), sm["s5_b_im"][l].transpose(2, 0, 1), nm + "s5_disc")
    abar_re, abar_im, bbar_re, bbar_im = disc
    s5p = dict(
        bd_re=_block_diag_b(bbar_re).astype(BF16), bd_im=_block_diag_b(bbar_im).astype(BF16),
        cd_re=_block_diag_c(sm["s5_c_re"][l]).astype(BF16), cd_im=_block_diag_c(sm["s5_c_im"][l]).astype(BF16),
        a_re=abar_re.reshape(S5_SLABS, 1, S5_SLAB_STATE), a_im=abar_im.reshape(S5_SLABS, 1, S5_SLAB_STATE),
        d=sm["s5_d"][l][None])
    y_s5, st_re, st_im = _s5_fwd(proj, s5p["bd_re"], s5p["bd_im"], s5p["cd_re"], s5p["cd_im"],
                                 s5p["a_re"], s5p["a_im"], s5p["d"], nm + "s5_scan")
    glu = _mm(y_s5, wts["s5_glu_w"], "nn", F32, nm + "glu")
    sv.update(s5p=s5p, y_s5=y_s5, st_re=st_re, st_im=st_im, glu=glu)
    conv = _conv_fwd(proj, wts["dn_conv_w"], DN_CONV, nm + "dn_conv", x_off=2, width=DN_QKV, cw=512)
    q, = _tile_fwd(nm + "dn_q", _f_dn_q, [(conv, LANES, 0)], [], [(1024, LANES, F32)], rows, 512, ncol=8)
    k, = _tile_fwd(nm + "dn_k", _f_dn_k, [(conv, LANES, 8)], [], [(1024, LANES, F32)], rows, 512, ncol=8)
    v, = _tile_fwd(nm + "dn_v", _f_dn_v, [(conv, LANES, 16)], [], [(1024, LANES, F32)], rows, 512, ncol=8)
    qkv = jnp.concatenate([q, k, v], axis=1)
    pad8 = lambda a: jnp.pad(a[None], ((0, 0), (DN_HEADS, LANES - 2 * DN_HEADS)))
    alog, dtb = pad8(sm["dn_a_log"][l]), pad8(sm["dn_dt_bias"][l])
    bg, = _tile_fwd(nm + "dn_gates", _f_dn_gates, [(ba, None, 0)], [(alog, None, 0), (dtb, None, 0)],
                    [(LANES, None, F32)], rows, 512)
    bcol = bg[:, 0:DN_HEADS].T[:, :, None]
    gcol = bg[:, DN_HEADS:2 * DN_HEADS].T[:, :, None]
    o, states = _dn_fwd(qkv, gcol, bcol, nm + "dn_chunk")
    dnw = sm["dn_norm_w"][l][None]
    y_dn, = _tile_fwd(nm + "dn_post", _f_dn_post, [(o, LANES, 0), (proj, LANES, MAIN_Z // LANES)], [(dnw, None, 0)],
                      [(1024, LANES, BF16)], rows, 512, ncol=8)
    br_dn = _mm(y_dn, wts["dn_proj_w"], "nn", F32, nm + "dn_proj")
    sv.update(conv=conv, qkv=qkv, alog=alog, dtb=dtb, bcol=bcol, gcol=gcol, o=o, states=states, dnw=dnw,
              y_dn=y_dn, br_dn=br_dn)
    cw = 512
    merged, = _tile_fwd(nm + "merge", _f_merge,
                        [(glu, cw, 0), (glu, cw, D_MODEL // cw), (br_dn, cw, 0),
                         (proj, cw, MAIN_GS // cw), (proj, cw, MAIN_GD // cw)], [],
                        [(D_MODEL, cw, BF16)], rows, 512, ncol=D_MODEL // cw)
    x1 = _mm(merged, wts["w_out"], "nn", F32, nm + "w_out", res=x)
    sv.update(merged=merged, x1=x1)
    ffw = sm["ffn_norm_w"][l][None]
    (h2,) = _tile_fwd(nm + "ffn_norm", _f_rms, [(x1, None, 0)], [(ffw, None, 0)], [(D_MODEL, None, BF16)], rows, 256)
    up = _mm(h2, wts["ffn_up"], "nn", F32, nm + "ffn_up")
    hid = _ffn_mix_fwd(up, wts["ffn_conv_w"], nm + "ffn_mix")
    x2 = _mm(hid, wts["ffn_down"], "nn", F32, nm + "ffn_down", res=x1, tk=1408)
    sv.update(h2=h2, up=up, hid=hid)
    return x2, sv


def _layer_backward(l, dx2, wts, sm, sv):
    rows = dx2.shape[0]
    nm = f"l{l}_b_"
    cw = 512
    gr = {}
    dxb = dx2.astype(BF16)
    gr["ffn_down"] = _mm(sv["hid"], dxb, "tn", BF16, nm + "ffn_down_w", tm=1408)
    dhid = _mm(dxb, wts["ffn_down"], "nt", F32, nm + "ffn_down_x", tn=1408)
    dup_a, dup_v, dw_a, dw_v = _ffn_mix_bwd(sv["up"], wts["ffn_conv_w"], dhid, nm + "ffn_mix")
    gr["ffn_conv_w"] = jnp.concatenate([dw_a[:FFN_CONV], dw_v[:FFN_CONV]], axis=1)
    dupb = jnp.concatenate([dup_a, dup_v], axis=1)
    gr["ffn_up"] = _mm(sv["h2"], dupb, "tn", BF16, nm + "ffn_up_w", tk=1024)
    dh2 = _mm(dupb, wts["ffn_up"], "nt", F32, nm + "ffn_up_x", tk=1024)
    ffw = sm["ffn_norm_w"][l][None]
    (dx1n,), (dffw,) = _tile_bwd(nm + "ffn_norm", _f_rms, [(sv["x1"], None, 0)], [(ffw, None, 0)],
                                 [(dh2, None, 0)], [F32], rows, 256)
    gr["ffn_norm_w"] = dffw[0]
    dx1 = dx2 + dx1n
    dx1b = dx1.astype(BF16)
    gr["w_out"] = _mm(sv["merged"], dx1b, "tn", BF16, nm + "w_out_w", tk=1024)
    dmerged = _mm(dx1b, wts["w_out"], "nt", F32, nm + "w_out_x")
    (dga, dgb, dbr, dgs, dgd), _ = _tile_bwd(
        nm + "merge", _f_merge,
        [(sv["glu"], cw, 0), (sv["glu"], cw, D_MODEL // cw), (sv["br_dn"], cw, 0),
         (sv["proj"], cw, MAIN_GS // cw), (sv["proj"], cw, MAIN_GD // cw)], [], [(dmerged, cw, 0)],
        [BF16, BF16, BF16, BF16, BF16], rows, 512, ncol=D_MODEL // cw)
    dglu = jnp.concatenate([dga, dgb], axis=1)
    gr["s5_glu_w"] = _mm(sv["y_s5"], dglu, "tn", BF16, nm + "glu_w", tk=1024)
    dy_s5 = _mm(dglu, wts["s5_glu_w"], "nt", F32, nm + "glu_x")
    gr["dn_proj_w"] = _mm(sv["y_dn"], dbr, "tn", BF16, nm + "dn_proj_w", tk=1024)
    dy_dn = _mm(dbr, wts["dn_proj_w"], "nt", F32, nm + "dn_proj_x")
    s5p = sv["s5p"]
    du, gbr, gbi, gcr, gci, gar, gai, gd = _s5_bwd(
        sv["proj"], dy_s5, sv["st_re"], sv["st_im"], s5p["bd_re"], s5p["bd_im"], s5p["cd_re"], s5p["cd_im"],
        s5p["a_re"], s5p["a_im"], s5p["d"], nm + "s5_scan")
    gr["s5_d"] = gd[0]
    gr["s5_c_re"] = _block_diag_c_grad(gcr)
    gr["s5_c_im"] = _block_diag_c_grad(gci)
    bt_re, bt_im = sm["s5_b_re"][l].transpose(2, 0, 1), sm["s5_b_im"][l].transpose(2, 0, 1)
    dldt, dare, daim, dbtr, dbti = _s5_disc_bwd(
        sm["s5_log_dt"][l][:, None], sm["s5_a_re"][l], sm["s5_a_im"][l], bt_re, bt_im,
        [gar.reshape(S5_GROUPS, S5_STATE), gai.reshape(S5_GROUPS, S5_STATE),
         _block_diag_b_grad(gbr), _block_diag_b_grad(gbi)], nm + "s5_disc")
    gr.update(s5_log_dt=dldt[:, 0], s5_a_re=dare, s5_a_im=daim,
              s5_b_re=dbtr.transpose(1, 2, 0), s5_b_im=dbti.transpose(1, 2, 0))
    (do, dz), (ddnw,) = _tile_bwd(nm + "dn_post", _f_dn_post,
                                  [(sv["o"], LANES, 0), (sv["proj"], LANES, MAIN_Z // LANES)],
                                  [(sv["dnw"], None, 0)], [(dy_dn, LANES, 0)], [F32, BF16], rows, 512, ncol=8)
    gr["dn_norm_w"] = ddnw[0]
    dq, dk, dv, dgc, dbc = _dn_bwd(sv["qkv"], sv["gcol"], sv["bcol"], sv["states"], do, nm + "dn_chunk")
    dbg = jnp.pad(jnp.concatenate([dbc[:, :, 0].T, dgc[:, :, 0].T], axis=1), ((0, 0), (0, LANES - 2 * DN_HEADS)))
    (dba,), (dalog, ddtb) = _tile_bwd(nm + "dn_gates", _f_dn_gates, [(sv["ba"], None, 0)],
                                      [(sv["alog"], None, 0), (sv["dtb"], None, 0)], [(dbg, None, 0)],
                                      [BF16], rows, 512)
    gr["dn_a_log"] = dalog[0, DN_HEADS:2 * DN_HEADS]
    gr["dn_dt_bias"] = ddtb[0, DN_HEADS:2 * DN_HEADS]
    (dcq,), _ = _tile_bwd(nm + "dn_q", _f_dn_q, [(sv["conv"], LANES, 0)], [], [(dq, LANES, 0)], [F32], rows, 512, 8)
    (dck,), _ = _tile_bwd(nm + "dn_k", _f_dn_k, [(sv["conv"], LANES, 8)], [], [(dk, LANES, 0)], [F32], rows, 512, 8)
    (dcv,), _ = _tile_bwd(nm + "dn_v", _f_dn_v, [(sv["conv"], LANES, 16)], [], [(dv, LANES, 0)], [F32], rows, 512, 8)
    dconv_out = jnp.concatenate([dcq, dck, dcv], axis=1)
    dqkv, ddnconv = _conv_bwd(sv["proj"], wts["dn_conv_w"], dconv_out, DN_CONV, nm + "dn_conv", x_off=2, cw=512)
    gr["dn_conv_w"] = ddnconv[:DN_CONV]
    dproj = jnp.concatenate([du.astype(BF16), dqkv.astype(BF16), dz, dgs, dgd], axis=1)
    gmain = _mm(sv["h"], dproj, "tn", BF16, nm + "proj_w", tk=1024)
    gba = _mm(sv["h"], dba, "tn", BF16, nm + "proj_ba_w", tk=1024)
    gr["w_in"] = jnp.concatenate([gmain[:, :OFF_BA], gba[:, :2 * DN_HEADS], gmain[:, OFF_BA:]], axis=1)
    dh = _mm(dproj, wts["w_in_main"], "nt", F32, nm + "proj_x", tk=1024)
    dh = _mm(dba, wts["w_in_ba"], "nt", F32, nm + "proj_ba_x", res=dh)
    mixw = sm["mix_norm_w"][l][None]
    (dx0n,), (dmixw,) = _tile_bwd(nm + "mix_norm", _f_rms, [(sv["x0"], None, 0)], [(mixw, None, 0)],
                                  [(dh, None, 0)], [F32], rows, 256)
    gr["mix_norm_w"] = dmixw[0]
    return dx1 + dx0n, gr


def kernel(x, mix_norm_w, w_in, s5_log_dt, s5_a_re, s5_a_im, s5_b_re, s5_b_im, s5_c_re, s5_c_im, s5_d, s5_glu_w, dn_conv_w, dn_a_log, dn_dt_bias, dn_norm_w, dn_proj_w, w_out, ffn_norm_w, ffn_up, ffn_conv_w, ffn_down, final_norm_w, loss_target, m_mix_norm_w, m_w_in, m_s5_log_dt, m_s5_a_re, m_s5_a_im, m_s5_b_re, m_s5_b_im, m_s5_c_re, m_s5_c_im, m_s5_d, m_s5_glu_w, m_dn_conv_w, m_dn_a_log, m_dn_dt_bias, m_dn_norm_w, m_dn_proj_w, m_w_out, m_ffn_norm_w, m_ffn_up, m_ffn_conv_w, m_ffn_down, m_final_norm_w, v_mix_norm_w, v_w_in, v_s5_log_dt, v_s5_a_re, v_s5_a_im, v_s5_b_re, v_s5_b_im, v_s5_c_re, v_s5_c_im, v_s5_d, v_s5_glu_w, v_dn_conv_w, v_dn_a_log, v_dn_dt_bias, v_dn_norm_w, v_dn_proj_w, v_w_out, v_ffn_norm_w, v_ffn_up, v_ffn_conv_w, v_ffn_down, v_final_norm_w):
    w = dict(mix_norm_w=mix_norm_w, w_in=w_in, s5_log_dt=s5_log_dt, s5_a_re=s5_a_re, s5_a_im=s5_a_im, s5_b_re=s5_b_re, s5_b_im=s5_b_im, s5_c_re=s5_c_re, s5_c_im=s5_c_im, s5_d=s5_d, s5_glu_w=s5_glu_w, dn_conv_w=dn_conv_w, dn_a_log=dn_a_log, dn_dt_bias=dn_dt_bias, dn_norm_w=dn_norm_w, dn_proj_w=dn_proj_w, w_out=w_out, ffn_norm_w=ffn_norm_w, ffn_up=ffn_up, ffn_conv_w=ffn_conv_w, ffn_down=ffn_down, final_norm_w=final_norm_w)
    mo = dict(mix_norm_w=m_mix_norm_w, w_in=m_w_in, s5_log_dt=m_s5_log_dt, s5_a_re=m_s5_a_re, s5_a_im=m_s5_a_im, s5_b_re=m_s5_b_re, s5_b_im=m_s5_b_im, s5_c_re=m_s5_c_re, s5_c_im=m_s5_c_im, s5_d=m_s5_d, s5_glu_w=m_s5_glu_w, dn_conv_w=m_dn_conv_w, dn_a_log=m_dn_a_log, dn_dt_bias=m_dn_dt_bias, dn_norm_w=m_dn_norm_w, dn_proj_w=m_dn_proj_w, w_out=m_w_out, ffn_norm_w=m_ffn_norm_w, ffn_up=m_ffn_up, ffn_conv_w=m_ffn_conv_w, ffn_down=m_ffn_down, final_norm_w=m_final_norm_w)
    vo = dict(mix_norm_w=v_mix_norm_w, w_in=v_w_in, s5_log_dt=v_s5_log_dt, s5_a_re=v_s5_a_re, s5_a_im=v_s5_a_im, s5_b_re=v_s5_b_re, s5_b_im=v_s5_b_im, s5_c_re=v_s5_c_re, s5_c_im=v_s5_c_im, s5_d=v_s5_d, s5_glu_w=v_s5_glu_w, dn_conv_w=v_dn_conv_w, dn_a_log=v_dn_a_log, dn_dt_bias=v_dn_dt_bias, dn_norm_w=v_dn_norm_w, dn_proj_w=v_dn_proj_w, w_out=v_w_out, ffn_norm_w=v_ffn_norm_w, ffn_up=v_ffn_up, ffn_conv_w=v_ffn_conv_w, ffn_down=v_ffn_down, final_norm_w=v_final_norm_w)
    depth = w_in.shape[0]
    me = 4 * lax.axis_index("x") + 2 * lax.axis_index("y") + lax.axis_index("c")
    xs = x[0]
    target = loss_target[0]

    gather_names = BIG_NAMES + CONV_NAMES
    gathered = _all_gather([w[n].astype(BF16) if n in BIG_NAMES else w[n] for n in gather_names], "gather_weights")
    full = {}
    for n, g in zip(gather_names, gathered):
        if n in ROW_SHARDED:
            full[n] = g.transpose(1, 0, 2, 3).reshape(depth, -1, g.shape[-1])
        else:
            full[n] = g.transpose(1, 2, 0, 3).reshape(depth, g.shape[2], -1)
    layer_w = []
    for l in range(depth):
        wi = full["w_in"][l]
        lw = {n: full[n][l] for n in gather_names if n != "w_in"}
        lw["w_in_main"] = jnp.concatenate([wi[:, :OFF_BA], wi[:, OFF_GS:]], axis=1)
        lw["w_in_ba"] = jnp.pad(wi[:, OFF_BA:OFF_GS], ((0, 0), (0, LANES - 2 * DN_HEADS)))
        layer_w.append(lw)

    saved = []
    h = xs
    for l in range(depth):
        h, sv = _layer_forward(l, h, layer_w[l], w)
        saved.append(sv)
    dx, dfinal, loss_tile = _final_loss(h, final_norm_w[None], target, "final_loss")

    grads = [None] * depth
    for l in reversed(range(depth)):
        dx, grads[l] = _layer_backward(l, dx, layer_w[l], w, saved[l])

    def stacked(n):
        return jnp.stack([grads[l][n] for l in range(depth)])

    big_send = []
    for n in BIG_NAMES:
        g = stacked(n)
        if n in ROW_SHARDED:
            g = g.reshape(depth, N_CHIPS, 2, g.shape[1] // N_DEV, g.shape[2]).transpose(2, 1, 0, 3, 4)
        else:
            g = g.reshape(depth, g.shape[1], N_CHIPS, 2, g.shape[2] // N_DEV).transpose(3, 2, 0, 1, 4)
        big_send.append(g.reshape(2, N_CHIPS, -1, g.shape[-1]))
    from_sibling = _sibling_swap(big_send, "swap_grads")
    chip_partials = [_pair_sum(own, got, "pair_sum_" + n) for n, own, got in zip(BIG_NAMES, big_send, from_sibling)]
    big_recv = _chip_scatter(chip_partials, "scatter_grads")
    small_list = [stacked(n) for n in SMALL_NAMES if n != "final_norm_w"] + [dfinal[0]]
    small_list += [stacked(n) for n in CONV_NAMES] + [loss_tile[0, 0:1]]
    small_shapes = [a.shape for a in small_list]
    (small_recv,) = _all_gather([_pack(small_list)], "gather_small")
    small_sum = _unpack(_sum_parts(small_recv, "sum_small"), small_shapes)
    loss = small_sum[-1][0]
    small_names = [n for n in SMALL_NAMES if n != "final_norm_w"] + ["final_norm_w"]
    g_out = dict(zip(small_names, small_sum[:len(small_names)]))
    for n, gfull in zip(CONV_NAMES, small_sum[len(small_names):len(small_names) + 2]):
        shard = w[n].shape[-1]
        g_out[n] = lax.dynamic_slice_in_dim(gfull, me * shard, shard, axis=2)

    d_out, m_out, v_out = {}, {}, {}
    for n, parts in zip(BIG_NAMES, big_recv):
        shp = w[n].shape
        two = lambda a: a.reshape(-1, shp[-1])
        g2, d2, m2, v2 = _reduce_adamw(parts, two(w[n]), two(mo[n]), two(vo[n]), "adamw_" + n)
        g_out[n], d_out[n], m_out[n], v_out[n] = (a.reshape(shp) for a in (g2, d2, m2, v2))
    rest = list(small_names) + list(CONV_NAMES)
    rest_shapes = [w[n].shape for n in rest]
    d2, m2, v2 = _adamw(_pack([g_out[n] for n in rest]), _pack([w[n] for n in rest]), _pack([mo[n] for n in rest]),
                        _pack([vo[n] for n in rest]), "adamw_small")
    for n, d, m_, v_ in zip(rest, _unpack(d2, rest_shapes), _unpack(m2, rest_shapes), _unpack(v2, rest_shapes)):
        d_out[n], m_out[n], v_out[n] = d, m_, v_

    return (loss, dx[None], *[g_out[n] for n in WEIGHT_ORDER], *[d_out[n] for n in WEIGHT_ORDER],
            *[m_out[n] for n in WEIGHT_ORDER], *[v_out[n] for n in WEIGHT_ORDER])
```

```python
import functools
import math

import jax
import jax.numpy as jnp
from jax import lax
from jax.experimental import pallas as pl
from jax.experimental.pallas import tpu as pltpu

F32 = jnp.float32
BF16 = jnp.bfloat16

D_MODEL = 2048
DEPTH = 4
S5_WIDTH = 1024
S5_GROUP = 16
S5_GROUPS = 64
S5_STATE = 64
DN_HEADS = 8
DN_DK = 128
DN_QKV = 3072
DN_CONV = 4
DN_CHUNK = 64
FFN_DIM = 5632
FFN_CONV = 3
NORM_EPS = 1e-6
N_IN = 9232
OFF_Z = 4096
OFF_BA = 5120
OFF_GS = 5136
N_MAIN = 9216
MAIN_Z = 4096
MAIN_GS = 5120
MAIN_GD = 7168

ADAM_LR = 0.001
ADAM_B1 = 0.9
ADAM_B2 = 0.999
ADAM_EPS = 1e-08
ADAM_WD = 0.01
ADAM_STEP = 10

N_DEV = 8
LANES = 128
SUBLANES = 8
VMEM_LIMIT_BYTES = 48 * 1024 * 1024

S5_SLABS = 8
S5_SLAB_STATE = 512
S5_TB = 256
DN_TB = 512
PACK_ROWS = 512


def _cparams(sem):
    return pltpu.CompilerParams(dimension_semantics=sem, vmem_limit_bytes=VMEM_LIMIT_BYTES)


def _dot(a, b, dims, precision=None):
    return lax.dot_general(a, b, (dims, ((), ())), precision=precision, preferred_element_type=F32)


_NN = ((1,), (0,))
_NT = ((1,), (1,))
_TN = ((0,), (0,))


def _mm(a, b, mode, out_dtype, name, res=None, tm=1024, tn=1024, tk=2048, place=None):
    if mode == "nn":
        (m, k), (_, n) = a.shape, b.shape
    elif mode == "nt":
        (m, k), (n, _) = a.shape, b.shape
    else:
        (k, m), (_, n) = a.shape, b.shape
    tm, tn, tk = min(tm, m), min(tn, n), min(tk, k)
    assert m % tm == 0 and n % tn == 0 and k % tk == 0, (name, a.shape, b.shape)
    nk = k // tk
    if mode == "tn":
        a_spec = pl.BlockSpec((tk, tm), lambda i, j, kk: (kk, i))
    else:
        a_spec = pl.BlockSpec((tm, tk), lambda i, j, kk: (i, kk))
    if mode == "nt":
        b_spec = pl.BlockSpec((tn, tk), lambda i, j, kk: (j, kk))
    else:
        b_spec = pl.BlockSpec((tk, tn), lambda i, j, kk: (kk, j))
    dims = {"nn": _NN, "nt": _NT, "tn": _TN}[mode]
    o_spec = pl.BlockSpec((tm, tn), lambda i, j, kk: (i, j))
    has_res = res is not None

    has_buf = place is not None and place[3] is not None

    def body(*refs):
        if has_buf:
            refs = refs[:-3] + refs[-2:]
        if has_res:
            a_ref, b_ref, r_ref, o_ref, acc = refs
        else:
            a_ref, b_ref, o_ref, acc = refs
        p = _dot(a_ref[...], b_ref[...], dims)

        def finish(total):
            if has_res:
                total = total + r_ref[...]
            o_ref[...] = total.astype(out_dtype)

        if nk == 1:
            finish(p)
        else:
            kk = pl.program_id(2)

            @pl.when(kk == 0)
            def _():
                acc[...] = p

            @pl.when(jnp.logical_and(kk > 0, kk < nk - 1))
            def _():
                acc[...] += p

            @pl.when(kk == nk - 1)
            def _():
                finish(acc[...] + p)

    in_specs = [a_spec, b_spec] + ([o_spec] if has_res else [])
    args = (a, b) + ((res,) if has_res else ())
    out_shape, out_spec, aliases = jax.ShapeDtypeStruct((m, n), out_dtype), o_spec, {}
    if place is not None:
        shape, block, index_map, buf = place
        out_shape, out_spec = jax.ShapeDtypeStruct(shape, out_dtype), pl.BlockSpec(block, index_map)
        if buf is not None:
            aliases = {len(args): 0}
            in_specs = in_specs + [pl.BlockSpec(memory_space=pl.ANY)]
            args = args + (buf,)
    return pl.pallas_call(
        body, name=name, grid=(m // tm, n // tn, nk), in_specs=in_specs, out_specs=out_spec,
        out_shape=out_shape, input_output_aliases=aliases,
        scratch_shapes=[pltpu.VMEM((tm, tn) if nk > 1 else (SUBLANES, LANES), F32)],
        compiler_params=_cparams(("parallel", "parallel", "arbitrary")),
    )(*args)


def _row_spec(tb, width, cw, off):
    if cw is None:
        return pl.BlockSpec((tb, width), lambda j, i: (i, 0))
    return pl.BlockSpec((tb, cw), lambda j, i: (i, j + off))


def _par_spec(rows, width, cw, off):
    if cw is None:
        return pl.BlockSpec((rows, width), lambda j, i: (0, 0))
    return pl.BlockSpec((rows, cw), lambda j, i: (0, j + off))


def _tile_fwd(name, fn, tiled, params, outs, rows, tb, ncol=1):
    tb = min(tb, rows)
    nt, npar = len(tiled), len(params)

    def body(*refs):
        vals = [r[...] for r in refs[:nt + npar]]
        res = fn(*vals)
        for o_ref, r in zip(refs[nt + npar:], res):
            o_ref[...] = r.astype(o_ref.dtype)

    in_specs = [_row_spec(tb, a.shape[1], cw, off) for a, cw, off in tiled]
    in_specs += [_par_spec(a.shape[0], a.shape[1], cw, off) for a, cw, off in params]
    out_specs = [_row_spec(tb, w, cw, 0) for w, cw, _ in outs]
    out_shape = [jax.ShapeDtypeStruct((rows, w), dt) for w, _, dt in outs]
    return pl.pallas_call(
        body, name=name, grid=(ncol, rows // tb), in_specs=in_specs, out_specs=out_specs, out_shape=out_shape,
        compiler_params=_cparams(("parallel", "parallel")),
    )(*[a for a, _, _ in tiled], *[a for a, _, _ in params])


def _tile_bwd(name, fn, tiled, params, cots, gdtypes, rows, tb, ncol=1):
    tb = min(tb, rows)
    nt, npar, nc = len(tiled), len(params), len(cots)
    want = [i for i, g in enumerate(gdtypes) if g is not None]

    def body(*refs):
        vals = [r[...] for r in refs[:nt + npar]]
        cot_refs = refs[nt + npar:nt + npar + nc]
        g_refs = refs[nt + npar + nc:nt + npar + nc + len(want)]
        p_refs = refs[nt + npar + nc + len(want):]
        _, vjp = jax.vjp(fn, *vals)
        grads = vjp(tuple(c[...].astype(F32) for c in cot_refs))
        for g_ref, i in zip(g_refs, want):
            g_ref[...] = grads[i].astype(g_ref.dtype)
        jcol, irow = pl.program_id(0), pl.program_id(1)
        for p_ref, g, (_, cw, _) in zip(p_refs, grads[nt:], params):
            first = (irow == 0) if cw is not None else jnp.logical_and(irow == 0, jcol == 0)

            @pl.when(first)
            def _():
                p_ref[...] = g

            @pl.when(jnp.logical_not(first))
            def _():
                p_ref[...] += g

    in_specs = [_row_spec(tb, a.shape[1], cw, off) for a, cw, off in tiled]
    in_specs += [_par_spec(a.shape[0], a.shape[1], cw, off) for a, cw, off in params]
    in_specs += [_row_spec(tb, a.shape[1], cw, off) for a, cw, off in cots]
    out_specs, out_shape = [], []
    for i in want:
        a, cw, _ = tiled[i]
        width = a.shape[1] if cw is None else ncol * cw
        out_specs.append(_row_spec(tb, width, cw, 0))
        out_shape.append(jax.ShapeDtypeStruct((rows, width), gdtypes[i]))
    for a, cw, _ in params:
        width = a.shape[1] if cw is None else ncol * cw
        out_specs.append(_par_spec(a.shape[0], width, cw, 0))
        out_shape.append(jax.ShapeDtypeStruct((a.shape[0], width), F32))
    res = pl.pallas_call(
        body, name=name, grid=(ncol, rows // tb), in_specs=in_specs, out_specs=out_specs, out_shape=out_shape,
        compiler_params=_cparams(("arbitrary", "arbitrary")),
    )(*[a for a, _, _ in tiled], *[a for a, _, _ in params], *[a for a, _, _ in cots])
    return res[:len(want)], res[len(want):]


def _sigmoid(x):
    return 1.0 / (1.0 + jnp.exp(-x))


def _silu(x):
    return x * _sigmoid(x)


def _softplus(x):
    return jnp.maximum(x, 0.0) + jnp.log1p(jnp.exp(-jnp.abs(x)))


def _f_rms(x, w):
    return (x * lax.rsqrt(jnp.mean(x * x, axis=-1, keepdims=True) + NORM_EPS) * w,)


def _f_merge(glu_a, glu_b, br_dn, gs, gd):
    return (_sigmoid(gs) * (glu_a * _sigmoid(glu_b)) + _sigmoid(gd) * br_dn,)


def _f_ffn_gate(act, val):
    return (_silu(act) * val,)


def _l2n(x):
    return x * lax.rsqrt(jnp.sum(x * x, axis=-1, keepdims=True) + NORM_EPS)


def _f_dn_q(c):
    return (_l2n(_silu(c)) * (DN_DK ** -0.5),)


def _f_dn_k(c):
    return (_l2n(_silu(c)),)


def _f_dn_v(c):
    return (_silu(c),)


def _f_dn_gates(ba, a_log, dt_bias):
    col = lax.broadcasted_iota(jnp.int32, ba.shape, 1)
    beta = _sigmoid(ba)
    g = -jnp.exp(a_log) * _softplus(ba + dt_bias)
    return (jnp.where(col < DN_HEADS, beta, jnp.where(col < 2 * DN_HEADS, g, 0.0)),)


def _f_dn_post(o, z, w):
    return (_f_rms(o, w)[0] * _silu(z),)


def _shift_down(x, halo, s, tb):
    if s == 0:
        return x
    y = pltpu.roll(x, s, 0)
    row8 = lax.broadcasted_iota(jnp.int32, halo.shape, 0)
    top = jnp.where(row8 < s, pltpu.roll(halo, s, 0), y[0:SUBLANES])
    if tb == SUBLANES:
        return top
    return jnp.concatenate([top, y[SUBLANES:]], axis=0)


def _shift_up(x, halo, s, tb):
    if s == 0:
        return x
    y = pltpu.roll(x, tb - s, 0)
    row8 = lax.broadcasted_iota(jnp.int32, halo.shape, 0)
    bot = jnp.where(row8 >= SUBLANES - s, pltpu.roll(halo, SUBLANES - s, 0), y[tb - SUBLANES:])
    if tb == SUBLANES:
        return bot
    return jnp.concatenate([y[:tb - SUBLANES], bot], axis=0)


def _conv_fwd(x, w, kw, name, x_off=0, width=None, cw=512, tb=512):
    rows = x.shape[0]
    width = w.shape[1] if width is None else width
    tb = min(tb, rows)
    nb = tb // SUBLANES

    def body(x_ref, h_ref, w_ref, o_ref):
        i = pl.program_id(1)
        xv = x_ref[...]
        halo = jnp.where(i > 0, h_ref[...], 0.0)
        acc = w_ref[kw - 1:kw, :] * xv
        for s in range(1, kw):
            acc = acc + w_ref[kw - 1 - s:kw - s, :] * _shift_down(xv, halo, s, tb)
        o_ref[...] = acc

    return pl.pallas_call(
        body, name=name, grid=(width // cw, rows // tb),
        in_specs=[pl.BlockSpec((tb, cw), lambda j, i: (i, j + x_off)),
                  pl.BlockSpec((SUBLANES, cw), lambda j, i: (jnp.maximum(i * nb - 1, 0), j + x_off)),
                  pl.BlockSpec((kw, cw), lambda j, i: (0, j))],
        out_specs=pl.BlockSpec((tb, cw), lambda j, i: (i, j)),
        out_shape=jax.ShapeDtypeStruct((rows, width), F32),
        compiler_params=_cparams(("parallel", "parallel")),
    )(x, x, w)


def _conv_bwd(x, w, dout, kw, name, x_off=0, cw=512, tb=512):
    rows, width = dout.shape
    tb = min(tb, rows)
    nb = tb // SUBLANES
    nrow = rows // tb

    def body(x_ref, h_ref, w_ref, d_ref, dn_ref, dx_ref, dw_ref):
        i = pl.program_id(1)
        xv, dv = x_ref[...], d_ref[...]
        halo = jnp.where(i > 0, h_ref[...], 0.0)
        nxt = jnp.where(i < nrow - 1, dn_ref[...], 0.0)

        @pl.when(i == 0)
        def _():
            dw_ref[...] = jnp.zeros_like(dw_ref)

        acc = w_ref[kw - 1:kw, :] * dv
        dw_ref[kw - 1:kw, :] += jnp.sum(dv * xv, axis=0, keepdims=True)
        for s in range(1, kw):
            acc = acc + w_ref[kw - 1 - s:kw - s, :] * _shift_up(dv, nxt, s, tb)
            dw_ref[kw - 1 - s:kw - s, :] += jnp.sum(dv * _shift_down(xv, halo, s, tb), axis=0, keepdims=True)
        dx_ref[...] = acc

    return pl.pallas_call(
        body, name=name, grid=(width // cw, nrow),
        in_specs=[pl.BlockSpec((tb, cw), lambda j, i: (i, j + x_off)),
                  pl.BlockSpec((SUBLANES, cw), lambda j, i: (jnp.maximum(i * nb - 1, 0), j + x_off)),
                  pl.BlockSpec((kw, cw), lambda j, i: (0, j)),
                  pl.BlockSpec((tb, cw), lambda j, i: (i, j)),
                  pl.BlockSpec((SUBLANES, cw), lambda j, i: (jnp.minimum((i + 1) * nb, rows // SUBLANES - 1), j))],
        out_specs=[pl.BlockSpec((tb, cw), lambda j, i: (i, j)),
                   pl.BlockSpec((SUBLANES, cw), lambda j, i: (0, j))],
        out_shape=[jax.ShapeDtypeStruct((rows, width), F32), jax.ShapeDtypeStruct((SUBLANES, width), F32)],
        compiler_params=_cparams(("parallel", "arbitrary")),
    )(x, x, w, dout, dout)


def _conv_taps(xv, halo, w_ref, kw, tb):
    acc = w_ref[kw - 1:kw, :] * xv
    for s in range(1, kw):
        acc = acc + w_ref[kw - 1 - s:kw - s, :] * _shift_down(xv, halo, s, tb)
    return acc


def _ffn_mix_specs(rows, f, kw, cw, tb):
    nj, nb, last8 = f // cw, tb // SUBLANES, rows // SUBLANES - 1
    blk = lambda off: pl.BlockSpec((tb, cw), lambda j, i: (i, j + off))
    prev = lambda off: pl.BlockSpec((SUBLANES, cw), lambda j, i: (jnp.maximum(i * nb - 1, 0), j + off))
    nxt = lambda off: pl.BlockSpec((SUBLANES, cw), lambda j, i: (jnp.minimum((i + 1) * nb, last8), j + off))
    wsp = lambda off: pl.BlockSpec((kw, cw), lambda j, i: (0, j + off))
    return nj, blk, prev, nxt, wsp


def _ffn_mix_fwd(up, w, name, cw=512, tb=512):
    rows, f, kw = up.shape[0], up.shape[1] // 2, w.shape[0]
    tb = min(tb, rows)
    nj, blk, prev, _, wsp = _ffn_mix_specs(rows, f, kw, cw, tb)

    def body(xa_ref, xv_ref, ha_ref, hv_ref, wa_ref, wv_ref, o_ref):
        first = pl.program_id(1) == 0
        ca = _conv_taps(xa_ref[...], jnp.where(first, 0.0, ha_ref[...]), wa_ref, kw, tb)
        cv = _conv_taps(xv_ref[...], jnp.where(first, 0.0, hv_ref[...]), wv_ref, kw, tb)
        o_ref[...] = _f_ffn_gate(ca, cv)[0].astype(BF16)

    return pl.pallas_call(
        body, name=name, grid=(nj, rows // tb),
        in_specs=[blk(0), blk(nj), prev(0), prev(nj), wsp(0), wsp(nj)],
        out_specs=blk(0), out_shape=jax.ShapeDtypeStruct((rows, f), BF16),
        compiler_params=_cparams(("parallel", "parallel")),
    )(up, up, up, up, w, w)


def _ffn_mix_bwd(up, w, dhid, name, cw=512, tb=512):
    rows, f, kw = up.shape[0], up.shape[1] // 2, w.shape[0]
    tb = min(tb, rows)
    te = tb + SUBLANES
    nrow = rows // tb
    nj, blk, prev, nxt, wsp = _ffn_mix_specs(rows, f, kw, cw, tb)

    def body(xa_ref, xv_ref, ha_ref, hv_ref, na_ref, nv_ref, wa_ref, wv_ref, d_ref, dn_ref,
             da_ref, dv_ref, dwa_ref, dwv_ref):
        i = pl.program_id(1)
        first, last = i == 0, i == nrow - 1

        @pl.when(first)
        def _():
            dwa_ref[...] = jnp.zeros_like(dwa_ref)
            dwv_ref[...] = jnp.zeros_like(dwv_ref)

        def conv_ext(x_ref, h_ref, n_ref, w_ref):
            xe = jnp.concatenate([x_ref[...], jnp.where(last, 0.0, n_ref[...])], axis=0)
            return _conv_taps(xe, jnp.where(first, 0.0, h_ref[...]), w_ref, kw, te)

        dh = jnp.concatenate([d_ref[...], jnp.where(last, 0.0, dn_ref[...])], axis=0)
        _, vjp = jax.vjp(_f_ffn_gate, conv_ext(xa_ref, ha_ref, na_ref, wa_ref), conv_ext(xv_ref, hv_ref, nv_ref, wv_ref))
        dca, dcv = vjp((dh,))

        def back(dc_ext, x_ref, h_ref, w_ref, dx_ref, dw_ref):
            dc = dc_ext[0:tb]
            xv = x_ref[...]
            halo = jnp.where(first, 0.0, h_ref[...])
            acc = w_ref[kw - 1:kw, :] * dc
            dw_ref[kw - 1:kw, :] += jnp.sum(dc * xv, axis=0, keepdims=True)
            for s in range(1, kw):
                acc = acc + w_ref[kw - 1 - s:kw - s, :] * pltpu.roll(dc_ext, te - s, 0)[0:tb]
                dw_ref[kw - 1 - s:kw - s, :] += jnp.sum(dc * _shift_down(xv, halo, s, tb), axis=0, keepdims=True)
            dx_ref[...] = acc.astype(dx_ref.dtype)

        back(dca, xa_ref, ha_ref, wa_ref, da_ref, dwa_ref)
        back(dcv, xv_ref, hv_ref, wv_ref, dv_ref, dwv_ref)

    wide = jax.ShapeDtypeStruct((rows, f), BF16)
    taps = jax.ShapeDtypeStruct((SUBLANES, f), F32)
    tap_spec = pl.BlockSpec((SUBLANES, cw), lambda j, i: (0, j))
    return pl.pallas_call(
        body, name=name, grid=(nj, nrow),
        in_specs=[blk(0), blk(nj), prev(0), prev(nj), nxt(0), nxt(nj), wsp(0), wsp(nj), blk(0), nxt(0)],
        out_specs=[blk(0), blk(0), tap_spec, tap_spec], out_shape=[wide, wide, taps, taps],
        compiler_params=_cparams(("parallel", "arbitrary")),
    )(up, up, up, up, up, up, w, w, dhid, dhid)


DN_PRE_TB = 1024


def _dn_pre_specs(rows, tb):
    nb, last8, off = tb // SUBLANES, rows // SUBLANES - 1, S5_WIDTH // LANES
    blk = pl.BlockSpec((tb, LANES), lambda j, i: (i, j + off))
    prev = pl.BlockSpec((SUBLANES, LANES), lambda j, i: (jnp.maximum(i * nb - 1, 0), j + off))
    nxt = pl.BlockSpec((SUBLANES, LANES), lambda j, i: (jnp.minimum((i + 1) * nb, last8), j + off))
    wsp = pl.BlockSpec((DN_CONV, LANES), lambda j, i: (0, j))
    return blk, prev, nxt, wsp


def _dn_head_kind(j, q_fn, k_fn, v_fn):
    pl.when(j < DN_HEADS)(q_fn)
    pl.when(jnp.logical_and(j >= DN_HEADS, j < 2 * DN_HEADS))(k_fn)
    pl.when(j >= 2 * DN_HEADS)(v_fn)


def _dn_pre_fwd(proj, w, name):
    rows = proj.shape[0]
    tb = min(DN_PRE_TB, rows)
    blk, prev, _, wsp = _dn_pre_specs(rows, tb)

    def body(x_ref, h_ref, w_ref, o_ref):
        first = pl.program_id(1) == 0
        c = _conv_taps(x_ref[...], jnp.where(first, 0.0, h_ref[...]), w_ref, DN_CONV, tb)

        def store(fn):
            def run():
                o_ref[...] = fn(c)[0]
            return run

        _dn_head_kind(pl.program_id(0), store(_f_dn_q), store(_f_dn_k), store(_f_dn_v))

    return pl.pallas_call(
        body, name=name, grid=(DN_QKV // LANES, rows // tb), in_specs=[blk, prev, wsp],
        out_specs=pl.BlockSpec((tb, LANES), lambda j, i: (i, j)),
        out_shape=jax.ShapeDtypeStruct((rows, DN_QKV), F32),
        compiler_params=_cparams(("parallel", "parallel")),
    )(proj, proj, w)


def _dn_pre_bwd(proj, w, dq, dk, dv, name):
    rows = proj.shape[0]
    tb = min(DN_PRE_TB, rows)
    te = tb + SUBLANES
    nrow = rows // tb
    kw = DN_CONV
    blk, prev, nxt, wsp = _dn_pre_specs(rows, tb)
    nb, last8 = tb // SUBLANES, rows // SUBLANES - 1

    def cot_specs(part):
        col = lambda j: jnp.clip(j - part * DN_HEADS, 0, DN_HEADS - 1)
        return (pl.BlockSpec((tb, LANES), lambda j, i: (i, col(j))),
                pl.BlockSpec((SUBLANES, LANES), lambda j, i: (jnp.minimum((i + 1) * nb, last8), col(j))))

    def body(x_ref, h_ref, n_ref, w_ref, dq_ref, dqn_ref, dk_ref, dkn_ref, dv_ref, dvn_ref, dx_ref, dw_ref):
        i = pl.program_id(1)
        first, last = i == 0, i == nrow - 1

        @pl.when(first)
        def _():
            dw_ref[...] = jnp.zeros_like(dw_ref)

        xv = x_ref[...]
        halo = jnp.where(first, 0.0, h_ref[...])
        c_ext = _conv_taps(jnp.concatenate([xv, jnp.where(last, 0.0, n_ref[...])], axis=0), halo, w_ref, kw, te)

        def back(fn, d_ref, dn_ref):
            def run():
                ct = jnp.concatenate([d_ref[...], jnp.where(last, 0.0, dn_ref[...])], axis=0)
                _, vjp = jax.vjp(fn, c_ext)
                (dc_ext,) = vjp((ct,))
                dc = dc_ext[0:tb]
                acc = w_ref[kw - 1:kw, :] * dc
                dw_ref[kw - 1:kw, :] += jnp.sum(dc * xv, axis=0, keepdims=True)
                for s in range(1, kw):
                    acc = acc + w_ref[kw - 1 - s:kw - s, :] * pltpu.roll(dc_ext, te - s, 0)[0:tb]
                    dw_ref[kw - 1 - s:kw - s, :] += jnp.sum(dc * _shift_down(xv, halo, s, tb), axis=0, keepdims=True)
                dx_ref[...] = acc.astype(dx_ref.dtype)
            return run

        _dn_head_kind(pl.program_id(0), back(_f_dn_q, dq_ref, dqn_ref), back(_f_dn_k, dk_ref, dkn_ref),
                      back(_f_dn_v, dv_ref, dvn_ref))

    return pl.pallas_call(
        body, name=name, grid=(DN_QKV // LANES, nrow),
        in_specs=[blk, prev, nxt, wsp, *cot_specs(0), *cot_specs(1), *cot_specs(2)],
        out_specs=[pl.BlockSpec((tb, LANES), lambda j, i: (i, j)),
                   pl.BlockSpec((SUBLANES, LANES), lambda j, i: (0, j))],
        out_shape=[jax.ShapeDtypeStruct((rows, DN_QKV), BF16), jax.ShapeDtypeStruct((SUBLANES, DN_QKV), F32)],
        compiler_params=_cparams(("parallel", "arbitrary")),
    )(proj, proj, proj, w, dq, dq, dk, dk, dv, dv)


def _f_s5_disc(log_dt, a_re, a_im, bt_re, bt_im):
    dt = jnp.exp(log_dt)
    mag = jnp.exp(a_re * dt)
    abar_re, abar_im = mag * jnp.cos(a_im * dt), mag * jnp.sin(a_im * dt)
    den = a_re * a_re + a_im * a_im
    nr, ni = abar_re - 1.0, abar_im
    coef_re = (nr * a_re + ni * a_im) / den
    coef_im = (ni * a_re - nr * a_im) / den
    bbar_re = coef_re[None] * bt_re - coef_im[None] * bt_im
    bbar_im = coef_re[None] * bt_im + coef_im[None] * bt_re
    return abar_re, abar_im, bbar_re, bbar_im


def _s5_disc_fwd(log_dt, a_re, a_im, bt_re, bt_im, name):
    def body(*refs):
        res = _f_s5_disc(*[r[...] for r in refs[:5]])
        for o_ref, r in zip(refs[5:], res):
            o_ref[...] = r

    shp = [a_re, a_re, bt_re, bt_re]
    return pl.pallas_call(body, name=name, out_shape=[jax.ShapeDtypeStruct(s.shape, F32) for s in shp])(
        log_dt, a_re, a_im, bt_re, bt_im)


def _s5_disc_bwd(log_dt, a_re, a_im, bt_re, bt_im, cots, name):
    def body(*refs):
        _, vjp = jax.vjp(_f_s5_disc, *[r[...] for r in refs[:5]])
        grads = vjp(tuple(r[...] for r in refs[5:9]))
        for o_ref, g in zip(refs[9:], grads):
            o_ref[...] = g

    ins = [log_dt, a_re, a_im, bt_re, bt_im]
    return pl.pallas_call(body, name=name, out_shape=[jax.ShapeDtypeStruct(s.shape, F32) for s in ins])(*ins, *cots)


def _cmul(ar, ai, br, bi):
    return ar * br - ai * bi, ar * bi + ai * br


def _seg_scan(xr_ref, xi_ref, ar, ai, cr, ci, pr_ref, pi_ref, tb, reverse):
    sl = tb // SUBLANES
    pitch = _seg_pitch(sl)
    groups = xr_ref.shape[0]
    lane = lambda a, k: a[:, k * LANES:(k + 1) * LANES]
    outs_r, outs_i = [], []
    a_k = [(lane(ar, k), lane(ai, k)) for k in range(groups)]

    def step(i, carry):
        t = sl - 1 - i if reverse else i
        rows = pl.ds(t, SUBLANES, stride=pitch)
        nxt = []
        for k, (xr, xi, pr, pi) in enumerate(carry):
            akr, aki = a_k[k]
            mr, mi = _cmul(akr, aki, xr, xi)
            xr, xi = mr + xr_ref[k, rows, :], mi + xi_ref[k, rows, :]
            xr_ref[k, rows, :] = xr
            xi_ref[k, rows, :] = xi
            pr_ref[k, pl.ds(t, 1), :] = pr
            pi_ref[k, pl.ds(t, 1), :] = pi
            nr, ni = _cmul(akr, aki, pr, pi)
            nxt.append((xr, xi, nr, ni))
        return tuple(nxt)

    zero = jnp.zeros((SUBLANES, LANES), F32)
    lax.fori_loop(0, sl, step, tuple((zero, zero, akr, aki) for akr, aki in a_k))
    last = 0 if reverse else sl - 1
    for k in range(groups):
        qr, qi = pr_ref[k, last:last + 1, :], pi_ref[k, last:last + 1, :]
        tr, ti = pr_ref[k], pi_ref[k]
        ckr, cki = lane(cr, k), lane(ci, k)
        order = range(SUBLANES - 1, -1, -1) if reverse else range(SUBLANES)
        for j in order:
            rows = slice(j * pitch, j * pitch + sl)
            edge = j * pitch if reverse else j * pitch + sl - 1
            er, ei = xr_ref[k, edge:edge + 1, :], xi_ref[k, edge:edge + 1, :]
            mr, mi = _cmul(tr, ti, ckr, cki)
            xr_ref[k, rows, :] += mr
            xi_ref[k, rows, :] += mi
            mr, mi = _cmul(qr, qi, ckr, cki)
            ckr, cki = er + mr, ei + mi
        outs_r.append(ckr)
        outs_i.append(cki)
    return jnp.concatenate(outs_r, axis=1), jnp.concatenate(outs_i, axis=1)


def _seg_pitch(sl):
    return sl + SUBLANES


def _to_groups(ref, val):
    sl = val.shape[0] // SUBLANES
    pitch = _seg_pitch(sl)
    for k in range(ref.shape[0]):
        for j in range(SUBLANES):
            ref[k, j * pitch:j * pitch + sl, :] = val[j * sl:(j + 1) * sl, k * LANES:(k + 1) * LANES]


def _from_groups(ref):
    pitch = ref.shape[1] // SUBLANES
    sl = pitch - SUBLANES
    return jnp.concatenate(
        [jnp.concatenate([ref[k, j * pitch:j * pitch + sl, :] for j in range(SUBLANES)], axis=0)
         for k in range(ref.shape[0])], axis=1)


_INV_SQRT2 = 1.0 / math.sqrt(2.0)
_INV_SQRT2PI = 1.0 / math.sqrt(2.0 * math.pi)


def _gelu(y):
    return 0.5 * y * (1.0 + lax.erf(y * _INV_SQRT2))


def _gelu_grad(y):
    return 0.5 * (1.0 + lax.erf(y * _INV_SQRT2)) + y * jnp.exp(-0.5 * y * y) * _INV_SQRT2PI


def _s5_states(u, bd_re, bd_im, ar, ai, cr, ci, xr_ref, xi_ref, pr_ref, pi_ref, tb):
    ub = u.astype(BF16)
    _to_groups(xr_ref, _dot(ub, bd_re, _NN))
    _to_groups(xi_ref, _dot(ub, bd_im, _NN))
    return _seg_scan(xr_ref, xi_ref, ar, ai, cr, ci, pr_ref, pi_ref, tb, reverse=False)


def _s5_scratch(tb, nbuf):
    groups = S5_SLAB_STATE // LANES
    sl = tb // SUBLANES
    return ([pltpu.VMEM((1, S5_SLAB_STATE), F32)] * 2
            + [pltpu.VMEM((groups, SUBLANES * _seg_pitch(sl), LANES), F32)] * nbuf
            + [pltpu.VMEM((groups, sl, LANES), F32)] * 2)


def _s5_specs(tb, u_off):
    slab3 = lambda r, c: pl.BlockSpec((None, r, c), lambda s, t: (s, 0, 0))
    return dict(
        u=lambda tmap: pl.BlockSpec((tb, LANES), lambda s, t: (tmap(t), s + u_off)),
        bd=slab3(LANES, S5_SLAB_STATE), cd=slab3(S5_SLAB_STATE, LANES), a=slab3(1, S5_SLAB_STATE),
        d=pl.BlockSpec((1, LANES), lambda s, t: (0, s)))


def _s5_fwd(proj, bd_re, bd_im, cd_re, cd_im, a_re, a_im, d, name):
    rows = proj.shape[0]
    tb = min(S5_TB, rows)
    nt = rows // tb
    sp = _s5_specs(tb, 0)

    def body(u_ref, bdr, bdi, cdr, cdi, ar_ref, ai_ref, d_ref, y_ref, sr_ref, si_ref,
             cr_s, ci_s, xr_s, xi_s, pr_s, pi_s):
        t = pl.program_id(1)

        @pl.when(t == 0)
        def _():
            cr_s[...] = jnp.zeros_like(cr_s)
            ci_s[...] = jnp.zeros_like(ci_s)

        cr, ci = cr_s[...], ci_s[...]
        sr_ref[...] = cr
        si_ref[...] = ci
        u = u_ref[...]
        cr, ci = _s5_states(u, bdr[...], bdi[...], ar_ref[...], ai_ref[...], cr, ci, xr_s, xi_s, pr_s, pi_s, tb)
        cr_s[...] = cr
        ci_s[...] = ci
        y = (_dot(_from_groups(xr_s).astype(BF16), cdr[...], _NN)
             - _dot(_from_groups(xi_s).astype(BF16), cdi[...], _NN) + d_ref[...] * u)
        y_ref[...] = _gelu(y).astype(BF16)

    st_spec = pl.BlockSpec((None, None, 1, S5_SLAB_STATE), lambda s, t: (s, t, 0, 0))
    st_shape = jax.ShapeDtypeStruct((S5_SLABS, nt, 1, S5_SLAB_STATE), F32)
    return pl.pallas_call(
        body, name=name, grid=(S5_SLABS, nt),
        in_specs=[sp["u"](lambda t: t), sp["bd"], sp["bd"], sp["cd"], sp["cd"], sp["a"], sp["a"], sp["d"]],
        out_specs=[pl.BlockSpec((tb, LANES), lambda s, t: (t, s)), st_spec, st_spec],
        out_shape=[jax.ShapeDtypeStruct((rows, S5_WIDTH), BF16), st_shape, st_shape],
        scratch_shapes=_s5_scratch(tb, 2),
        compiler_params=_cparams(("parallel", "arbitrary")),
    )(proj, bd_re, bd_im, cd_re, cd_im, a_re, a_im, d)


def _s5_bwd(proj, dy, st_re, st_im, bd_re, bd_im, cd_re, cd_im, a_re, a_im, d, name):
    rows = proj.shape[0]
    tb = min(S5_TB, rows)
    nt = rows // tb
    sp = _s5_specs(tb, 0)
    rev = lambda t: nt - 1 - t

    def body(u_ref, dy_ref, sr_ref, si_ref, bdr, bdi, cdr, cdi, ar_ref, ai_ref, d_ref,
             du_ref, gbr, gbi, gcr, gci, gar, gai, gd_ref, lr_s, li_s, tr_s, ti_s, xr_s, xi_s, pr_s, pi_s):
        t = pl.program_id(1)

        @pl.when(t == 0)
        def _():
            lr_s[...] = jnp.zeros_like(lr_s)
            li_s[...] = jnp.zeros_like(li_s)
            for r in (gbr, gbi, gcr, gci, gar, gai, gd_ref):
                r[...] = jnp.zeros_like(r)

        u = u_ref[...]
        ar, ai = ar_ref[...], ai_ref[...]
        cr, ci = sr_ref[...], si_ref[...]
        _s5_states(u, bdr[...], bdi[...], ar, ai, cr, ci, xr_s, xi_s, pr_s, pi_s, tb)
        xr, xi = _from_groups(xr_s), _from_groups(xi_s)
        xrb, xib = xr.astype(BF16), xi.astype(BF16)
        ypre = _dot(xrb, cdr[...], _NN) - _dot(xib, cdi[...], _NN) + d_ref[...] * u
        dyp = dy_ref[...] * _gelu_grad(ypre)
        dypb = dyp.astype(BF16)
        gd_ref[...] += jnp.sum(dyp * u, axis=0, keepdims=True)
        gcr[...] += _dot(xrb, dypb, _TN)
        gci[...] -= _dot(xib, dypb, _TN)
        _to_groups(tr_s, _dot(dypb, cdr[...], _NT))
        _to_groups(ti_s, -_dot(dypb, cdi[...], _NT))
        nr, ni = _seg_scan(tr_s, ti_s, ar, -ai, lr_s[...], li_s[...], pr_s, pi_s, tb, reverse=True)
        lr_s[...] = nr
        li_s[...] = ni
        lr, li = _from_groups(tr_s), _from_groups(ti_s)
        row = lax.broadcasted_iota(jnp.int32, (tb, 1), 0)
        lrb, lib = lr.astype(BF16), li.astype(BF16)
        du_ref[...] = _dot(lrb, bdr[...], _NT) + _dot(lib, bdi[...], _NT) + d_ref[...] * dyp
        ub = u.astype(BF16)
        gbr[...] += _dot(ub, lrb, _TN)
        gbi[...] += _dot(ub, lib, _TN)
        xpr = jnp.where(row == 0, cr, pltpu.roll(xr, 1, 0))
        xpi = jnp.where(row == 0, ci, pltpu.roll(xi, 1, 0))
        gar[...] += jnp.sum(lr * xpr + li * xpi, axis=0, keepdims=True)
        gai[...] += jnp.sum(li * xpr - lr * xpi, axis=0, keepdims=True)

    st_spec = pl.BlockSpec((None, None, 1, S5_SLAB_STATE), lambda s, t: (s, rev(t), 0, 0))
    slab = lambda r, c: pl.BlockSpec((None, r, c), lambda s, t: (s, 0, 0))
    return pl.pallas_call(
        body, name=name, grid=(S5_SLABS, nt),
        in_specs=[sp["u"](rev), pl.BlockSpec((tb, LANES), lambda s, t: (rev(t), s)), st_spec, st_spec,
                  sp["bd"], sp["bd"], sp["cd"], sp["cd"], sp["a"], sp["a"], sp["d"]],
        out_specs=[pl.BlockSpec((tb, LANES), lambda s, t: (rev(t), s)),
                   slab(LANES, S5_SLAB_STATE), slab(LANES, S5_SLAB_STATE),
                   slab(S5_SLAB_STATE, LANES), slab(S5_SLAB_STATE, LANES),
                   slab(1, S5_SLAB_STATE), slab(1, S5_SLAB_STATE),
                   pl.BlockSpec((1, LANES), lambda s, t: (0, s))],
        out_shape=[jax.ShapeDtypeStruct((rows, S5_WIDTH), F32),
                   jax.ShapeDtypeStruct((S5_SLABS, LANES, S5_SLAB_STATE), F32),
                   jax.ShapeDtypeStruct((S5_SLABS, LANES, S5_SLAB_STATE), F32),
                   jax.ShapeDtypeStruct((S5_SLABS, S5_SLAB_STATE, LANES), F32),
                   jax.ShapeDtypeStruct((S5_SLABS, S5_SLAB_STATE, LANES), F32),
                   jax.ShapeDtypeStruct((S5_SLABS, 1, S5_SLAB_STATE), F32),
                   jax.ShapeDtypeStruct((S5_SLABS, 1, S5_SLAB_STATE), F32),
                   jax.ShapeDtypeStruct((1, S5_WIDTH), F32)],
        scratch_shapes=_s5_scratch(tb, 4),
        compiler_params=_cparams(("parallel", "arbitrary")),
    )(proj, dy, st_re, st_im, bd_re, bd_im, cd_re, cd_im, a_re, a_im, d)


@functools.partial(jax.custom_vjp, nondiff_argnums=(2,))
def _bdot(a, b, dims):
    return _dot(a.astype(BF16), b.astype(BF16), dims)


def _bdot_fwd(a, b, dims):
    return _bdot(a, b, dims), (a, b)


def _bdot_bwd(dims, res, ct):
    a, b = res
    if dims == _NN:
        return _bdot(ct, b, _NT), _bdot(a, ct, _TN)
    if dims == _NT:
        return _bdot(ct, b, _NN), _bdot(ct, a, _TN)
    return _bdot(b, ct, _NT), _bdot(a, ct, _NN)


_bdot.defvjp(_bdot_fwd, _bdot_bwd)


def _split_bf16(a):
    hi = a.astype(BF16)
    return hi, (a - hi.astype(F32)).astype(BF16)


@functools.partial(jax.custom_vjp, nondiff_argnums=(2,))
def _dot3(a, b, dims):
    ah, al = _split_bf16(a)
    bh, bl = _split_bf16(b)
    return _dot(ah, bh, dims) + (_dot(ah, bl, dims) + _dot(al, bh, dims))


def _dot3_fwd(a, b, dims):
    return _dot3(a, b, dims), (a, b)


def _dot3_bwd(dims, res, ct):
    a, b = res
    if dims == _NN:
        return _dot3(ct, b, _NT), _dot3(a, ct, _TN)
    if dims == _NT:
        return _dot3(ct, b, _NN), _dot3(ct, a, _TN)
    return _dot3(b, ct, _NT), _dot3(a, ct, _NN)


_dot3.defvjp(_dot3_fwd, _dot3_bwd)


def _tril_ones(c):
    r = lax.broadcasted_iota(jnp.int32, (c, c), 0)
    col = lax.broadcasted_iota(jnp.int32, (c, c), 1)
    return jnp.where(r >= col, 1.0, 0.0).astype(BF16)


@jax.custom_vjp
def _chunk_cumsum(x):
    xh, xl = _split_bf16(x)
    t = _tril_ones(x.shape[0])
    return _dot(t, xh, _NN) + _dot(t, xl, _NN)


def _chunk_cumsum_fwd(x):
    return _chunk_cumsum(x), None


def _chunk_cumsum_bwd(_, ct):
    ch, cl = _split_bf16(ct)
    t = _tril_ones(ct.shape[0])
    return (_dot(t, ch, _TN) + _dot(t, cl, _TN),)


_chunk_cumsum.defvjp(_chunk_cumsum_fwd, _chunk_cumsum_bwd)


def _unit_lower_inverses(lms, n):
    r = lax.broadcasted_iota(jnp.int32, (n, n), 0)
    c = lax.broadcasted_iota(jnp.int32, (n, n), 1)
    eye = jnp.where(r == c, 1.0, 0.0)
    ps = [eye - lm for lm in lms]
    powers = list(lms)
    steps = int(math.log2(n)) - 1
    for _ in range(steps):
        powers = [_dot3(x, x, _NN) for x in powers]
        ps = [p + _dot3(p, x, _NN) for p, x in zip(ps, powers)]
    return ps


def _dn_chunk(qs, ks, vs, gs, bs, ss):
    c = qs[0].shape[0]
    r = lax.broadcasted_iota(jnp.int32, (c, c), 0)
    col = lax.broadcasted_iota(jnp.int32, (c, c), 1)
    tril = r >= col
    strict = r > col
    gls = [jnp.broadcast_to(g, (c, LANES)) for g in gs]
    gcs = [_chunk_cumsum(gl) for gl in gls]
    gtots = [jnp.sum(gl, axis=0, keepdims=True) for gl in gls]
    gdiffs = [_chunk_cumsum(jnp.where(strict, jnp.broadcast_to(g, (c, c)), 0.0)) for g in gs]
    decays = [jnp.where(tril, jnp.exp(jnp.where(tril, gd, 0.0)), 0.0) for gd in gdiffs]
    kbs = [k * b for k, b in zip(ks, bs)]
    vbs = [v * b for v, b in zip(vs, bs)]
    lmats = [jnp.where(strict, _bdot(kb, k, _NT) * d, 0.0) for kb, k, d in zip(kbs, ks, decays)]
    attns = [jnp.where(tril, _bdot(q, k, _NT) * d, 0.0) for q, k, d in zip(qs, ks, decays)]
    tinvs = _unit_lower_inverses(lmats, c)
    us = [_dot3(t, vb, _NN) for t, vb in zip(tinvs, vbs)]
    ws = [_dot3(t, kb * jnp.exp(gc), _NN) for t, kb, gc in zip(tinvs, kbs, gcs)]
    ws_s = [_bdot(w, s, _NN) for w, s in zip(ws, ss)]
    qs_s = [_bdot(q * jnp.exp(gc), s, _NN) for q, gc, s in zip(qs, gcs, ss)]
    v_news = [u - x for u, x in zip(us, ws_s)]
    os_ = [x + _bdot(a, vn, _NN) for x, a, vn in zip(qs_s, attns, v_news)]
    s_news = [s * jnp.exp(gt) + _bdot(k * jnp.exp(gt - gc), vn, _TN)
              for s, gt, k, gc, vn in zip(ss, gtots, ks, gcs, v_news)]
    return tuple(os_), tuple(s_news)


def _dn_specs(tb, hb, tmap):
    groups = DN_HEADS // hb
    blk = lambda part: pl.BlockSpec((tb, hb * LANES), lambda hg, t: (tmap(t), hg + part * groups))
    colv = pl.BlockSpec((hb, tb, 1), lambda hg, t: (hg, tmap(t), 0))
    st = pl.BlockSpec((hb, tb // DN_CHUNK, DN_DK, DN_DK), lambda hg, t: (hg, tmap(t), 0, 0))
    return blk, colv, st


def _dn_fwd(qkv, gcol, bcol, name, hb=DN_HEADS):
    rows = qkv.shape[0]
    tb = min(DN_TB, rows)
    nt = rows // tb
    nch = tb // DN_CHUNK
    blk, colv, st = _dn_specs(tb, hb, lambda t: t)

    def body(q_ref, k_ref, v_ref, g_ref, b_ref, o_ref, st_ref, s_scr):
        @pl.when(pl.program_id(1) == 0)
        def _():
            s_scr[...] = jnp.zeros_like(s_scr)

        def chunk(ci, carry):
            rs = pl.ds(pl.multiple_of(ci * DN_CHUNK, DN_CHUNK), DN_CHUNK)
            cols = [slice(j * LANES, (j + 1) * LANES) for j in range(hb)]
            s_in = tuple(s_scr[j] for j in range(hb))
            for j in range(hb):
                st_ref[j, ci] = s_in[j]
            os_, s_new = _dn_chunk(tuple(q_ref[rs, cs] for cs in cols), tuple(k_ref[rs, cs] for cs in cols),
                                   tuple(v_ref[rs, cs] for cs in cols), tuple(g_ref[j, rs, :] for j in range(hb)),
                                   tuple(b_ref[j, rs, :] for j in range(hb)), s_in)
            for j in range(hb):
                o_ref[rs, cols[j]] = os_[j]
                s_scr[j] = s_new[j]
            return carry

        lax.fori_loop(0, nch, chunk, 0)

    return pl.pallas_call(
        body, name=name, grid=(DN_HEADS // hb, nt),
        in_specs=[blk(0), blk(1), blk(2), colv, colv],
        out_specs=[blk(0), st],
        out_shape=[jax.ShapeDtypeStruct((rows, DN_HEADS * DN_DK), F32),
                   jax.ShapeDtypeStruct((DN_HEADS, rows // DN_CHUNK, DN_DK, DN_DK), F32)],
        scratch_shapes=[pltpu.VMEM((hb, DN_DK, DN_DK), F32)],
        compiler_params=_cparams(("parallel", "arbitrary")),
    )(qkv, qkv, qkv, gcol, bcol)


def _dn_bwd(qkv, gcol, bcol, states, do, name, hb=DN_HEADS // 2):
    rows = qkv.shape[0]
    tb = min(DN_TB, rows)
    nt = rows // tb
    nch = tb // DN_CHUNK
    blk, colv, st = _dn_specs(tb, hb, lambda t: nt - 1 - t)

    def body(q_ref, k_ref, v_ref, g_ref, b_ref, st_ref, do_ref, dq_ref, dk_ref, dv_ref, dg_ref, db_ref, ds_scr):
        @pl.when(pl.program_id(1) == 0)
        def _():
            ds_scr[...] = jnp.zeros_like(ds_scr)

        def chunk(cj, carry):
            ci = nch - 1 - cj
            rs = pl.ds(pl.multiple_of(ci * DN_CHUNK, DN_CHUNK), DN_CHUNK)
            cols = [slice(j * LANES, (j + 1) * LANES) for j in range(hb)]
            heads = range(hb)
            args = (tuple(q_ref[rs, cs] for cs in cols), tuple(k_ref[rs, cs] for cs in cols),
                    tuple(v_ref[rs, cs] for cs in cols), tuple(g_ref[j, rs, :] for j in heads),
                    tuple(b_ref[j, rs, :] for j in heads), tuple(st_ref[j, ci] for j in heads))
            _, vjp = jax.vjp(_dn_chunk, *args)
            dq, dk, dv, dg, db, ds = vjp((tuple(do_ref[rs, cs] for cs in cols), tuple(ds_scr[j] for j in heads)))
            for j in heads:
                dq_ref[rs, cols[j]] = dq[j]
                dk_ref[rs, cols[j]] = dk[j]
                dv_ref[rs, cols[j]] = dv[j]
                dg_ref[j, rs, :] = dg[j]
                db_ref[j, rs, :] = db[j]
                ds_scr[j] = ds[j]
            return carry

        lax.fori_loop(0, nch, chunk, 0)

    wide = jax.ShapeDtypeStruct((rows, DN_HEADS * DN_DK), F32)
    narrow = jax.ShapeDtypeStruct((DN_HEADS, rows, 1), F32)
    return pl.pallas_call(
        body, name=name, grid=(DN_HEADS // hb, nt),
        in_specs=[blk(0), blk(1), blk(2), colv, colv, st, blk(0)],
        out_specs=[blk(0), blk(0), blk(0), colv, colv],
        out_shape=[wide, wide, wide, narrow, narrow],
        scratch_shapes=[pltpu.VMEM((hb, DN_DK, DN_DK), F32)],
        compiler_params=_cparams(("parallel", "arbitrary")),
    )(qkv, qkv, qkv, gcol, bcol, states, do)


def _final_loss(x, w, target, name, tb=256):
    rows, width = x.shape
    tb = min(tb, rows)

    def body(x_ref, w_ref, t_ref, dx_ref, dw_ref, loss_ref):
        i = pl.program_id(0)
        (y,), vjp = jax.vjp(_f_rms, x_ref[...], w_ref[...])
        err = y - t_ref[...]
        part = 0.5 * jnp.sum(jnp.mean(err * err, axis=-1, keepdims=True), axis=0, keepdims=True)
        dx, dw = vjp((err * (1.0 / width),))
        dx_ref[...] = dx

        @pl.when(i == 0)
        def _():
            dw_ref[...] = dw
            loss_ref[...] = jnp.broadcast_to(part, loss_ref.shape)

        @pl.when(i > 0)
        def _():
            dw_ref[...] += dw
            loss_ref[...] += jnp.broadcast_to(part, loss_ref.shape)

    row = pl.BlockSpec((tb, width), lambda i: (i, 0))
    par = pl.BlockSpec((1, width), lambda i: (0, 0))
    return pl.pallas_call(
        body, name=name, grid=(rows // tb,), in_specs=[row, par, row],
        out_specs=[row, par, pl.BlockSpec((SUBLANES, LANES), lambda i: (0, 0))],
        out_shape=[jax.ShapeDtypeStruct((rows, width), F32), jax.ShapeDtypeStruct((1, width), F32),
                   jax.ShapeDtypeStruct((SUBLANES, LANES), F32)],
        compiler_params=_cparams(("arbitrary",)),
    )(x, w, target)


_ANY = pl.BlockSpec(memory_space=pl.ANY)
N_CHIPS = 4


def _mesh_place():
    x, y, c = lax.axis_index("x"), lax.axis_index("y"), lax.axis_index("c")
    other_chips = [(1 - x, y), (x, 1 - y), (1 - x, 1 - y)]
    return x, y, c, other_chips


def _remote(src, dst, send_sem, recv_sem, dev):
    return pltpu.make_async_remote_copy(src_ref=src, dst_ref=dst, send_sem=send_sem, recv_sem=recv_sem,
                                        device_id=dev, device_id_type=pl.DeviceIdType.MESH)


def _all_gather(arrs, name):
    n = len(arrs)

    def body(*refs):
        ins, outs = refs[:n], refs[n:2 * n]
        send_sems, recv_sems, loc_sems = refs[2 * n:]
        x, y, c, chips = _mesh_place()
        me, sibling = (x, y, c), (x, y, 1 - c)
        slot = lambda px, py, pc: 4 * px + 2 * py + pc
        local = []
        for i in range(n):
            cp = pltpu.make_async_copy(ins[i], outs[i].at[slot(*me)], loc_sems.at[i])
            cp.start()
            local.append(cp)
        sends = []
        for i in range(n):
            cp = _remote(ins[i], outs[i].at[slot(*me)], send_sems.at[i, 0], recv_sems.at[i, 0], sibling)
            cp.start()
            sends.append(cp)
            for j, chip in enumerate(chips):
                cp = _remote(ins[i], outs[i].at[slot(*me)], send_sems.at[i, 1 + j], recv_sems.at[i, 1 + j], (*chip, c))
                cp.start()
                sends.append(cp)
        for j, chip in enumerate(chips):
            for i in range(n):
                got = outs[i].at[slot(*chip, c)]
                _remote(got, got, send_sems.at[i, 1 + j], recv_sems.at[i, 1 + j], (*chip, c)).wait_recv()
                cp = _remote(got, got, send_sems.at[i, 4 + j], recv_sems.at[i, 4 + j], sibling)
                cp.start()
                sends.append(cp)
        for i in range(n):
            got = outs[i].at[slot(*sibling)]
            _remote(got, got, send_sems.at[i, 0], recv_sems.at[i, 0], sibling).wait_recv()
            for j, chip in enumerate(chips):
                got = outs[i].at[slot(*chip, 1 - c)]
                _remote(got, got, send_sems.at[i, 4 + j], recv_sems.at[i, 4 + j], sibling).wait_recv()
        for cp in sends:
            cp.wait_send()
        for cp in local:
            cp.wait()

    return pl.pallas_call(
        body, name=name, in_specs=[_ANY] * n, out_specs=[_ANY] * n,
        out_shape=[jax.ShapeDtypeStruct((N_DEV,) + a.shape, a.dtype) for a in arrs],
        scratch_shapes=[pltpu.SemaphoreType.DMA((n, N_DEV - 1)), pltpu.SemaphoreType.DMA((n, N_DEV - 1)),
                        pltpu.SemaphoreType.DMA((n,))],
    )(*arrs)


def _sibling_swap(arrs, name):
    n = len(arrs)

    def body(*refs):
        ins, outs = refs[:n], refs[n:2 * n]
        send_sems, recv_sems = refs[2 * n:]
        x, y, c, _ = _mesh_place()
        sibling = (x, y, 1 - c)
        cps = [_remote(ins[i].at[1 - c], outs[i], send_sems.at[i], recv_sems.at[i], sibling) for i in range(n)]
        for cp in cps:
            cp.start()
        for cp in cps:
            cp.wait()

    return pl.pallas_call(
        body, name=name, in_specs=[_ANY] * n, out_specs=[_ANY] * n,
        out_shape=[jax.ShapeDtypeStruct(a.shape[1:], a.dtype) for a in arrs],
        scratch_shapes=[pltpu.SemaphoreType.DMA((n,)), pltpu.SemaphoreType.DMA((n,))],
    )(*arrs)


def _chip_scatter(arrs, name):
    n = len(arrs)

    def body(*refs):
        ins, outs = refs[:n], refs[n:2 * n]
        send_sems, recv_sems, loc_sems = refs[2 * n:]
        x, y, c, chips = _mesh_place()
        mine = 2 * x + y
        local = []
        for i in range(n):
            cp = pltpu.make_async_copy(ins[i].at[mine], outs[i].at[mine], loc_sems.at[i])
            cp.start()
            local.append(cp)
        sends = []
        for i in range(n):
            for j, (px, py) in enumerate(chips):
                cp = _remote(ins[i].at[2 * px + py], outs[i].at[mine], send_sems.at[i, j], recv_sems.at[i, j],
                             (px, py, c))
                cp.start()
                sends.append(cp)
        for i in range(n):
            for j, (px, py) in enumerate(chips):
                got = outs[i].at[2 * px + py]
                _remote(got, got, send_sems.at[i, j], recv_sems.at[i, j], (px, py, c)).wait_recv()
        for cp in sends:
            cp.wait_send()
        for cp in local:
            cp.wait()

    return pl.pallas_call(
        body, name=name, in_specs=[_ANY] * n, out_specs=[_ANY] * n,
        out_shape=[jax.ShapeDtypeStruct(a.shape, a.dtype) for a in arrs],
        scratch_shapes=[pltpu.SemaphoreType.DMA((n, N_CHIPS - 1)), pltpu.SemaphoreType.DMA((n, N_CHIPS - 1)),
                        pltpu.SemaphoreType.DMA((n,))],
    )(*arrs)


def _pair_sum(own, got, name):
    _, _, rows, cols = own.shape
    tb = _row_tile(rows, cols, budget=512 * 1024)

    def body(c_ref, a_ref, b_ref, o_ref):
        o_ref[...] = (a_ref[...].astype(F32) + b_ref[...].astype(F32)).astype(o_ref.dtype)

    core = lax.axis_index("c").astype(jnp.int32).reshape(1)
    grid_spec = pltpu.PrefetchScalarGridSpec(
        num_scalar_prefetch=1, grid=(N_CHIPS, rows // tb),
        in_specs=[pl.BlockSpec((None, None, tb, cols), lambda p, i, c_ref: (c_ref[0], p, i, 0)),
                  pl.BlockSpec((None, tb, cols), lambda p, i, c_ref: (p, i, 0))],
        out_specs=pl.BlockSpec((None, tb, cols), lambda p, i, c_ref: (p, i, 0)))
    return pl.pallas_call(
        body, name=name, grid_spec=grid_spec, out_shape=jax.ShapeDtypeStruct(got.shape, got.dtype),
        compiler_params=_cparams(("parallel", "parallel")),
    )(core, own, got)


def _adamw_math(w, g, m, v):
    m = ADAM_B1 * m + (1.0 - ADAM_B1) * g
    v = ADAM_B2 * v + (1.0 - ADAM_B2) * (g * g)
    m_hat = m / (1.0 - ADAM_B1 ** ADAM_STEP)
    v_hat = v / (1.0 - ADAM_B2 ** ADAM_STEP)
    delta = -ADAM_LR * (m_hat / (jnp.sqrt(v_hat) + ADAM_EPS) + ADAM_WD * w)
    return delta, m, v


def _row_tile(rows, cols, budget=128 * 1024):
    if rows * cols <= budget or rows % SUBLANES:
        return rows
    best = SUBLANES
    for t in range(SUBLANES, rows + 1, SUBLANES):
        if rows % t == 0 and t * cols <= budget:
            best = t
    return best


def _reduce_adamw(parts, w, m, v, name):
    rows, cols = w.shape
    nparts = parts.shape[0]
    tb = _row_tile(rows, cols)

    def body(p_ref, w_ref, m_ref, v_ref, g_ref, d_ref, nm_ref, nv_ref):
        g = p_ref[0].astype(F32)
        for s in range(1, nparts):
            g = g + p_ref[s].astype(F32)
        delta, nm, nv = _adamw_math(w_ref[...], g, m_ref[...], v_ref[...])
        g_ref[...] = g
        d_ref[...] = delta
        nm_ref[...] = nm
        nv_ref[...] = nv

    spec = pl.BlockSpec((tb, cols), lambda i: (i, 0))
    shp = jax.ShapeDtypeStruct((rows, cols), F32)
    return pl.pallas_call(
        body, name=name, grid=(rows // tb,),
        in_specs=[pl.BlockSpec((nparts, tb, cols), lambda i: (0, i, 0)), spec, spec, spec],
        out_specs=[spec] * 4, out_shape=[shp] * 4,
        compiler_params=_cparams(("parallel",)),
    )(parts, w, m, v)


def _sum_parts(parts, name):
    nparts, rows, cols = parts.shape
    tb = _row_tile(rows, cols)

    def body(p_ref, g_ref):
        g = p_ref[0]
        for s in range(1, nparts):
            g = g + p_ref[s]
        g_ref[...] = g

    return pl.pallas_call(
        body, name=name, grid=(rows // tb,),
        in_specs=[pl.BlockSpec((nparts, tb, cols), lambda i: (0, i, 0))],
        out_specs=pl.BlockSpec((tb, cols), lambda i: (i, 0)),
        out_shape=jax.ShapeDtypeStruct((rows, cols), F32),
        compiler_params=_cparams(("parallel",)),
    )(parts)


def _adamw(g, w, m, v, name):
    rows, cols = w.shape
    tb = _row_tile(rows, cols)

    def body(g_ref, w_ref, m_ref, v_ref, d_ref, nm_ref, nv_ref):
        delta, nm, nv = _adamw_math(w_ref[...], g_ref[...], m_ref[...], v_ref[...])
        d_ref[...] = delta
        nm_ref[...] = nm
        nv_ref[...] = nv

    spec = pl.BlockSpec((tb, cols), lambda i: (i, 0))
    shp = jax.ShapeDtypeStruct((rows, cols), F32)
    return pl.pallas_call(
        body, name=name, grid=(rows // tb,), in_specs=[spec] * 4, out_specs=[spec] * 3, out_shape=[shp] * 3,
        compiler_params=_cparams(("parallel",)),
    )(g, w, m, v)


def _pack(arrs):
    flat = jnp.concatenate([a.reshape(-1) for a in arrs])
    pad = (-flat.shape[0]) % (PACK_ROWS * LANES)
    return jnp.pad(flat, (0, pad)).reshape(-1, LANES)


def _unpack(packed, shapes):
    flat = packed.reshape(-1)
    out, off = [], 0
    for s in shapes:
        n = math.prod(s)
        out.append(flat[off:off + n].reshape(s))
        off += n
    return out


def _block_diag_b(bt):
    bb = bt.transpose(1, 0, 2).reshape(S5_SLABS, 8, S5_GROUP, S5_STATE)
    eye = jnp.eye(8, dtype=bt.dtype)
    return (bb[:, :, :, None, :] * eye[None, :, None, :, None]).reshape(S5_SLABS, LANES, S5_SLAB_STATE)


def _block_diag_b_grad(g):
    g5 = g.reshape(S5_SLABS, 8, S5_GROUP, 8, S5_STATE)
    diag = jnp.stack([g5[:, a, :, a, :] for a in range(8)], axis=1)
    return diag.reshape(S5_GROUPS, S5_GROUP, S5_STATE).transpose(1, 0, 2)


def _block_diag_c(cw):
    cc = cw.reshape(S5_SLABS, 8, S5_GROUP, S5_STATE).transpose(0, 1, 3, 2)
    eye = jnp.eye(8, dtype=cw.dtype)
    return (cc[:, :, :, None, :] * eye[None, :, None, :, None]).reshape(S5_SLABS, S5_SLAB_STATE, LANES)


def _block_diag_c_grad(g):
    g5 = g.reshape(S5_SLABS, 8, S5_STATE, 8, S5_GROUP)
    diag = jnp.stack([g5[:, a, :, a, :] for a in range(8)], axis=1)
    return diag.transpose(0, 1, 3, 2).reshape(S5_GROUPS, S5_GROUP, S5_STATE)


def _cols_full(gathered):
    _, k, n = gathered.shape
    return gathered.transpose(1, 0, 2).reshape(k, N_DEV * n)


def _cols_split(full):
    k, n8 = full.shape
    return full.reshape(k, N_DEV, n8 // N_DEV).transpose(1, 0, 2)


SMALL_NAMES = ("mix_norm_w", "s5_log_dt", "s5_a_re", "s5_a_im", "s5_b_re", "s5_b_im", "s5_c_re", "s5_c_im", "s5_d",
               "dn_a_log", "dn_dt_bias", "dn_norm_w", "ffn_norm_w", "final_norm_w")
CONV_NAMES = ("dn_conv_w", "ffn_conv_w")
BIG_NAMES = ("w_in", "s5_glu_w", "dn_proj_w", "w_out", "ffn_up", "ffn_down")
ROW_SHARDED = ("w_out", "ffn_down")
WEIGHT_ORDER = ("mix_norm_w", "w_in", "s5_log_dt", "s5_a_re", "s5_a_im", "s5_b_re", "s5_b_im", "s5_c_re", "s5_c_im",
                "s5_d", "s5_glu_w", "dn_conv_w", "dn_a_log", "dn_dt_bias", "dn_norm_w", "dn_proj_w", "w_out",
                "ffn_norm_w", "ffn_up", "ffn_conv_w", "ffn_down", "final_norm_w")


def _layer_forward(l, x, wts, sm):
    rows = x.shape[0]
    sv = {"x0": x}
    nm = f"l{l}_"
    mixw = sm["mix_norm_w"][l][None]
    (h,) = _tile_fwd(nm + "mix_norm", _f_rms, [(x, None, 0)], [(mixw, None, 0)], [(D_MODEL, None, BF16)], rows, 256)
    proj = _mm(h, wts["w_in_main"], "nn", F32, nm + "proj")
    ba = _mm(h, wts["w_in_ba"], "nn", F32, nm + "proj_ba")
    sv.update(h=h, proj=proj, ba=ba)
    disc = _s5_disc_fwd(sm["s5_log_dt"][l][:, None], sm["s5_a_re"][l], sm["s5_a_im"][l],
                        sm["s5_b_re"][l].transpose(2, 0, 1), sm["s5_b_im"][l].transpose(2, 0, 1), nm + "s5_disc")
    abar_re, abar_im, bbar_re, bbar_im = disc
    s5p = dict(
        bd_re=_block_diag_b(bbar_re).astype(BF16), bd_im=_block_diag_b(bbar_im).astype(BF16),
        cd_re=_block_diag_c(sm["s5_c_re"][l]).astype(BF16), cd_im=_block_diag_c(sm["s5_c_im"][l]).astype(BF16),
        a_re=abar_re.reshape(S5_SLABS, 1, S5_SLAB_STATE), a_im=abar_im.reshape(S5_SLABS, 1, S5_SLAB_STATE),
        d=sm["s5_d"][l][None])
    y_s5, st_re, st_im = _s5_fwd(proj, s5p["bd_re"], s5p["bd_im"], s5p["cd_re"], s5p["cd_im"],
                                 s5p["a_re"], s5p["a_im"], s5p["d"], nm + "s5_scan")
    glu = _mm(y_s5, wts["s5_glu_w"], "nn", F32, nm + "glu")
    sv.update(s5p=s5p, y_s5=y_s5, st_re=st_re, st_im=st_im, glu=glu)
    qkv = _dn_pre_fwd(proj, wts["dn_conv_w"], nm + "dn_pre")
    pad8 = lambda a: jnp.pad(a[None], ((0, 0), (DN_HEADS, LANES - 2 * DN_HEADS)))
    alog, dtb = pad8(sm["dn_a_log"][l]), pad8(sm["dn_dt_bias"][l])
    bg, = _tile_fwd(nm + "dn_gates", _f_dn_gates, [(ba, None, 0)], [(alog, None, 0), (dtb, None, 0)],
                    [(LANES, None, F32)], rows, 512)
    bcol = bg[:, 0:DN_HEADS].T[:, :, None]
    gcol = bg[:, DN_HEADS:2 * DN_HEADS].T[:, :, None]
    o, states = _dn_fwd(qkv, gcol, bcol, nm + "dn_chunk")
    dnw = sm["dn_norm_w"][l][None]
    y_dn, = _tile_fwd(nm + "dn_post", _f_dn_post, [(o, LANES, 0), (proj, LANES, MAIN_Z // LANES)], [(dnw, None, 0)],
                      [(1024, LANES, BF16)], rows, 512, ncol=8)
    br_dn = _mm(y_dn, wts["dn_proj_w"], "nn", F32, nm + "dn_proj")
    sv.update(qkv=qkv, alog=alog, dtb=dtb, bcol=bcol, gcol=gcol, o=o, states=states, dnw=dnw,
              y_dn=y_dn, br_dn=br_dn)
    cw = 512
    merged, = _tile_fwd(nm + "merge", _f_merge,
                        [(glu, cw, 0), (glu, cw, D_MODEL // cw), (br_dn, cw, 0),
                         (proj, cw, MAIN_GS // cw), (proj, cw, MAIN_GD // cw)], [],
                        [(D_MODEL, cw, BF16)], rows, 512, ncol=D_MODEL // cw)
    x1 = _mm(merged, wts["w_out"], "nn", F32, nm + "w_out", res=x)
    sv.update(merged=merged, x1=x1)
    ffw = sm["ffn_norm_w"][l][None]
    (h2,) = _tile_fwd(nm + "ffn_norm", _f_rms, [(x1, None, 0)], [(ffw, None, 0)], [(D_MODEL, None, BF16)], rows, 256)
    up = _mm(h2, wts["ffn_up"], "nn", F32, nm + "ffn_up")
    hid = _ffn_mix_fwd(up, wts["ffn_conv_w"], nm + "ffn_mix")
    x2 = _mm(hid, wts["ffn_down"], "nn", F32, nm + "ffn_down", res=x1, tk=1408)
    sv.update(h2=h2, up=up, hid=hid)
    return x2, sv


def _wgrad(name, act, dout, l, depth, bufs, call_name):
    kdim, ndim = act.shape[1], dout.shape[1]
    if name not in PLACED_NAMES:
        return _mm(act, dout, "tn", BF16, call_name, tm=1408)
    if name in ROW_SHARDED:
        k = kdim // N_DEV
        tn = min(1024, ndim)
        place = ((2, N_CHIPS, depth * k, ndim), (None, None, k, tn),
                 lambda i, j, kk: (i % 2, i // 2, l, j), bufs.get(name))
        return _mm(act, dout, "tn", BF16, call_name, tm=k, tn=tn, place=place)
    n = ndim // N_DEV
    tm = min(1024, kdim)
    per_layer = kdim // tm
    place = ((2, N_CHIPS, depth * kdim, n), (None, None, tm, n),
             lambda i, j, kk: (j % 2, j // 2, l * per_layer + i, 0), bufs.get(name))
    return _mm(act, dout, "tn", BF16, call_name, tm=tm, tn=n, place=place)


PLACED_NAMES = ("s5_glu_w", "dn_proj_w", "w_out", "ffn_up")


def _layer_backward(l, dx2, wts, sm, sv, depth, bufs):
    rows = dx2.shape[0]
    nm = f"l{l}_b_"
    cw = 512
    gr = {}
    dxb = dx2.astype(BF16)
    gr["ffn_down"] = _wgrad("ffn_down", sv["hid"], dxb, l, depth, bufs, nm + "ffn_down_w")
    dhid = _mm(dxb, wts["ffn_down"], "nt", F32, nm + "ffn_down_x", tn=1408)
    dup_a, dup_v, dw_a, dw_v = _ffn_mix_bwd(sv["up"], wts["ffn_conv_w"], dhid, nm + "ffn_mix")
    gr["ffn_conv_w"] = jnp.concatenate([dw_a[:FFN_CONV], dw_v[:FFN_CONV]], axis=1)
    dupb = jnp.concatenate([dup_a, dup_v], axis=1)
    gr["ffn_up"] = _wgrad("ffn_up", sv["h2"], dupb, l, depth, bufs, nm + "ffn_up_w")
    dh2 = _mm(dupb, wts["ffn_up"], "nt", F32, nm + "ffn_up_x", tk=2816)
    ffw = sm["ffn_norm_w"][l][None]
    (dx1n,), (dffw,) = _tile_bwd(nm + "ffn_norm", _f_rms, [(sv["x1"], None, 0)], [(ffw, None, 0)],
                                 [(dh2, None, 0)], [F32], rows, 256)
    gr["ffn_norm_w"] = dffw[0]
    dx1 = dx2 + dx1n
    dx1b = dx1.astype(BF16)
    gr["w_out"] = _wgrad("w_out", sv["merged"], dx1b, l, depth, bufs, nm + "w_out_w")
    dmerged = _mm(dx1b, wts["w_out"], "nt", F32, nm + "w_out_x")
    (dga, dgb, dbr, dgs, dgd), _ = _tile_bwd(
        nm + "merge", _f_merge,
        [(sv["glu"], cw, 0), (sv["glu"], cw, D_MODEL // cw), (sv["br_dn"], cw, 0),
         (sv["proj"], cw, MAIN_GS // cw), (sv["proj"], cw, MAIN_GD // cw)], [], [(dmerged, cw, 0)],
        [BF16, BF16, BF16, BF16, BF16], rows, 512, ncol=D_MODEL // cw)
    dglu = jnp.concatenate([dga, dgb], axis=1)
    gr["s5_glu_w"] = _wgrad("s5_glu_w", sv["y_s5"], dglu, l, depth, bufs, nm + "glu_w")
    dy_s5 = _mm(dglu, wts["s5_glu_w"], "nt", F32, nm + "glu_x")
    gr["dn_proj_w"] = _wgrad("dn_proj_w", sv["y_dn"], dbr, l, depth, bufs, nm + "dn_proj_w")
    dy_dn = _mm(dbr, wts["dn_proj_w"], "nt", F32, nm + "dn_proj_x")
    s5p = sv["s5p"]
    du, gbr, gbi, gcr, gci, gar, gai, gd = _s5_bwd(
        sv["proj"], dy_s5, sv["st_re"], sv["st_im"], s5p["bd_re"], s5p["bd_im"], s5p["cd_re"], s5p["cd_im"],
        s5p["a_re"], s5p["a_im"], s5p["d"], nm + "s5_scan")
    gr["s5_d"] = gd[0]
    gr["s5_c_re"] = _block_diag_c_grad(gcr)
    gr["s5_c_im"] = _block_diag_c_grad(gci)
    bt_re, bt_im = sm["s5_b_re"][l].transpose(2, 0, 1), sm["s5_b_im"][l].transpose(2, 0, 1)
    dldt, dare, daim, dbtr, dbti = _s5_disc_bwd(
        sm["s5_log_dt"][l][:, None], sm["s5_a_re"][l], sm["s5_a_im"][l], bt_re, bt_im,
        [gar.reshape(S5_GROUPS, S5_STATE), gai.reshape(S5_GROUPS, S5_STATE),
         _block_diag_b_grad(gbr), _block_diag_b_grad(gbi)], nm + "s5_disc")
    gr.update(s5_log_dt=dldt[:, 0], s5_a_re=dare, s5_a_im=daim,
              s5_b_re=dbtr.transpose(1, 2, 0), s5_b_im=dbti.transpose(1, 2, 0))
    (do, dz), (ddnw,) = _tile_bwd(nm + "dn_post", _f_dn_post,
                                  [(sv["o"], LANES, 0), (sv["proj"], LANES, MAIN_Z // LANES)],
                                  [(sv["dnw"], None, 0)], [(dy_dn, LANES, 0)], [F32, BF16], rows, 512, ncol=8)
    gr["dn_norm_w"] = ddnw[0]
    dq, dk, dv, dgc, dbc = _dn_bwd(sv["qkv"], sv["gcol"], sv["bcol"], sv["states"], do, nm + "dn_chunk")
    dbg = jnp.pad(jnp.concatenate([dbc[:, :, 0].T, dgc[:, :, 0].T], axis=1), ((0, 0), (0, LANES - 2 * DN_HEADS)))
    (dba,), (dalog, ddtb) = _tile_bwd(nm + "dn_gates", _f_dn_gates, [(sv["ba"], None, 0)],
                                      [(sv["alog"], None, 0), (sv["dtb"], None, 0)], [(dbg, None, 0)],
                                      [BF16], rows, 512)
    gr["dn_a_log"] = dalog[0, DN_HEADS:2 * DN_HEADS]
    gr["dn_dt_bias"] = ddtb[0, DN_HEADS:2 * DN_HEADS]
    dqkv, ddnconv = _dn_pre_bwd(sv["proj"], wts["dn_conv_w"], dq, dk, dv, nm + "dn_pre")
    gr["dn_conv_w"] = ddnconv[:DN_CONV]
    dproj = jnp.concatenate([du.astype(BF16), dqkv, dz, dgs, dgd], axis=1)
    gmain = _mm(sv["h"], dproj, "tn", BF16, nm + "proj_w")
    gba = _mm(sv["h"], dba, "tn", BF16, nm + "proj_ba_w", tk=1024)
    gr["w_in"] = jnp.concatenate([gmain[:, :OFF_BA], gba[:, :2 * DN_HEADS], gmain[:, OFF_BA:]], axis=1)
    dh = _mm(dproj, wts["w_in_main"], "nt", F32, nm + "proj_x", tk=2304)
    dh = _mm(dba, wts["w_in_ba"], "nt", F32, nm + "proj_ba_x", res=dh)
    mixw = sm["mix_norm_w"][l][None]
    (dx0n,), (dmixw,) = _tile_bwd(nm + "mix_norm", _f_rms, [(sv["x0"], None, 0)], [(mixw, None, 0)],
                                  [(dh, None, 0)], [F32], rows, 256)
    gr["mix_norm_w"] = dmixw[0]
    return dx1 + dx0n, gr


def kernel(x, mix_norm_w, w_in, s5_log_dt, s5_a_re, s5_a_im, s5_b_re, s5_b_im, s5_c_re, s5_c_im, s5_d, s5_glu_w, dn_conv_w, dn_a_log, dn_dt_bias, dn_norm_w, dn_proj_w, w_out, ffn_norm_w, ffn_up, ffn_conv_w, ffn_down, final_norm_w, loss_target, m_mix_norm_w, m_w_in, m_s5_log_dt, m_s5_a_re, m_s5_a_im, m_s5_b_re, m_s5_b_im, m_s5_c_re, m_s5_c_im, m_s5_d, m_s5_glu_w, m_dn_conv_w, m_dn_a_log, m_dn_dt_bias, m_dn_norm_w, m_dn_proj_w, m_w_out, m_ffn_norm_w, m_ffn_up, m_ffn_conv_w, m_ffn_down, m_final_norm_w, v_mix_norm_w, v_w_in, v_s5_log_dt, v_s5_a_re, v_s5_a_im, v_s5_b_re, v_s5_b_im, v_s5_c_re, v_s5_c_im, v_s5_d, v_s5_glu_w, v_dn_conv_w, v_dn_a_log, v_dn_dt_bias, v_dn_norm_w, v_dn_proj_w, v_w_out, v_ffn_norm_w, v_ffn_up, v_ffn_conv_w, v_ffn_down, v_final_norm_w):
    w = dict(mix_norm_w=mix_norm_w, w_in=w_in, s5_log_dt=s5_log_dt, s5_a_re=s5_a_re, s5_a_im=s5_a_im, s5_b_re=s5_b_re, s5_b_im=s5_b_im, s5_c_re=s5_c_re, s5_c_im=s5_c_im, s5_d=s5_d, s5_glu_w=s5_glu_w, dn_conv_w=dn_conv_w, dn_a_log=dn_a_log, dn_dt_bias=dn_dt_bias, dn_norm_w=dn_norm_w, dn_proj_w=dn_proj_w, w_out=w_out, ffn_norm_w=ffn_norm_w, ffn_up=ffn_up, ffn_conv_w=ffn_conv_w, ffn_down=ffn_down, final_norm_w=final_norm_w)
    mo = dict(mix_norm_w=m_mix_norm_w, w_in=m_w_in, s5_log_dt=m_s5_log_dt, s5_a_re=m_s5_a_re, s5_a_im=m_s5_a_im, s5_b_re=m_s5_b_re, s5_b_im=m_s5_b_im, s5_c_re=m_s5_c_re, s5_c_im=m_s5_c_im, s5_d=m_s5_d, s5_glu_w=m_s5_glu_w, dn_conv_w=m_dn_conv_w, dn_a_log=m_dn_a_log, dn_dt_bias=m_dn_dt_bias, dn_norm_w=m_dn_norm_w, dn_proj_w=m_dn_proj_w, w_out=m_w_out, ffn_norm_w=m_ffn_norm_w, ffn_up=m_ffn_up, ffn_conv_w=m_ffn_conv_w, ffn_down=m_ffn_down, final_norm_w=m_final_norm_w)
    vo = dict(mix_norm_w=v_mix_norm_w, w_in=v_w_in, s5_log_dt=v_s5_log_dt, s5_a_re=v_s5_a_re, s5_a_im=v_s5_a_im, s5_b_re=v_s5_b_re, s5_b_im=v_s5_b_im, s5_c_re=v_s5_c_re, s5_c_im=v_s5_c_im, s5_d=v_s5_d, s5_glu_w=v_s5_glu_w, dn_conv_w=v_dn_conv_w, dn_a_log=v_dn_a_log, dn_dt_bias=v_dn_dt_bias, dn_norm_w=v_dn_norm_w, dn_proj_w=v_dn_proj_w, w_out=v_w_out, ffn_norm_w=v_ffn_norm_w, ffn_up=v_ffn_up, ffn_conv_w=v_ffn_conv_w, ffn_down=v_ffn_down, final_norm_w=v_final_norm_w)
    depth = w_in.shape[0]
    me = 4 * lax.axis_index("x") + 2 * lax.axis_index("y") + lax.axis_index("c")
    xs = x[0]
    target = loss_target[0]

    gather_names = BIG_NAMES + CONV_NAMES
    gathered = _all_gather([w[n].astype(BF16) if n in BIG_NAMES else w[n] for n in gather_names], "gather_weights")
    full = {}
    for n, g in zip(gather_names, gathered):
        if n in ROW_SHARDED:
            full[n] = g.transpose(1, 0, 2, 3).reshape(depth, -1, g.shape[-1])
        else:
            full[n] = g.transpose(1, 2, 0, 3).reshape(depth, g.shape[2], -1)
    layer_w = []
    for l in range(depth):
        wi = full["w_in"][l]
        lw = {n: full[n][l] for n in gather_names if n != "w_in"}
        lw["w_in_main"] = jnp.concatenate([wi[:, :OFF_BA], wi[:, OFF_GS:]], axis=1)
        lw["w_in_ba"] = jnp.pad(wi[:, OFF_BA:OFF_GS], ((0, 0), (0, LANES - 2 * DN_HEADS)))
        layer_w.append(lw)

    saved = []
    h = xs
    for l in range(depth):
        h, sv = _layer_forward(l, h, layer_w[l], w)
        saved.append(sv)
    dx, dfinal, loss_tile = _final_loss(h, final_norm_w[None], target, "final_loss")

    grads = [None] * depth
    bufs = {}
    for l in reversed(range(depth)):
        dx, grads[l] = _layer_backward(l, dx, layer_w[l], w, saved[l], depth, bufs)
        bufs = {n: grads[l][n] for n in PLACED_NAMES}

    def stacked(n):
        return jnp.stack([grads[l][n] for l in range(depth)])

    big_send = []
    for n in BIG_NAMES:
        if n in PLACED_NAMES:
            big_send.append(bufs[n])
            continue
        g = stacked(n)
        if n in ROW_SHARDED:
            g = g.reshape(depth, N_CHIPS, 2, g.shape[1] // N_DEV, g.shape[2]).transpose(2, 1, 0, 3, 4)
        else:
            g = g.reshape(depth, g.shape[1], N_CHIPS, 2, g.shape[2] // N_DEV).transpose(3, 2, 0, 1, 4)
        big_send.append(g.reshape(2, N_CHIPS, -1, g.shape[-1]))
    from_sibling = _sibling_swap(big_send, "swap_grads")
    chip_partials = [_pair_sum(own, got, "pair_sum_" + n) for n, own, got in zip(BIG_NAMES, big_send, from_sibling)]
    big_recv = _chip_scatter(chip_partials, "scatter_grads")
    small_list = [stacked(n) for n in SMALL_NAMES if n != "final_norm_w"] + [dfinal[0]]
    small_list += [stacked(n) for n in CONV_NAMES] + [loss_tile[0, 0:1]]
    small_shapes = [a.shape for a in small_list]
    (small_recv,) = _all_gather([_pack(small_list)], "gather_small")
    small_sum = _unpack(_sum_parts(small_recv, "sum_small"), small_shapes)
    loss = small_sum[-1][0]
    small_names = [n for n in SMALL_NAMES if n != "final_norm_w"] + ["final_norm_w"]
    g_out = dict(zip(small_names, small_sum[:len(small_names)]))
    for n, gfull in zip(CONV_NAMES, small_sum[len(small_names):len(small_names) + 2]):
        shard = w[n].shape[-1]
        g_out[n] = lax.dynamic_slice_in_dim(gfull, me * shard, shard, axis=2)

    d_out, m_out, v_out = {}, {}, {}
    for n, parts in zip(BIG_NAMES, big_recv):
        shp = w[n].shape
        two = lambda a: a.reshape(-1, shp[-1])
        g2, d2, m2, v2 = _reduce_adamw(parts, two(w[n]), two(mo[n]), two(vo[n]), "adamw_" + n)
        g_out[n], d_out[n], m_out[n], v_out[n] = (a.reshape(shp) for a in (g2, d2, m2, v2))
    rest = list(small_names) + list(CONV_NAMES)
    rest_shapes = [w[n].shape for n in rest]
    d2, m2, v2 = _adamw(_pack([g_out[n] for n in rest]), _pack([w[n] for n in rest]), _pack([mo[n] for n in rest]),
                        _pack([vo[n] for n in rest]), "adamw_small")
    for n, d, m_, v_ in zip(rest, _unpack(d2, rest_shapes), _unpack(m2, rest_shapes), _unpack(v2, rest_shapes)):
        d_out[n], m_out[n], v_out[n] = d, m_, v_

    return (loss, dx[None], *[g_out[n] for n in WEIGHT_ORDER], *[d_out[n] for n in WEIGHT_ORDER],
            *[m_out[n] for n in WEIGHT_ORDER], *[v_out[n] for n in WEIGHT_ORDER])
```

```python
import functools
import math

import jax
import jax.numpy as jnp
from jax import lax
from jax.experimental import pallas as pl
from jax.experimental.pallas import tpu as pltpu

F32 = jnp.float32
BF16 = jnp.bfloat16

D_MODEL = 2048
DEPTH = 4
S5_WIDTH = 1024
S5_GROUP = 16
S5_GROUPS = 64
S5_STATE = 64
DN_HEADS = 8
DN_DK = 128
DN_QKV = 3072
DN_CONV = 4
DN_CHUNK = 64
FFN_DIM = 5632
FFN_CONV = 3
NORM_EPS = 1e-6
N_IN = 9232
OFF_Z = 4096
OFF_BA = 5120
OFF_GS = 5136
N_MAIN = 9216
MAIN_Z = 4096
MAIN_GS = 5120
MAIN_GD = 7168

ADAM_LR = 0.001
ADAM_B1 = 0.9
ADAM_B2 = 0.999
ADAM_EPS = 1e-08
ADAM_WD = 0.01
ADAM_STEP = 10

N_DEV = 8
LANES = 128
SUBLANES = 8
VMEM_LIMIT_BYTES = 48 * 1024 * 1024

S5_SLABS = 8
S5_SLAB_STATE = 512
S5_TB = 256
DN_TB = 512
PACK_ROWS = 512


def _cparams(sem):
    return pltpu.CompilerParams(dimension_semantics=sem, vmem_limit_bytes=VMEM_LIMIT_BYTES)


def _dot(a, b, dims, precision=None):
    return lax.dot_general(a, b, (dims, ((), ())), precision=precision, preferred_element_type=F32)


_NN = ((1,), (0,))
_NT = ((1,), (1,))
_TN = ((0,), (0,))


def _mm(a, b, mode, out_dtype, name, res=None, tm=1024, tn=1024, tk=2048, place=None):
    if mode == "nn":
        (m, k), (_, n) = a.shape, b.shape
    elif mode == "nt":
        (m, k), (n, _) = a.shape, b.shape
    else:
        (k, m), (_, n) = a.shape, b.shape
    tm, tn, tk = min(tm, m), min(tn, n), min(tk, k)
    assert m % tm == 0 and n % tn == 0 and k % tk == 0, (name, a.shape, b.shape)
    nk = k // tk
    if mode == "tn":
        a_spec = pl.BlockSpec((tk, tm), lambda i, j, kk: (kk, i))
    else:
        a_spec = pl.BlockSpec((tm, tk), lambda i, j, kk: (i, kk))
    if mode == "nt":
        b_spec = pl.BlockSpec((tn, tk), lambda i, j, kk: (j, kk))
    else:
        b_spec = pl.BlockSpec((tk, tn), lambda i, j, kk: (kk, j))
    dims = {"nn": _NN, "nt": _NT, "tn": _TN}[mode]
    o_spec = pl.BlockSpec((tm, tn), lambda i, j, kk: (i, j))
    has_res = res is not None

    has_buf = place is not None and place[3] is not None

    def body(*refs):
        if has_buf:
            refs = refs[:-3] + refs[-2:]
        if has_res:
            a_ref, b_ref, r_ref, o_ref, acc = refs
        else:
            a_ref, b_ref, o_ref, acc = refs
        p = _dot(a_ref[...], b_ref[...], dims)

        def finish(total):
            if has_res:
                total = total + r_ref[...]
            o_ref[...] = total.astype(out_dtype)

        if nk == 1:
            finish(p)
        else:
            kk = pl.program_id(2)

            @pl.when(kk == 0)
            def _():
                acc[...] = p

            @pl.when(jnp.logical_and(kk > 0, kk < nk - 1))
            def _():
                acc[...] += p

            @pl.when(kk == nk - 1)
            def _():
                finish(acc[...] + p)

    in_specs = [a_spec, b_spec] + ([o_spec] if has_res else [])
    args = (a, b) + ((res,) if has_res else ())
    out_shape, out_spec, aliases = jax.ShapeDtypeStruct((m, n), out_dtype), o_spec, {}
    if place is not None:
        shape, block, index_map, buf = place
        out_shape, out_spec = jax.ShapeDtypeStruct(shape, out_dtype), pl.BlockSpec(block, index_map)
        if buf is not None:
            aliases = {len(args): 0}
            in_specs = in_specs + [pl.BlockSpec(memory_space=pl.ANY)]
            args = args + (buf,)
    return pl.pallas_call(
        body, name=name, grid=(m // tm, n // tn, nk), in_specs=in_specs, out_specs=out_spec,
        out_shape=out_shape, input_output_aliases=aliases,
        scratch_shapes=[pltpu.VMEM((tm, tn) if nk > 1 else (SUBLANES, LANES), F32)],
        compiler_params=_cparams(("parallel", "parallel", "arbitrary")),
    )(*args)


def _row_spec(tb, width, cw, off):
    if cw is None:
        return pl.BlockSpec((tb, width), lambda j, i: (i, 0))
    return pl.BlockSpec((tb, cw), lambda j, i: (i, j + off))


def _par_spec(rows, width, cw, off):
    if cw is None:
        return pl.BlockSpec((rows, width), lambda j, i: (0, 0))
    return pl.BlockSpec((rows, cw), lambda j, i: (0, j + off))


def _tile_fwd(name, fn, tiled, params, outs, rows, tb, ncol=1):
    tb = min(tb, rows)
    nt, npar = len(tiled), len(params)

    def body(*refs):
        vals = [r[...] for r in refs[:nt + npar]]
        res = fn(*vals)
        for o_ref, r in zip(refs[nt + npar:], res):
            o_ref[...] = r.astype(o_ref.dtype)

    in_specs = [_row_spec(tb, a.shape[1], cw, off) for a, cw, off in tiled]
    in_specs += [_par_spec(a.shape[0], a.shape[1], cw, off) for a, cw, off in params]
    out_specs = [_row_spec(tb, w, cw, 0) for w, cw, _ in outs]
    out_shape = [jax.ShapeDtypeStruct((rows, w), dt) for w, _, dt in outs]
    return pl.pallas_call(
        body, name=name, grid=(ncol, rows // tb), in_specs=in_specs, out_specs=out_specs, out_shape=out_shape,
        compiler_params=_cparams(("parallel", "parallel")),
    )(*[a for a, _, _ in tiled], *[a for a, _, _ in params])


def _tile_bwd(name, fn, tiled, params, cots, gdtypes, rows, tb, ncol=1, add_first=None):
    tb = min(tb, rows)
    nt, npar, nc = len(tiled), len(params), len(cots)
    want = [i for i, g in enumerate(gdtypes) if g is not None]
    nadd = 0 if add_first is None else 1

    def body(*refs):
        vals = [r[...] for r in refs[:nt + npar]]
        cot_refs = refs[nt + npar:nt + npar + nc]
        base = nt + npar + nc + nadd
        g_refs = refs[base:base + len(want)]
        p_refs = refs[base + len(want):]
        _, vjp = jax.vjp(fn, *vals)
        grads = list(vjp(tuple(c[...].astype(F32) for c in cot_refs)))
        if nadd:
            grads[0] = grads[0] + refs[base - 1][...]
        for g_ref, i in zip(g_refs, want):
            g_ref[...] = grads[i].astype(g_ref.dtype)
        jcol, irow = pl.program_id(0), pl.program_id(1)
        for p_ref, g, (_, cw, _) in zip(p_refs, grads[nt:], params):
            first = (irow == 0) if cw is not None else jnp.logical_and(irow == 0, jcol == 0)

            @pl.when(first)
            def _():
                p_ref[...] = g

            @pl.when(jnp.logical_not(first))
            def _():
                p_ref[...] += g

    in_specs = [_row_spec(tb, a.shape[1], cw, off) for a, cw, off in tiled]
    in_specs += [_par_spec(a.shape[0], a.shape[1], cw, off) for a, cw, off in params]
    in_specs += [_row_spec(tb, a.shape[1], cw, off) for a, cw, off in cots]
    extra = ()
    if nadd:
        in_specs.append(_row_spec(tb, tiled[0][0].shape[1], tiled[0][1], tiled[0][2]))
        extra = (add_first,)
    out_specs, out_shape = [], []
    for i in want:
        a, cw, _ = tiled[i]
        width = a.shape[1] if cw is None else ncol * cw
        out_specs.append(_row_spec(tb, width, cw, 0))
        out_shape.append(jax.ShapeDtypeStruct((rows, width), gdtypes[i]))
    for a, cw, _ in params:
        width = a.shape[1] if cw is None else ncol * cw
        out_specs.append(_par_spec(a.shape[0], width, cw, 0))
        out_shape.append(jax.ShapeDtypeStruct((a.shape[0], width), F32))
    res = pl.pallas_call(
        body, name=name, grid=(ncol, rows // tb), in_specs=in_specs, out_specs=out_specs, out_shape=out_shape,
        compiler_params=_cparams(("arbitrary", "arbitrary")),
    )(*[a for a, _, _ in tiled], *[a for a, _, _ in params], *[a for a, _, _ in cots], *extra)
    return res[:len(want)], res[len(want):]


def _sigmoid(x):
    return 1.0 / (1.0 + jnp.exp(-x))


def _silu(x):
    return x * _sigmoid(x)


def _softplus(x):
    return jnp.maximum(x, 0.0) + jnp.log1p(jnp.exp(-jnp.abs(x)))


def _f_rms(x, w):
    return (x * lax.rsqrt(jnp.mean(x * x, axis=-1, keepdims=True) + NORM_EPS) * w,)


def _f_merge(glu_a, glu_b, br_dn, gs, gd):
    return (_sigmoid(gs) * (glu_a * _sigmoid(glu_b)) + _sigmoid(gd) * br_dn,)


def _f_ffn_gate(act, val):
    return (_silu(act) * val,)


def _l2n(x):
    return x * lax.rsqrt(jnp.sum(x * x, axis=-1, keepdims=True) + NORM_EPS)


def _f_dn_q(c):
    return (_l2n(_silu(c)) * (DN_DK ** -0.5),)


def _f_dn_k(c):
    return (_l2n(_silu(c)),)


def _f_dn_v(c):
    return (_silu(c),)


def _f_dn_gates(ba, a_log, dt_bias):
    col = lax.broadcasted_iota(jnp.int32, ba.shape, 1)
    beta = _sigmoid(ba)
    g = -jnp.exp(a_log) * _softplus(ba + dt_bias)
    return (jnp.where(col < DN_HEADS, beta, jnp.where(col < 2 * DN_HEADS, g, 0.0)),)


def _f_dn_post(o, z, w):
    return (_f_rms(o, w)[0] * _silu(z),)


def _shift_down(x, halo, s, tb):
    if s == 0:
        return x
    y = pltpu.roll(x, s, 0)
    row8 = lax.broadcasted_iota(jnp.int32, halo.shape, 0)
    top = jnp.where(row8 < s, pltpu.roll(halo, s, 0), y[0:SUBLANES])
    if tb == SUBLANES:
        return top
    return jnp.concatenate([top, y[SUBLANES:]], axis=0)


def _shift_up(x, halo, s, tb):
    if s == 0:
        return x
    y = pltpu.roll(x, tb - s, 0)
    row8 = lax.broadcasted_iota(jnp.int32, halo.shape, 0)
    bot = jnp.where(row8 >= SUBLANES - s, pltpu.roll(halo, SUBLANES - s, 0), y[tb - SUBLANES:])
    if tb == SUBLANES:
        return bot
    return jnp.concatenate([y[:tb - SUBLANES], bot], axis=0)


def _conv_fwd(x, w, kw, name, x_off=0, width=None, cw=512, tb=512):
    rows = x.shape[0]
    width = w.shape[1] if width is None else width
    tb = min(tb, rows)
    nb = tb // SUBLANES

    def body(x_ref, h_ref, w_ref, o_ref):
        i = pl.program_id(1)
        xv = x_ref[...]
        halo = jnp.where(i > 0, h_ref[...], 0.0)
        acc = w_ref[kw - 1:kw, :] * xv
        for s in range(1, kw):
            acc = acc + w_ref[kw - 1 - s:kw - s, :] * _shift_down(xv, halo, s, tb)
        o_ref[...] = acc

    return pl.pallas_call(
        body, name=name, grid=(width // cw, rows // tb),
        in_specs=[pl.BlockSpec((tb, cw), lambda j, i: (i, j + x_off)),
                  pl.BlockSpec((SUBLANES, cw), lambda j, i: (jnp.maximum(i * nb - 1, 0), j + x_off)),
                  pl.BlockSpec((kw, cw), lambda j, i: (0, j))],
        out_specs=pl.BlockSpec((tb, cw), lambda j, i: (i, j)),
        out_shape=jax.ShapeDtypeStruct((rows, width), F32),
        compiler_params=_cparams(("parallel", "parallel")),
    )(x, x, w)


def _conv_bwd(x, w, dout, kw, name, x_off=0, cw=512, tb=512):
    rows, width = dout.shape
    tb = min(tb, rows)
    nb = tb // SUBLANES
    nrow = rows // tb

    def body(x_ref, h_ref, w_ref, d_ref, dn_ref, dx_ref, dw_ref):
        i = pl.program_id(1)
        xv, dv = x_ref[...], d_ref[...]
        halo = jnp.where(i > 0, h_ref[...], 0.0)
        nxt = jnp.where(i < nrow - 1, dn_ref[...], 0.0)

        @pl.when(i == 0)
        def _():
            dw_ref[...] = jnp.zeros_like(dw_ref)

        acc = w_ref[kw - 1:kw, :] * dv
        dw_ref[kw - 1:kw, :] += jnp.sum(dv * xv, axis=0, keepdims=True)
        for s in range(1, kw):
            acc = acc + w_ref[kw - 1 - s:kw - s, :] * _shift_up(dv, nxt, s, tb)
            dw_ref[kw - 1 - s:kw - s, :] += jnp.sum(dv * _shift_down(xv, halo, s, tb), axis=0, keepdims=True)
        dx_ref[...] = acc

    return pl.pallas_call(
        body, name=name, grid=(width // cw, nrow),
        in_specs=[pl.BlockSpec((tb, cw), lambda j, i: (i, j + x_off)),
                  pl.BlockSpec((SUBLANES, cw), lambda j, i: (jnp.maximum(i * nb - 1, 0), j + x_off)),
                  pl.BlockSpec((kw, cw), lambda j, i: (0, j)),
                  pl.BlockSpec((tb, cw), lambda j, i: (i, j)),
                  pl.BlockSpec((SUBLANES, cw), lambda j, i: (jnp.minimum((i + 1) * nb, rows // SUBLANES - 1), j))],
        out_specs=[pl.BlockSpec((tb, cw), lambda j, i: (i, j)),
                   pl.BlockSpec((SUBLANES, cw), lambda j, i: (0, j))],
        out_shape=[jax.ShapeDtypeStruct((rows, width), F32), jax.ShapeDtypeStruct((SUBLANES, width), F32)],
        compiler_params=_cparams(("parallel", "arbitrary")),
    )(x, x, w, dout, dout)


def _conv_taps(xv, halo, w_ref, kw, tb):
    acc = w_ref[kw - 1:kw, :] * xv
    for s in range(1, kw):
        acc = acc + w_ref[kw - 1 - s:kw - s, :] * _shift_down(xv, halo, s, tb)
    return acc


def _ffn_mix_specs(rows, f, kw, cw, tb):
    nj, nb, last8 = f // cw, tb // SUBLANES, rows // SUBLANES - 1
    blk = lambda off: pl.BlockSpec((tb, cw), lambda j, i: (i, j + off))
    prev = lambda off: pl.BlockSpec((SUBLANES, cw), lambda j, i: (jnp.maximum(i * nb - 1, 0), j + off))
    nxt = lambda off: pl.BlockSpec((SUBLANES, cw), lambda j, i: (jnp.minimum((i + 1) * nb, last8), j + off))
    wsp = lambda off: pl.BlockSpec((kw, cw), lambda j, i: (0, j + off))
    return nj, blk, prev, nxt, wsp


def _ffn_mix_fwd(up, w, name, cw=512, tb=512):
    rows, f, kw = up.shape[0], up.shape[1] // 2, w.shape[0]
    tb = min(tb, rows)
    nj, blk, prev, _, wsp = _ffn_mix_specs(rows, f, kw, cw, tb)

    def body(xa_ref, xv_ref, ha_ref, hv_ref, wa_ref, wv_ref, o_ref):
        first = pl.program_id(1) == 0
        ca = _conv_taps(xa_ref[...], jnp.where(first, 0.0, ha_ref[...]), wa_ref, kw, tb)
        cv = _conv_taps(xv_ref[...], jnp.where(first, 0.0, hv_ref[...]), wv_ref, kw, tb)
        o_ref[...] = _f_ffn_gate(ca, cv)[0].astype(BF16)

    return pl.pallas_call(
        body, name=name, grid=(nj, rows // tb),
        in_specs=[blk(0), blk(nj), prev(0), prev(nj), wsp(0), wsp(nj)],
        out_specs=blk(0), out_shape=jax.ShapeDtypeStruct((rows, f), BF16),
        compiler_params=_cparams(("parallel", "parallel")),
    )(up, up, up, up, w, w)


def _ffn_mix_bwd(up, w, dhid, name, cw=512, tb=512):
    rows, f, kw = up.shape[0], up.shape[1] // 2, w.shape[0]
    tb = min(tb, rows)
    te = tb + SUBLANES
    nrow = rows // tb
    nj, blk, prev, nxt, wsp = _ffn_mix_specs(rows, f, kw, cw, tb)

    def body(xa_ref, xv_ref, ha_ref, hv_ref, na_ref, nv_ref, wa_ref, wv_ref, d_ref, dn_ref,
             da_ref, dv_ref, dwa_ref, dwv_ref):
        i = pl.program_id(1)
        first, last = i == 0, i == nrow - 1

        @pl.when(first)
        def _():
            dwa_ref[...] = jnp.zeros_like(dwa_ref)
            dwv_ref[...] = jnp.zeros_like(dwv_ref)

        def conv_ext(x_ref, h_ref, n_ref, w_ref):
            xe = jnp.concatenate([x_ref[...], jnp.where(last, 0.0, n_ref[...])], axis=0)
            return _conv_taps(xe, jnp.where(first, 0.0, h_ref[...]), w_ref, kw, te)

        dh = jnp.concatenate([d_ref[...], jnp.where(last, 0.0, dn_ref[...])], axis=0)
        _, vjp = jax.vjp(_f_ffn_gate, conv_ext(xa_ref, ha_ref, na_ref, wa_ref), conv_ext(xv_ref, hv_ref, nv_ref, wv_ref))
        dca, dcv = vjp((dh,))

        def back(dc_ext, x_ref, h_ref, w_ref, dx_ref, dw_ref):
            dc = dc_ext[0:tb]
            xv = x_ref[...]
            halo = jnp.where(first, 0.0, h_ref[...])
            acc = w_ref[kw - 1:kw, :] * dc
            dw_ref[kw - 1:kw, :] += jnp.sum(dc * xv, axis=0, keepdims=True)
            for s in range(1, kw):
                acc = acc + w_ref[kw - 1 - s:kw - s, :] * pltpu.roll(dc_ext, te - s, 0)[0:tb]
                dw_ref[kw - 1 - s:kw - s, :] += jnp.sum(dc * _shift_down(xv, halo, s, tb), axis=0, keepdims=True)
            dx_ref[...] = acc.astype(dx_ref.dtype)

        back(dca, xa_ref, ha_ref, wa_ref, da_ref, dwa_ref)
        back(dcv, xv_ref, hv_ref, wv_ref, dv_ref, dwv_ref)

    wide = jax.ShapeDtypeStruct((rows, f), BF16)
    taps = jax.ShapeDtypeStruct((SUBLANES, f), F32)
    tap_spec = pl.BlockSpec((SUBLANES, cw), lambda j, i: (0, j))
    return pl.pallas_call(
        body, name=name, grid=(nj, nrow),
        in_specs=[blk(0), blk(nj), prev(0), prev(nj), nxt(0), nxt(nj), wsp(0), wsp(nj), blk(0), nxt(0)],
        out_specs=[blk(0), blk(0), tap_spec, tap_spec], out_shape=[wide, wide, taps, taps],
        compiler_params=_cparams(("parallel", "arbitrary")),
    )(up, up, up, up, up, up, w, w, dhid, dhid)


DN_PRE_TB = 1024


def _dn_pre_specs(rows, tb):
    nb, last8, off = tb // SUBLANES, rows // SUBLANES - 1, S5_WIDTH // LANES
    blk = pl.BlockSpec((tb, LANES), lambda j, i: (i, j + off))
    prev = pl.BlockSpec((SUBLANES, LANES), lambda j, i: (jnp.maximum(i * nb - 1, 0), j + off))
    nxt = pl.BlockSpec((SUBLANES, LANES), lambda j, i: (jnp.minimum((i + 1) * nb, last8), j + off))
    wsp = pl.BlockSpec((DN_CONV, LANES), lambda j, i: (0, j))
    return blk, prev, nxt, wsp


def _dn_head_kind(j, q_fn, k_fn, v_fn):
    pl.when(j < DN_HEADS)(q_fn)
    pl.when(jnp.logical_and(j >= DN_HEADS, j < 2 * DN_HEADS))(k_fn)
    pl.when(j >= 2 * DN_HEADS)(v_fn)


def _dn_pre_fwd(proj, w, name):
    rows = proj.shape[0]
    tb = min(DN_PRE_TB, rows)
    blk, prev, _, wsp = _dn_pre_specs(rows, tb)

    def body(x_ref, h_ref, w_ref, o_ref):
        first = pl.program_id(1) == 0
        c = _conv_taps(x_ref[...], jnp.where(first, 0.0, h_ref[...]), w_ref, DN_CONV, tb)

        def store(fn):
            def run():
                o_ref[...] = fn(c)[0]
            return run

        _dn_head_kind(pl.program_id(0), store(_f_dn_q), store(_f_dn_k), store(_f_dn_v))

    return pl.pallas_call(
        body, name=name, grid=(DN_QKV // LANES, rows // tb), in_specs=[blk, prev, wsp],
        out_specs=pl.BlockSpec((tb, LANES), lambda j, i: (i, j)),
        out_shape=jax.ShapeDtypeStruct((rows, DN_QKV), F32),
        compiler_params=_cparams(("parallel", "parallel")),
    )(proj, proj, w)


def _dn_pre_bwd(proj, w, dq, dk, dv, name):
    rows = proj.shape[0]
    tb = min(DN_PRE_TB, rows)
    te = tb + SUBLANES
    nrow = rows // tb
    kw = DN_CONV
    blk, prev, nxt, wsp = _dn_pre_specs(rows, tb)
    nb, last8 = tb // SUBLANES, rows // SUBLANES - 1

    def cot_specs(part):
        col = lambda j: jnp.clip(j - part * DN_HEADS, 0, DN_HEADS - 1)
        return (pl.BlockSpec((tb, LANES), lambda j, i: (i, col(j))),
                pl.BlockSpec((SUBLANES, LANES), lambda j, i: (jnp.minimum((i + 1) * nb, last8), col(j))))

    def body(x_ref, h_ref, n_ref, w_ref, dq_ref, dqn_ref, dk_ref, dkn_ref, dv_ref, dvn_ref, dx_ref, dw_ref):
        i = pl.program_id(1)
        first, last = i == 0, i == nrow - 1

        @pl.when(first)
        def _():
            dw_ref[...] = jnp.zeros_like(dw_ref)

        xv = x_ref[...]
        halo = jnp.where(first, 0.0, h_ref[...])
        c_ext = _conv_taps(jnp.concatenate([xv, jnp.where(last, 0.0, n_ref[...])], axis=0), halo, w_ref, kw, te)

        def back(fn, d_ref, dn_ref):
            def run():
                ct = jnp.concatenate([d_ref[...], jnp.where(last, 0.0, dn_ref[...])], axis=0)
                _, vjp = jax.vjp(fn, c_ext)
                (dc_ext,) = vjp((ct,))
                dc = dc_ext[0:tb]
                acc = w_ref[kw - 1:kw, :] * dc
                dw_ref[kw - 1:kw, :] += jnp.sum(dc * xv, axis=0, keepdims=True)
                for s in range(1, kw):
                    acc = acc + w_ref[kw - 1 - s:kw - s, :] * pltpu.roll(dc_ext, te - s, 0)[0:tb]
                    dw_ref[kw - 1 - s:kw - s, :] += jnp.sum(dc * _shift_down(xv, halo, s, tb), axis=0, keepdims=True)
                dx_ref[...] = acc.astype(dx_ref.dtype)
            return run

        _dn_head_kind(pl.program_id(0), back(_f_dn_q, dq_ref, dqn_ref), back(_f_dn_k, dk_ref, dkn_ref),
                      back(_f_dn_v, dv_ref, dvn_ref))

    return pl.pallas_call(
        body, name=name, grid=(DN_QKV // LANES, nrow),
        in_specs=[blk, prev, nxt, wsp, *cot_specs(0), *cot_specs(1), *cot_specs(2)],
        out_specs=[pl.BlockSpec((tb, LANES), lambda j, i: (i, j)),
                   pl.BlockSpec((SUBLANES, LANES), lambda j, i: (0, j))],
        out_shape=[jax.ShapeDtypeStruct((rows, DN_QKV), BF16), jax.ShapeDtypeStruct((SUBLANES, DN_QKV), F32)],
        compiler_params=_cparams(("parallel", "arbitrary")),
    )(proj, proj, proj, w, dq, dq, dk, dk, dv, dv)


def _f_s5_disc(log_dt, a_re, a_im, bt_re, bt_im):
    dt = jnp.exp(log_dt)
    mag = jnp.exp(a_re * dt)
    abar_re, abar_im = mag * jnp.cos(a_im * dt), mag * jnp.sin(a_im * dt)
    den = a_re * a_re + a_im * a_im
    nr, ni = abar_re - 1.0, abar_im
    coef_re = (nr * a_re + ni * a_im) / den
    coef_im = (ni * a_re - nr * a_im) / den
    bbar_re = coef_re[None] * bt_re - coef_im[None] * bt_im
    bbar_im = coef_re[None] * bt_im + coef_im[None] * bt_re
    return abar_re, abar_im, bbar_re, bbar_im


def _s5_disc_fwd(log_dt, a_re, a_im, bt_re, bt_im, name):
    def body(*refs):
        res = _f_s5_disc(*[r[...] for r in refs[:5]])
        for o_ref, r in zip(refs[5:], res):
            o_ref[...] = r

    shp = [a_re, a_re, bt_re, bt_re]
    return pl.pallas_call(body, name=name, out_shape=[jax.ShapeDtypeStruct(s.shape, F32) for s in shp])(
        log_dt, a_re, a_im, bt_re, bt_im)


def _s5_disc_bwd(log_dt, a_re, a_im, bt_re, bt_im, cots, name):
    def body(*refs):
        _, vjp = jax.vjp(_f_s5_disc, *[r[...] for r in refs[:5]])
        grads = vjp(tuple(r[...] for r in refs[5:9]))
        for o_ref, g in zip(refs[9:], grads):
            o_ref[...] = g

    ins = [log_dt, a_re, a_im, bt_re, bt_im]
    return pl.pallas_call(body, name=name, out_shape=[jax.ShapeDtypeStruct(s.shape, F32) for s in ins])(*ins, *cots)


def _cmul(ar, ai, br, bi):
    return ar * br - ai * bi, ar * bi + ai * br


def _seg_scan(xr_ref, xi_ref, ar, ai, cr, ci, pr_ref, pi_ref, tb, reverse):
    sl = tb // SUBLANES
    pitch = _seg_pitch(sl)
    groups = xr_ref.shape[0]
    lane = lambda a, k: a[:, k * LANES:(k + 1) * LANES]
    outs_r, outs_i = [], []
    a_k = [(lane(ar, k), lane(ai, k)) for k in range(groups)]

    def step(i, carry):
        t = sl - 1 - i if reverse else i
        rows = pl.ds(t, SUBLANES, stride=pitch)
        nxt = []
        for k, (xr, xi, pr, pi) in enumerate(carry):
            akr, aki = a_k[k]
            mr, mi = _cmul(akr, aki, xr, xi)
            xr, xi = mr + xr_ref[k, rows, :], mi + xi_ref[k, rows, :]
            xr_ref[k, rows, :] = xr
            xi_ref[k, rows, :] = xi
            pr_ref[k, pl.ds(t, 1), :] = pr
            pi_ref[k, pl.ds(t, 1), :] = pi
            nr, ni = _cmul(akr, aki, pr, pi)
            nxt.append((xr, xi, nr, ni))
        return tuple(nxt)

    zero = jnp.zeros((SUBLANES, LANES), F32)
    lax.fori_loop(0, sl, step, tuple((zero, zero, akr, aki) for akr, aki in a_k))
    last = 0 if reverse else sl - 1
    for k in range(groups):
        qr, qi = pr_ref[k, last:last + 1, :], pi_ref[k, last:last + 1, :]
        tr, ti = pr_ref[k], pi_ref[k]
        ckr, cki = lane(cr, k), lane(ci, k)
        order = range(SUBLANES - 1, -1, -1) if reverse else range(SUBLANES)
        for j in order:
            rows = slice(j * pitch, j * pitch + sl)
            edge = j * pitch if reverse else j * pitch + sl - 1
            er, ei = xr_ref[k, edge:edge + 1, :], xi_ref[k, edge:edge + 1, :]
            mr, mi = _cmul(tr, ti, ckr, cki)
            xr_ref[k, rows, :] += mr
            xi_ref[k, rows, :] += mi
            mr, mi = _cmul(qr, qi, ckr, cki)
            ckr, cki = er + mr, ei + mi
        outs_r.append(ckr)
        outs_i.append(cki)
    return jnp.concatenate(outs_r, axis=1), jnp.concatenate(outs_i, axis=1)


def _seg_pitch(sl):
    return sl + SUBLANES


def _to_groups(ref, val):
    sl = val.shape[0] // SUBLANES
    pitch = _seg_pitch(sl)
    for k in range(ref.shape[0]):
        for j in range(SUBLANES):
            ref[k, j * pitch:j * pitch + sl, :] = val[j * sl:(j + 1) * sl, k * LANES:(k + 1) * LANES]


def _from_groups(ref):
    pitch = ref.shape[1] // SUBLANES
    sl = pitch - SUBLANES
    return jnp.concatenate(
        [jnp.concatenate([ref[k, j * pitch:j * pitch + sl, :] for j in range(SUBLANES)], axis=0)
         for k in range(ref.shape[0])], axis=1)


_INV_SQRT2 = 1.0 / math.sqrt(2.0)
_INV_SQRT2PI = 1.0 / math.sqrt(2.0 * math.pi)


def _gelu(y):
    return 0.5 * y * (1.0 + lax.erf(y * _INV_SQRT2))


def _gelu_grad(y):
    return 0.5 * (1.0 + lax.erf(y * _INV_SQRT2)) + y * jnp.exp(-0.5 * y * y) * _INV_SQRT2PI


def _s5_states(u, bd_re, bd_im, ar, ai, cr, ci, xr_ref, xi_ref, pr_ref, pi_ref, tb):
    ub = u.astype(BF16)
    _to_groups(xr_ref, _dot(ub, bd_re, _NN))
    _to_groups(xi_ref, _dot(ub, bd_im, _NN))
    return _seg_scan(xr_ref, xi_ref, ar, ai, cr, ci, pr_ref, pi_ref, tb, reverse=False)


def _s5_scratch(tb, nbuf):
    groups = S5_SLAB_STATE // LANES
    sl = tb // SUBLANES
    return ([pltpu.VMEM((1, S5_SLAB_STATE), F32)] * 2
            + [pltpu.VMEM((groups, SUBLANES * _seg_pitch(sl), LANES), F32)] * nbuf
            + [pltpu.VMEM((groups, sl, LANES), F32)] * 2)


def _s5_specs(tb, u_off):
    slab3 = lambda r, c: pl.BlockSpec((None, r, c), lambda s, t: (s, 0, 0))
    return dict(
        u=lambda tmap: pl.BlockSpec((tb, LANES), lambda s, t: (tmap(t), s + u_off)),
        bd=slab3(LANES, S5_SLAB_STATE), cd=slab3(S5_SLAB_STATE, LANES), a=slab3(1, S5_SLAB_STATE),
        d=pl.BlockSpec((1, LANES), lambda s, t: (0, s)))


def _s5_fwd(proj, bd_re, bd_im, cd_re, cd_im, a_re, a_im, d, name):
    rows = proj.shape[0]
    tb = min(S5_TB, rows)
    nt = rows // tb
    sp = _s5_specs(tb, 0)

    def body(u_ref, bdr, bdi, cdr, cdi, ar_ref, ai_ref, d_ref, y_ref, sr_ref, si_ref,
             cr_s, ci_s, xr_s, xi_s, pr_s, pi_s):
        t = pl.program_id(1)

        @pl.when(t == 0)
        def _():
            cr_s[...] = jnp.zeros_like(cr_s)
            ci_s[...] = jnp.zeros_like(ci_s)

        cr, ci = cr_s[...], ci_s[...]
        sr_ref[...] = cr
        si_ref[...] = ci
        u = u_ref[...]
        cr, ci = _s5_states(u, bdr[...], bdi[...], ar_ref[...], ai_ref[...], cr, ci, xr_s, xi_s, pr_s, pi_s, tb)
        cr_s[...] = cr
        ci_s[...] = ci
        y = (_dot(_from_groups(xr_s).astype(BF16), cdr[...], _NN)
             - _dot(_from_groups(xi_s).astype(BF16), cdi[...], _NN) + d_ref[...] * u)
        y_ref[...] = _gelu(y).astype(BF16)

    st_spec = pl.BlockSpec((None, None, 1, S5_SLAB_STATE), lambda s, t: (s, t, 0, 0))
    st_shape = jax.ShapeDtypeStruct((S5_SLABS, nt, 1, S5_SLAB_STATE), F32)
    return pl.pallas_call(
        body, name=name, grid=(S5_SLABS, nt),
        in_specs=[sp["u"](lambda t: t), sp["bd"], sp["bd"], sp["cd"], sp["cd"], sp["a"], sp["a"], sp["d"]],
        out_specs=[pl.BlockSpec((tb, LANES), lambda s, t: (t, s)), st_spec, st_spec],
        out_shape=[jax.ShapeDtypeStruct((rows, S5_WIDTH), BF16), st_shape, st_shape],
        scratch_shapes=_s5_scratch(tb, 2),
        compiler_params=_cparams(("parallel", "arbitrary")),
    )(proj, bd_re, bd_im, cd_re, cd_im, a_re, a_im, d)


def _s5_bwd(proj, dy, st_re, st_im, bd_re, bd_im, cd_re, cd_im, a_re, a_im, d, name):
    rows = proj.shape[0]
    tb = min(S5_TB, rows)
    nt = rows // tb
    sp = _s5_specs(tb, 0)
    rev = lambda t: nt - 1 - t

    def body(u_ref, dy_ref, sr_ref, si_ref, bdr, bdi, cdr, cdi, ar_ref, ai_ref, d_ref,
             du_ref, gbr, gbi, gcr, gci, gar, gai, gd_ref, lr_s, li_s, tr_s, ti_s, xr_s, xi_s, pr_s, pi_s):
        t = pl.program_id(1)

        @pl.when(t == 0)
        def _():
            lr_s[...] = jnp.zeros_like(lr_s)
            li_s[...] = jnp.zeros_like(li_s)
            for r in (gbr, gbi, gcr, gci, gar, gai, gd_ref):
                r[...] = jnp.zeros_like(r)

        u = u_ref[...]
        ar, ai = ar_ref[...], ai_ref[...]
        cr, ci = sr_ref[...], si_ref[...]
        _s5_states(u, bdr[...], bdi[...], ar, ai, cr, ci, xr_s, xi_s, pr_s, pi_s, tb)
        xr, xi = _from_groups(xr_s), _from_groups(xi_s)
        xrb, xib = xr.astype(BF16), xi.astype(BF16)
        ypre = _dot(xrb, cdr[...], _NN) - _dot(xib, cdi[...], _NN) + d_ref[...] * u
        dyp = dy_ref[...] * _gelu_grad(ypre)
        dypb = dyp.astype(BF16)
        gd_ref[...] += jnp.sum(dyp * u, axis=0, keepdims=True)
        gcr[...] += _dot(xrb, dypb, _TN)
        gci[...] -= _dot(xib, dypb, _TN)
        _to_groups(tr_s, _dot(dypb, cdr[...], _NT))
        _to_groups(ti_s, -_dot(dypb, cdi[...], _NT))
        nr, ni = _seg_scan(tr_s, ti_s, ar, -ai, lr_s[...], li_s[...], pr_s, pi_s, tb, reverse=True)
        lr_s[...] = nr
        li_s[...] = ni
        lr, li = _from_groups(tr_s), _from_groups(ti_s)
        row = lax.broadcasted_iota(jnp.int32, (tb, 1), 0)
        lrb, lib = lr.astype(BF16), li.astype(BF16)
        du_ref[...] = _dot(lrb, bdr[...], _NT) + _dot(lib, bdi[...], _NT) + d_ref[...] * dyp
        ub = u.astype(BF16)
        gbr[...] += _dot(ub, lrb, _TN)
        gbi[...] += _dot(ub, lib, _TN)
        xpr = jnp.where(row == 0, cr, pltpu.roll(xr, 1, 0))
        xpi = jnp.where(row == 0, ci, pltpu.roll(xi, 1, 0))
        gar[...] += jnp.sum(lr * xpr + li * xpi, axis=0, keepdims=True)
        gai[...] += jnp.sum(li * xpr - lr * xpi, axis=0, keepdims=True)

    st_spec = pl.BlockSpec((None, None, 1, S5_SLAB_STATE), lambda s, t: (s, rev(t), 0, 0))
    slab = lambda r, c: pl.BlockSpec((None, r, c), lambda s, t: (s, 0, 0))
    return pl.pallas_call(
        body, name=name, grid=(S5_SLABS, nt),
        in_specs=[sp["u"](rev), pl.BlockSpec((tb, LANES), lambda s, t: (rev(t), s)), st_spec, st_spec,
                  sp["bd"], sp["bd"], sp["cd"], sp["cd"], sp["a"], sp["a"], sp["d"]],
        out_specs=[pl.BlockSpec((tb, LANES), lambda s, t: (rev(t), s)),
                   slab(LANES, S5_SLAB_STATE), slab(LANES, S5_SLAB_STATE),
                   slab(S5_SLAB_STATE, LANES), slab(S5_SLAB_STATE, LANES),
                   slab(1, S5_SLAB_STATE), slab(1, S5_SLAB_STATE),
                   pl.BlockSpec((1, LANES), lambda s, t: (0, s))],
        out_shape=[jax.ShapeDtypeStruct((rows, S5_WIDTH), F32),
                   jax.ShapeDtypeStruct((S5_SLABS, LANES, S5_SLAB_STATE), F32),
                   jax.ShapeDtypeStruct((S5_SLABS, LANES, S5_SLAB_STATE), F32),
                   jax.ShapeDtypeStruct((S5_SLABS, S5_SLAB_STATE, LANES), F32),
                   jax.ShapeDtypeStruct((S5_SLABS, S5_SLAB_STATE, LANES), F32),
                   jax.ShapeDtypeStruct((S5_SLABS, 1, S5_SLAB_STATE), F32),
                   jax.ShapeDtypeStruct((S5_SLABS, 1, S5_SLAB_STATE), F32),
                   jax.ShapeDtypeStruct((1, S5_WIDTH), F32)],
        scratch_shapes=_s5_scratch(tb, 4),
        compiler_params=_cparams(("parallel", "arbitrary")),
    )(proj, dy, st_re, st_im, bd_re, bd_im, cd_re, cd_im, a_re, a_im, d)


@functools.partial(jax.custom_vjp, nondiff_argnums=(2,))
def _bdot(a, b, dims):
    return _dot(a.astype(BF16), b.astype(BF16), dims)


def _bdot_fwd(a, b, dims):
    return _bdot(a, b, dims), (a, b)


def _bdot_bwd(dims, res, ct):
    a, b = res
    if dims == _NN:
        return _bdot(ct, b, _NT), _bdot(a, ct, _TN)
    if dims == _NT:
        return _bdot(ct, b, _NN), _bdot(ct, a, _TN)
    return _bdot(b, ct, _NT), _bdot(a, ct, _NN)


_bdot.defvjp(_bdot_fwd, _bdot_bwd)


def _split_bf16(a):
    hi = a.astype(BF16)
    return hi, (a - hi.astype(F32)).astype(BF16)


@functools.partial(jax.custom_vjp, nondiff_argnums=(2,))
def _dot3(a, b, dims):
    ah, al = _split_bf16(a)
    bh, bl = _split_bf16(b)
    return _dot(ah, bh, dims) + (_dot(ah, bl, dims) + _dot(al, bh, dims))


def _dot3_fwd(a, b, dims):
    return _dot3(a, b, dims), (a, b)


def _dot3_bwd(dims, res, ct):
    a, b = res
    if dims == _NN:
        return _dot3(ct, b, _NT), _dot3(a, ct, _TN)
    if dims == _NT:
        return _dot3(ct, b, _NN), _dot3(ct, a, _TN)
    return _dot3(b, ct, _NT), _dot3(a, ct, _NN)


_dot3.defvjp(_dot3_fwd, _dot3_bwd)


def _tril_ones(c):
    r = lax.broadcasted_iota(jnp.int32, (c, c), 0)
    col = lax.broadcasted_iota(jnp.int32, (c, c), 1)
    return jnp.where(r >= col, 1.0, 0.0).astype(BF16)


@jax.custom_vjp
def _chunk_cumsum(x):
    xh, xl = _split_bf16(x)
    t = _tril_ones(x.shape[0])
    return _dot(t, xh, _NN) + _dot(t, xl, _NN)


def _chunk_cumsum_fwd(x):
    return _chunk_cumsum(x), None


def _chunk_cumsum_bwd(_, ct):
    ch, cl = _split_bf16(ct)
    t = _tril_ones(ct.shape[0])
    return (_dot(t, ch, _TN) + _dot(t, cl, _TN),)


_chunk_cumsum.defvjp(_chunk_cumsum_fwd, _chunk_cumsum_bwd)


def _unit_lower_inverses(lms, n):
    r = lax.broadcasted_iota(jnp.int32, (n, n), 0)
    c = lax.broadcasted_iota(jnp.int32, (n, n), 1)
    eye = jnp.where(r == c, 1.0, 0.0)
    ps = [eye - lm for lm in lms]
    powers = list(lms)
    steps = int(math.log2(n)) - 1
    for _ in range(steps):
        powers = [_dot3(x, x, _NN) for x in powers]
        ps = [p + _dot3(p, x, _NN) for p, x in zip(ps, powers)]
    return ps


def _dn_chunk(qs, ks, vs, gs, bs, ss):
    c = qs[0].shape[0]
    r = lax.broadcasted_iota(jnp.int32, (c, c), 0)
    col = lax.broadcasted_iota(jnp.int32, (c, c), 1)
    tril = r >= col
    strict = r > col
    gls = [jnp.broadcast_to(g, (c, LANES)) for g in gs]
    gcs = [_chunk_cumsum(gl) for gl in gls]
    gtots = [jnp.sum(gl, axis=0, keepdims=True) for gl in gls]
    gdiffs = [_chunk_cumsum(jnp.where(strict, jnp.broadcast_to(g, (c, c)), 0.0)) for g in gs]
    decays = [jnp.where(tril, jnp.exp(jnp.where(tril, gd, 0.0)), 0.0) for gd in gdiffs]
    kbs = [k * b for k, b in zip(ks, bs)]
    vbs = [v * b for v, b in zip(vs, bs)]
    lmats = [jnp.where(strict, _bdot(kb, k, _NT) * d, 0.0) for kb, k, d in zip(kbs, ks, decays)]
    attns = [jnp.where(tril, _bdot(q, k, _NT) * d, 0.0) for q, k, d in zip(qs, ks, decays)]
    tinvs = _unit_lower_inverses(lmats, c)
    us = [_dot3(t, vb, _NN) for t, vb in zip(tinvs, vbs)]
    ws = [_dot3(t, kb * jnp.exp(gc), _NN) for t, kb, gc in zip(tinvs, kbs, gcs)]
    ws_s = [_bdot(w, s, _NN) for w, s in zip(ws, ss)]
    qs_s = [_bdot(q * jnp.exp(gc), s, _NN) for q, gc, s in zip(qs, gcs, ss)]
    v_news = [u - x for u, x in zip(us, ws_s)]
    os_ = [x + _bdot(a, vn, _NN) for x, a, vn in zip(qs_s, attns, v_news)]
    s_news = [s * jnp.exp(gt) + _bdot(k * jnp.exp(gt - gc), vn, _TN)
              for s, gt, k, gc, vn in zip(ss, gtots, ks, gcs, v_news)]
    return tuple(os_), tuple(s_news)


def _dn_specs(tb, hb, tmap):
    groups = DN_HEADS // hb
    blk = lambda part: pl.BlockSpec((tb, hb * LANES), lambda hg, t: (tmap(t), hg + part * groups))
    colv = pl.BlockSpec((hb, tb, 1), lambda hg, t: (hg, tmap(t), 0))
    st = pl.BlockSpec((hb, tb // DN_CHUNK, DN_DK, DN_DK), lambda hg, t: (hg, tmap(t), 0, 0))
    return blk, colv, st


def _dn_fwd(qkv, gcol, bcol, name, hb=DN_HEADS):
    rows = qkv.shape[0]
    tb = min(DN_TB, rows)
    nt = rows // tb
    nch = tb // DN_CHUNK
    blk, colv, st = _dn_specs(tb, hb, lambda t: t)

    def body(q_ref, k_ref, v_ref, g_ref, b_ref, o_ref, st_ref, s_scr):
        @pl.when(pl.program_id(1) == 0)
        def _():
            s_scr[...] = jnp.zeros_like(s_scr)

        def chunk(ci, carry):
            rs = pl.ds(pl.multiple_of(ci * DN_CHUNK, DN_CHUNK), DN_CHUNK)
            cols = [slice(j * LANES, (j + 1) * LANES) for j in range(hb)]
            s_in = tuple(s_scr[j] for j in range(hb))
            for j in range(hb):
                st_ref[j, ci] = s_in[j]
            os_, s_new = _dn_chunk(tuple(q_ref[rs, cs] for cs in cols), tuple(k_ref[rs, cs] for cs in cols),
                                   tuple(v_ref[rs, cs] for cs in cols), tuple(g_ref[j, rs, :] for j in range(hb)),
                                   tuple(b_ref[j, rs, :] for j in range(hb)), s_in)
            for j in range(hb):
                o_ref[rs, cols[j]] = os_[j]
                s_scr[j] = s_new[j]
            return carry

        lax.fori_loop(0, nch, chunk, 0)

    return pl.pallas_call(
        body, name=name, grid=(DN_HEADS // hb, nt),
        in_specs=[blk(0), blk(1), blk(2), colv, colv],
        out_specs=[blk(0), st],
        out_shape=[jax.ShapeDtypeStruct((rows, DN_HEADS * DN_DK), F32),
                   jax.ShapeDtypeStruct((DN_HEADS, rows // DN_CHUNK, DN_DK, DN_DK), F32)],
        scratch_shapes=[pltpu.VMEM((hb, DN_DK, DN_DK), F32)],
        compiler_params=_cparams(("parallel", "arbitrary")),
    )(qkv, qkv, qkv, gcol, bcol)


def _dn_bwd(qkv, gcol, bcol, states, do, name, hb=DN_HEADS // 2):
    rows = qkv.shape[0]
    tb = min(DN_TB, rows)
    nt = rows // tb
    nch = tb // DN_CHUNK
    blk, colv, st = _dn_specs(tb, hb, lambda t: nt - 1 - t)

    def body(q_ref, k_ref, v_ref, g_ref, b_ref, st_ref, do_ref, dq_ref, dk_ref, dv_ref, dg_ref, db_ref, ds_scr):
        @pl.when(pl.program_id(1) == 0)
        def _():
            ds_scr[...] = jnp.zeros_like(ds_scr)

        def chunk(cj, carry):
            ci = nch - 1 - cj
            rs = pl.ds(pl.multiple_of(ci * DN_CHUNK, DN_CHUNK), DN_CHUNK)
            cols = [slice(j * LANES, (j + 1) * LANES) for j in range(hb)]
            heads = range(hb)
            args = (tuple(q_ref[rs, cs] for cs in cols), tuple(k_ref[rs, cs] for cs in cols),
                    tuple(v_ref[rs, cs] for cs in cols), tuple(g_ref[j, rs, :] for j in heads),
                    tuple(b_ref[j, rs, :] for j in heads), tuple(st_ref[j, ci] for j in heads))
            _, vjp = jax.vjp(_dn_chunk, *args)
            dq, dk, dv, dg, db, ds = vjp((tuple(do_ref[rs, cs] for cs in cols), tuple(ds_scr[j] for j in heads)))
            for j in heads:
                dq_ref[rs, cols[j]] = dq[j]
                dk_ref[rs, cols[j]] = dk[j]
                dv_ref[rs, cols[j]] = dv[j]
                dg_ref[j, rs, :] = dg[j]
                db_ref[j, rs, :] = db[j]
                ds_scr[j] = ds[j]
            return carry

        lax.fori_loop(0, nch, chunk, 0)

    wide = jax.ShapeDtypeStruct((rows, DN_HEADS * DN_DK), F32)
    narrow = jax.ShapeDtypeStruct((DN_HEADS, rows, 1), F32)
    return pl.pallas_call(
        body, name=name, grid=(DN_HEADS // hb, nt),
        in_specs=[blk(0), blk(1), blk(2), colv, colv, st, blk(0)],
        out_specs=[blk(0), blk(0), blk(0), colv, colv],
        out_shape=[wide, wide, wide, narrow, narrow],
        scratch_shapes=[pltpu.VMEM((hb, DN_DK, DN_DK), F32)],
        compiler_params=_cparams(("parallel", "arbitrary")),
    )(qkv, qkv, qkv, gcol, bcol, states, do)


def _final_loss(x, w, target, name, tb=256):
    rows, width = x.shape
    tb = min(tb, rows)

    def body(x_ref, w_ref, t_ref, dx_ref, dw_ref, loss_ref):
        i = pl.program_id(0)
        (y,), vjp = jax.vjp(_f_rms, x_ref[...], w_ref[...])
        err = y - t_ref[...]
        part = 0.5 * jnp.sum(jnp.mean(err * err, axis=-1, keepdims=True), axis=0, keepdims=True)
        dx, dw = vjp((err * (1.0 / width),))
        dx_ref[...] = dx

        @pl.when(i == 0)
        def _():
            dw_ref[...] = dw
            loss_ref[...] = jnp.broadcast_to(part, loss_ref.shape)

        @pl.when(i > 0)
        def _():
            dw_ref[...] += dw
            loss_ref[...] += jnp.broadcast_to(part, loss_ref.shape)

    row = pl.BlockSpec((tb, width), lambda i: (i, 0))
    par = pl.BlockSpec((1, width), lambda i: (0, 0))
    return pl.pallas_call(
        body, name=name, grid=(rows // tb,), in_specs=[row, par, row],
        out_specs=[row, par, pl.BlockSpec((SUBLANES, LANES), lambda i: (0, 0))],
        out_shape=[jax.ShapeDtypeStruct((rows, width), F32), jax.ShapeDtypeStruct((1, width), F32),
                   jax.ShapeDtypeStruct((SUBLANES, LANES), F32)],
        compiler_params=_cparams(("arbitrary",)),
    )(x, w, target)


_ANY = pl.BlockSpec(memory_space=pl.ANY)
N_CHIPS = 4


def _mesh_place():
    x, y, c = lax.axis_index("x"), lax.axis_index("y"), lax.axis_index("c")
    other_chips = [(1 - x, y), (x, 1 - y), (1 - x, 1 - y)]
    return x, y, c, other_chips


def _remote(src, dst, send_sem, recv_sem, dev):
    return pltpu.make_async_remote_copy(src_ref=src, dst_ref=dst, send_sem=send_sem, recv_sem=recv_sem,
                                        device_id=dev, device_id_type=pl.DeviceIdType.MESH)


def _gathered_shape(shape, place):
    if place == "cols":
        return shape[:-1] + (N_DEV * shape[-1],)
    if place == "rows":
        return shape[:-2] + (N_DEV * shape[-2], shape[-1])
    return (N_DEV,) + tuple(shape)


def _all_gather(arrs, name, places=None):
    n = len(arrs)
    places = places or [None] * n

    def body(*refs):
        ins, outs = refs[:n], refs[n:2 * n]
        send_sems, recv_sems, loc_sems = refs[2 * n:]
        x, y, c, chips = _mesh_place()
        me, sibling = (x, y, c), (x, y, 1 - c)

        def slot(i, px, py, pc):
            p = 4 * px + 2 * py + pc
            if places[i] == "cols":
                width = arrs[i].shape[-1]
                return outs[i].at[:, :, pl.ds(pl.multiple_of(p * width, LANES), width)]
            if places[i] == "rows":
                height = arrs[i].shape[-2]
                return outs[i].at[:, pl.ds(pl.multiple_of(p * height, SUBLANES), height), :]
            return outs[i].at[p]

        local = []
        for i in range(n):
            cp = pltpu.make_async_copy(ins[i], slot(i, *me), loc_sems.at[i])
            cp.start()
            local.append(cp)
        sends = []
        for i in range(n):
            cp = _remote(ins[i], slot(i, *me), send_sems.at[i, 0], recv_sems.at[i, 0], sibling)
            cp.start()
            sends.append(cp)
            for j, chip in enumerate(chips):
                cp = _remote(ins[i], slot(i, *me), send_sems.at[i, 1 + j], recv_sems.at[i, 1 + j], (*chip, c))
                cp.start()
                sends.append(cp)
        for j, chip in enumerate(chips):
            for i in range(n):
                got = slot(i, *chip, c)
                _remote(got, got, send_sems.at[i, 1 + j], recv_sems.at[i, 1 + j], (*chip, c)).wait_recv()
                cp = _remote(got, got, send_sems.at[i, 4 + j], recv_sems.at[i, 4 + j], sibling)
                cp.start()
                sends.append(cp)
        for i in range(n):
            got = slot(i, *sibling)
            _remote(got, got, send_sems.at[i, 0], recv_sems.at[i, 0], sibling).wait_recv()
            for j, chip in enumerate(chips):
                got = slot(i, *chip, 1 - c)
                _remote(got, got, send_sems.at[i, 4 + j], recv_sems.at[i, 4 + j], sibling).wait_recv()
        for cp in sends:
            cp.wait_send()
        for cp in local:
            cp.wait()

    return pl.pallas_call(
        body, name=name, in_specs=[_ANY] * n, out_specs=[_ANY] * n,
        out_shape=[jax.ShapeDtypeStruct(_gathered_shape(a.shape, p), a.dtype) for a, p in zip(arrs, places)],
        scratch_shapes=[pltpu.SemaphoreType.DMA((n, N_DEV - 1)), pltpu.SemaphoreType.DMA((n, N_DEV - 1)),
                        pltpu.SemaphoreType.DMA((n,))],
    )(*arrs)


def _sibling_swap(arrs, name):
    n = len(arrs)

    def body(*refs):
        ins, outs = refs[:n], refs[n:2 * n]
        send_sems, recv_sems = refs[2 * n:]
        x, y, c, _ = _mesh_place()
        sibling = (x, y, 1 - c)
        cps = [_remote(ins[i].at[1 - c], outs[i], send_sems.at[i], recv_sems.at[i], sibling) for i in range(n)]
        for cp in cps:
            cp.start()
        for cp in cps:
            cp.wait()

    return pl.pallas_call(
        body, name=name, in_specs=[_ANY] * n, out_specs=[_ANY] * n,
        out_shape=[jax.ShapeDtypeStruct(a.shape[1:], a.dtype) for a in arrs],
        scratch_shapes=[pltpu.SemaphoreType.DMA((n,)), pltpu.SemaphoreType.DMA((n,))],
    )(*arrs)


def _chip_scatter(arrs, name):
    n = len(arrs)

    def body(*refs):
        ins, outs = refs[:n], refs[n:2 * n]
        send_sems, recv_sems, loc_sems = refs[2 * n:]
        x, y, c, chips = _mesh_place()
        mine = 2 * x + y
        local = []
        for i in range(n):
            cp = pltpu.make_async_copy(ins[i].at[mine], outs[i].at[mine], loc_sems.at[i])
            cp.start()
            local.append(cp)
        sends = []
        for i in range(n):
            for j, (px, py) in enumerate(chips):
                cp = _remote(ins[i].at[2 * px + py], outs[i].at[mine], send_sems.at[i, j], recv_sems.at[i, j],
                             (px, py, c))
                cp.start()
                sends.append(cp)
        for i in range(n):
            for j, (px, py) in enumerate(chips):
                got = outs[i].at[2 * px + py]
                _remote(got, got, send_sems.at[i, j], recv_sems.at[i, j], (px, py, c)).wait_recv()
        for cp in sends:
            cp.wait_send()
        for cp in local:
            cp.wait()

    return pl.pallas_call(
        body, name=name, in_specs=[_ANY] * n, out_specs=[_ANY] * n,
        out_shape=[jax.ShapeDtypeStruct(a.shape, a.dtype) for a in arrs],
        scratch_shapes=[pltpu.SemaphoreType.DMA((n, N_CHIPS - 1)), pltpu.SemaphoreType.DMA((n, N_CHIPS - 1)),
                        pltpu.SemaphoreType.DMA((n,))],
    )(*arrs)


def _pair_sum(own, got, name):
    _, _, rows, cols = own.shape
    tb = _row_tile(rows, cols, budget=512 * 1024)

    def body(c_ref, a_ref, b_ref, o_ref):
        o_ref[...] = (a_ref[...].astype(F32) + b_ref[...].astype(F32)).astype(o_ref.dtype)

    core = lax.axis_index("c").astype(jnp.int32).reshape(1)
    grid_spec = pltpu.PrefetchScalarGridSpec(
        num_scalar_prefetch=1, grid=(N_CHIPS, rows // tb),
        in_specs=[pl.BlockSpec((None, None, tb, cols), lambda p, i, c_ref: (c_ref[0], p, i, 0)),
                  pl.BlockSpec((None, tb, cols), lambda p, i, c_ref: (p, i, 0))],
        out_specs=pl.BlockSpec((None, tb, cols), lambda p, i, c_ref: (p, i, 0)))
    return pl.pallas_call(
        body, name=name, grid_spec=grid_spec, out_shape=jax.ShapeDtypeStruct(got.shape, got.dtype),
        compiler_params=_cparams(("parallel", "parallel")),
    )(core, own, got)


def _adamw_math(w, g, m, v):
    m = ADAM_B1 * m + (1.0 - ADAM_B1) * g
    v = ADAM_B2 * v + (1.0 - ADAM_B2) * (g * g)
    m_hat = m / (1.0 - ADAM_B1 ** ADAM_STEP)
    v_hat = v / (1.0 - ADAM_B2 ** ADAM_STEP)
    delta = -ADAM_LR * (m_hat / (jnp.sqrt(v_hat) + ADAM_EPS) + ADAM_WD * w)
    return delta, m, v


def _row_tile(rows, cols, budget=128 * 1024):
    if rows * cols <= budget or rows % SUBLANES:
        return rows
    best = SUBLANES
    for t in range(SUBLANES, rows + 1, SUBLANES):
        if rows % t == 0 and t * cols <= budget:
            best = t
    return best


def _reduce_adamw(parts, w, m, v, name):
    rows, cols = w.shape
    nparts = parts.shape[0]
    tb = _row_tile(rows, cols)

    def body(p_ref, w_ref, m_ref, v_ref, g_ref, d_ref, nm_ref, nv_ref):
        g = p_ref[0].astype(F32)
        for s in range(1, nparts):
            g = g + p_ref[s].astype(F32)
        delta, nm, nv = _adamw_math(w_ref[...], g, m_ref[...], v_ref[...])
        g_ref[...] = g
        d_ref[...] = delta
        nm_ref[...] = nm
        nv_ref[...] = nv

    spec = pl.BlockSpec((tb, cols), lambda i: (i, 0))
    shp = jax.ShapeDtypeStruct((rows, cols), F32)
    return pl.pallas_call(
        body, name=name, grid=(rows // tb,),
        in_specs=[pl.BlockSpec((nparts, tb, cols), lambda i: (0, i, 0)), spec, spec, spec],
        out_specs=[spec] * 4, out_shape=[shp] * 4,
        compiler_params=_cparams(("parallel",)),
    )(parts, w, m, v)


def _sum_parts(parts, name):
    nparts, rows, cols = parts.shape
    tb = _row_tile(rows, cols)

    def body(p_ref, g_ref):
        g = p_ref[0]
        for s in range(1, nparts):
            g = g + p_ref[s]
        g_ref[...] = g

    return pl.pallas_call(
        body, name=name, grid=(rows // tb,),
        in_specs=[pl.BlockSpec((nparts, tb, cols), lambda i: (0, i, 0))],
        out_specs=pl.BlockSpec((tb, cols), lambda i: (i, 0)),
        out_shape=jax.ShapeDtypeStruct((rows, cols), F32),
        compiler_params=_cparams(("parallel",)),
    )(parts)


def _adamw(g, w, m, v, name):
    rows, cols = w.shape
    tb = _row_tile(rows, cols)

    def body(g_ref, w_ref, m_ref, v_ref, d_ref, nm_ref, nv_ref):
        delta, nm, nv = _adamw_math(w_ref[...], g_ref[...], m_ref[...], v_ref[...])
        d_ref[...] = delta
        nm_ref[...] = nm
        nv_ref[...] = nv

    spec = pl.BlockSpec((tb, cols), lambda i: (i, 0))
    shp = jax.ShapeDtypeStruct((rows, cols), F32)
    return pl.pallas_call(
        body, name=name, grid=(rows // tb,), in_specs=[spec] * 4, out_specs=[spec] * 3, out_shape=[shp] * 3,
        compiler_params=_cparams(("parallel",)),
    )(g, w, m, v)


def _pack(arrs):
    flat = jnp.concatenate([a.reshape(-1) for a in arrs])
    pad = (-flat.shape[0]) % (PACK_ROWS * LANES)
    return jnp.pad(flat, (0, pad)).reshape(-1, LANES)


def _unpack(packed, shapes):
    flat = packed.reshape(-1)
    out, off = [], 0
    for s in shapes:
        n = math.prod(s)
        out.append(flat[off:off + n].reshape(s))
        off += n
    return out


def _block_diag_b(bt):
    bb = bt.transpose(1, 0, 2).reshape(S5_SLABS, 8, S5_GROUP, S5_STATE)
    eye = jnp.eye(8, dtype=bt.dtype)
    return (bb[:, :, :, None, :] * eye[None, :, None, :, None]).reshape(S5_SLABS, LANES, S5_SLAB_STATE)


def _block_diag_b_grad(g):
    g5 = g.reshape(S5_SLABS, 8, S5_GROUP, 8, S5_STATE)
    diag = jnp.stack([g5[:, a, :, a, :] for a in range(8)], axis=1)
    return diag.reshape(S5_GROUPS, S5_GROUP, S5_STATE).transpose(1, 0, 2)


def _block_diag_c(cw):
    cc = cw.reshape(S5_SLABS, 8, S5_GROUP, S5_STATE).transpose(0, 1, 3, 2)
    eye = jnp.eye(8, dtype=cw.dtype)
    return (cc[:, :, :, None, :] * eye[None, :, None, :, None]).reshape(S5_SLABS, S5_SLAB_STATE, LANES)


def _block_diag_c_grad(g):
    g5 = g.reshape(S5_SLABS, 8, S5_STATE, 8, S5_GROUP)
    diag = jnp.stack([g5[:, a, :, a, :] for a in range(8)], axis=1)
    return diag.transpose(0, 1, 3, 2).reshape(S5_GROUPS, S5_GROUP, S5_STATE)


def _cols_full(gathered):
    _, k, n = gathered.shape
    return gathered.transpose(1, 0, 2).reshape(k, N_DEV * n)


def _cols_split(full):
    k, n8 = full.shape
    return full.reshape(k, N_DEV, n8 // N_DEV).transpose(1, 0, 2)


SMALL_NAMES = ("mix_norm_w", "s5_log_dt", "s5_a_re", "s5_a_im", "s5_b_re", "s5_b_im", "s5_c_re", "s5_c_im", "s5_d",
               "dn_a_log", "dn_dt_bias", "dn_norm_w", "ffn_norm_w", "final_norm_w")
CONV_NAMES = ("dn_conv_w", "ffn_conv_w")
BIG_NAMES = ("w_in", "s5_glu_w", "dn_proj_w", "w_out", "ffn_up", "ffn_down")
ROW_SHARDED = ("w_out", "ffn_down")
WEIGHT_ORDER = ("mix_norm_w", "w_in", "s5_log_dt", "s5_a_re", "s5_a_im", "s5_b_re", "s5_b_im", "s5_c_re", "s5_c_im",
                "s5_d", "s5_glu_w", "dn_conv_w", "dn_a_log", "dn_dt_bias", "dn_norm_w", "dn_proj_w", "w_out",
                "ffn_norm_w", "ffn_up", "ffn_conv_w", "ffn_down", "final_norm_w")


def _layer_forward(l, x, wts, sm):
    rows = x.shape[0]
    sv = {"x0": x}
    nm = f"l{l}_"
    mixw = sm["mix_norm_w"][l][None]
    (h,) = _tile_fwd(nm + "mix_norm", _f_rms, [(x, None, 0)], [(mixw, None, 0)], [(D_MODEL, None, BF16)], rows, 256)
    proj = _mm(h, wts["w_in_main"], "nn", F32, nm + "proj")
    ba = _mm(h, wts["w_in_ba"], "nn", F32, nm + "proj_ba")
    sv.update(h=h, proj=proj, ba=ba)
    disc = _s5_disc_fwd(sm["s5_log_dt"][l][:, None], sm["s5_a_re"][l], sm["s5_a_im"][l],
                        sm["s5_b_re"][l].transpose(2, 0, 1), sm["s5_b_im"][l].transpose(2, 0, 1), nm + "s5_disc")
    abar_re, abar_im, bbar_re, bbar_im = disc
    s5p = dict(
        bd_re=_block_diag_b(bbar_re).astype(BF16), bd_im=_block_diag_b(bbar_im).astype(BF16),
        cd_re=_block_diag_c(sm["s5_c_re"][l]).astype(BF16), cd_im=_block_diag_c(sm["s5_c_im"][l]).astype(BF16),
        a_re=abar_re.reshape(S5_SLABS, 1, S5_SLAB_STATE), a_im=abar_im.reshape(S5_SLABS, 1, S5_SLAB_STATE),
        d=sm["s5_d"][l][None])
    y_s5, st_re, st_im = _s5_fwd(proj, s5p["bd_re"], s5p["bd_im"], s5p["cd_re"], s5p["cd_im"],
                                 s5p["a_re"], s5p["a_im"], s5p["d"], nm + "s5_scan")
    glu = _mm(y_s5, wts["s5_glu_w"], "nn", F32, nm + "glu")
    sv.update(s5p=s5p, y_s5=y_s5, st_re=st_re, st_im=st_im, glu=glu)
    qkv = _dn_pre_fwd(proj, wts["dn_conv_w"], nm + "dn_pre")
    pad8 = lambda a: jnp.pad(a[None], ((0, 0), (DN_HEADS, LANES - 2 * DN_HEADS)))
    alog, dtb = pad8(sm["dn_a_log"][l]), pad8(sm["dn_dt_bias"][l])
    bg, = _tile_fwd(nm + "dn_gates", _f_dn_gates, [(ba, None, 0)], [(alog, None, 0), (dtb, None, 0)],
                    [(LANES, None, F32)], rows, 512)
    bcol = bg[:, 0:DN_HEADS].T[:, :, None]
    gcol = bg[:, DN_HEADS:2 * DN_HEADS].T[:, :, None]
    o, states = _dn_fwd(qkv, gcol, bcol, nm + "dn_chunk")
    dnw = sm["dn_norm_w"][l][None]
    y_dn, = _tile_fwd(nm + "dn_post", _f_dn_post, [(o, LANES, 0), (proj, LANES, MAIN_Z // LANES)], [(dnw, None, 0)],
                      [(1024, LANES, BF16)], rows, 512, ncol=8)
    br_dn = _mm(y_dn, wts["dn_proj_w"], "nn", F32, nm + "dn_proj")
    sv.update(qkv=qkv, alog=alog, dtb=dtb, bcol=bcol, gcol=gcol, o=o, states=states, dnw=dnw,
              y_dn=y_dn, br_dn=br_dn)
    cw = 512
    merged, = _tile_fwd(nm + "merge", _f_merge,
                        [(glu, cw, 0), (glu, cw, D_MODEL // cw), (br_dn, cw, 0),
                         (proj, cw, MAIN_GS // cw), (proj, cw, MAIN_GD // cw)], [],
                        [(D_MODEL, cw, BF16)], rows, 512, ncol=D_MODEL // cw)
    x1 = _mm(merged, wts["w_out"], "nn", F32, nm + "w_out", res=x)
    sv.update(merged=merged, x1=x1)
    ffw = sm["ffn_norm_w"][l][None]
    (h2,) = _tile_fwd(nm + "ffn_norm", _f_rms, [(x1, None, 0)], [(ffw, None, 0)], [(D_MODEL, None, BF16)], rows, 256)
    up = _mm(h2, wts["ffn_up"], "nn", F32, nm + "ffn_up")
    hid = _ffn_mix_fwd(up, wts["ffn_conv_w"], nm + "ffn_mix")
    x2 = _mm(hid, wts["ffn_down"], "nn", F32, nm + "ffn_down", res=x1, tk=1408)
    sv.update(h2=h2, up=up, hid=hid)
    return x2, sv


def _wgrad(name, act, dout, l, depth, bufs, call_name):
    kdim, ndim = act.shape[1], dout.shape[1]
    if name not in PLACED_NAMES:
        return _mm(act, dout, "tn", BF16, call_name, tm=1408)
    if name in ROW_SHARDED:
        k = kdim // N_DEV
        tn = min(1024, ndim)
        place = ((2, N_CHIPS, depth * k, ndim), (None, None, k, tn),
                 lambda i, j, kk: (i % 2, i // 2, l, j), bufs.get(name))
        return _mm(act, dout, "tn", BF16, call_name, tm=k, tn=tn, place=place)
    n = ndim // N_DEV
    tm = min(1024, kdim)
    per_layer = kdim // tm
    place = ((2, N_CHIPS, depth * kdim, n), (None, None, tm, n),
             lambda i, j, kk: (j % 2, j // 2, l * per_layer + i, 0), bufs.get(name))
    return _mm(act, dout, "tn", BF16, call_name, tm=tm, tn=n, place=place)


PLACED_NAMES = ("s5_glu_w", "dn_proj_w", "w_out", "ffn_up")


def _layer_backward(l, dx2, wts, sm, sv, depth, bufs):
    rows = dx2.shape[0]
    nm = f"l{l}_b_"
    cw = 512
    gr = {}
    dxb = dx2.astype(BF16)
    gr["ffn_down"] = _wgrad("ffn_down", sv["hid"], dxb, l, depth, bufs, nm + "ffn_down_w")
    dhid = _mm(dxb, wts["ffn_down"], "nt", F32, nm + "ffn_down_x", tn=1408)
    dup_a, dup_v, dw_a, dw_v = _ffn_mix_bwd(sv["up"], wts["ffn_conv_w"], dhid, nm + "ffn_mix")
    gr["ffn_conv_w"] = jnp.concatenate([dw_a[:FFN_CONV], dw_v[:FFN_CONV]], axis=1)
    dupb = jnp.concatenate([dup_a, dup_v], axis=1)
    gr["ffn_up"] = _wgrad("ffn_up", sv["h2"], dupb, l, depth, bufs, nm + "ffn_up_w")
    dh2 = _mm(dupb, wts["ffn_up"], "nt", F32, nm + "ffn_up_x", tk=2816)
    ffw = sm["ffn_norm_w"][l][None]
    (dx1,), (dffw,) = _tile_bwd(nm + "ffn_norm", _f_rms, [(sv["x1"], None, 0)], [(ffw, None, 0)],
                                [(dh2, None, 0)], [F32], rows, 256, add_first=dx2)
    gr["ffn_norm_w"] = dffw[0]
    dx1b = dx1.astype(BF16)
    gr["w_out"] = _wgrad("w_out", sv["merged"], dx1b, l, depth, bufs, nm + "w_out_w")
    dmerged = _mm(dx1b, wts["w_out"], "nt", F32, nm + "w_out_x")
    (dga, dgb, dbr, dgs, dgd), _ = _tile_bwd(
        nm + "merge", _f_merge,
        [(sv["glu"], cw, 0), (sv["glu"], cw, D_MODEL // cw), (sv["br_dn"], cw, 0),
         (sv["proj"], cw, MAIN_GS // cw), (sv["proj"], cw, MAIN_GD // cw)], [], [(dmerged, cw, 0)],
        [BF16, BF16, BF16, BF16, BF16], rows, 512, ncol=D_MODEL // cw)
    dglu = jnp.concatenate([dga, dgb], axis=1)
    gr["s5_glu_w"] = _wgrad("s5_glu_w", sv["y_s5"], dglu, l, depth, bufs, nm + "glu_w")
    dy_s5 = _mm(dglu, wts["s5_glu_w"], "nt", F32, nm + "glu_x")
    gr["dn_proj_w"] = _wgrad("dn_proj_w", sv["y_dn"], dbr, l, depth, bufs, nm + "dn_proj_w")
    dy_dn = _mm(dbr, wts["dn_proj_w"], "nt", F32, nm + "dn_proj_x")
    s5p = sv["s5p"]
    du, gbr, gbi, gcr, gci, gar, gai, gd = _s5_bwd(
        sv["proj"], dy_s5, sv["st_re"], sv["st_im"], s5p["bd_re"], s5p["bd_im"], s5p["cd_re"], s5p["cd_im"],
        s5p["a_re"], s5p["a_im"], s5p["d"], nm + "s5_scan")
    gr["s5_d"] = gd[0]
    gr["s5_c_re"] = _block_diag_c_grad(gcr)
    gr["s5_c_im"] = _block_diag_c_grad(gci)
    bt_re, bt_im = sm["s5_b_re"][l].transpose(2, 0, 1), sm["s5_b_im"][l].transpose(2, 0, 1)
    dldt, dare, daim, dbtr, dbti = _s5_disc_bwd(
        sm["s5_log_dt"][l][:, None], sm["s5_a_re"][l], sm["s5_a_im"][l], bt_re, bt_im,
        [gar.reshape(S5_GROUPS, S5_STATE), gai.reshape(S5_GROUPS, S5_STATE),
         _block_diag_b_grad(gbr), _block_diag_b_grad(gbi)], nm + "s5_disc")
    gr.update(s5_log_dt=dldt[:, 0], s5_a_re=dare, s5_a_im=daim,
              s5_b_re=dbtr.transpose(1, 2, 0), s5_b_im=dbti.transpose(1, 2, 0))
    (do, dz), (ddnw,) = _tile_bwd(nm + "dn_post", _f_dn_post,
                                  [(sv["o"], LANES, 0), (sv["proj"], LANES, MAIN_Z // LANES)],
                                  [(sv["dnw"], None, 0)], [(dy_dn, LANES, 0)], [F32, BF16], rows, 512, ncol=8)
    gr["dn_norm_w"] = ddnw[0]
    dq, dk, dv, dgc, dbc = _dn_bwd(sv["qkv"], sv["gcol"], sv["bcol"], sv["states"], do, nm + "dn_chunk")
    dbg = jnp.pad(jnp.concatenate([dbc[:, :, 0].T, dgc[:, :, 0].T], axis=1), ((0, 0), (0, LANES - 2 * DN_HEADS)))
    (dba,), (dalog, ddtb) = _tile_bwd(nm + "dn_gates", _f_dn_gates, [(sv["ba"], None, 0)],
                                      [(sv["alog"], None, 0), (sv["dtb"], None, 0)], [(dbg, None, 0)],
                                      [BF16], rows, 512)
    gr["dn_a_log"] = dalog[0, DN_HEADS:2 * DN_HEADS]
    gr["dn_dt_bias"] = ddtb[0, DN_HEADS:2 * DN_HEADS]
    dqkv, ddnconv = _dn_pre_bwd(sv["proj"], wts["dn_conv_w"], dq, dk, dv, nm + "dn_pre")
    gr["dn_conv_w"] = ddnconv[:DN_CONV]
    dproj = jnp.concatenate([du.astype(BF16), dqkv, dz, dgs, dgd], axis=1)
    gmain = _mm(sv["h"], dproj, "tn", BF16, nm + "proj_w")
    gba = _mm(sv["h"], dba, "tn", BF16, nm + "proj_ba_w", tk=1024)
    gr["w_in"] = jnp.concatenate([gmain[:, :OFF_BA], gba[:, :2 * DN_HEADS], gmain[:, OFF_BA:]], axis=1)
    dh = _mm(dproj, wts["w_in_main"], "nt", F32, nm + "proj_x", tk=2304)
    dh = _mm(dba, wts["w_in_ba"], "nt", F32, nm + "proj_ba_x", res=dh)
    mixw = sm["mix_norm_w"][l][None]
    (dx0,), (dmixw,) = _tile_bwd(nm + "mix_norm", _f_rms, [(sv["x0"], None, 0)], [(mixw, None, 0)],
                                 [(dh, None, 0)], [F32], rows, 256, add_first=dx1)
    gr["mix_norm_w"] = dmixw[0]
    return dx0, gr


def kernel(x, mix_norm_w, w_in, s5_log_dt, s5_a_re, s5_a_im, s5_b_re, s5_b_im, s5_c_re, s5_c_im, s5_d, s5_glu_w, dn_conv_w, dn_a_log, dn_dt_bias, dn_norm_w, dn_proj_w, w_out, ffn_norm_w, ffn_up, ffn_conv_w, ffn_down, final_norm_w, loss_target, m_mix_norm_w, m_w_in, m_s5_log_dt, m_s5_a_re, m_s5_a_im, m_s5_b_re, m_s5_b_im, m_s5_c_re, m_s5_c_im, m_s5_d, m_s5_glu_w, m_dn_conv_w, m_dn_a_log, m_dn_dt_bias, m_dn_norm_w, m_dn_proj_w, m_w_out, m_ffn_norm_w, m_ffn_up, m_ffn_conv_w, m_ffn_down, m_final_norm_w, v_mix_norm_w, v_w_in, v_s5_log_dt, v_s5_a_re, v_s5_a_im, v_s5_b_re, v_s5_b_im, v_s5_c_re, v_s5_c_im, v_s5_d, v_s5_glu_w, v_dn_conv_w, v_dn_a_log, v_dn_dt_bias, v_dn_norm_w, v_dn_proj_w, v_w_out, v_ffn_norm_w, v_ffn_up, v_ffn_conv_w, v_ffn_down, v_final_norm_w):
    w = dict(mix_norm_w=mix_norm_w, w_in=w_in, s5_log_dt=s5_log_dt, s5_a_re=s5_a_re, s5_a_im=s5_a_im, s5_b_re=s5_b_re, s5_b_im=s5_b_im, s5_c_re=s5_c_re, s5_c_im=s5_c_im, s5_d=s5_d, s5_glu_w=s5_glu_w, dn_conv_w=dn_conv_w, dn_a_log=dn_a_log, dn_dt_bias=dn_dt_bias, dn_norm_w=dn_norm_w, dn_proj_w=dn_proj_w, w_out=w_out, ffn_norm_w=ffn_norm_w, ffn_up=ffn_up, ffn_conv_w=ffn_conv_w, ffn_down=ffn_down, final_norm_w=final_norm_w)
    mo = dict(mix_norm_w=m_mix_norm_w, w_in=m_w_in, s5_log_dt=m_s5_log_dt, s5_a_re=m_s5_a_re, s5_a_im=m_s5_a_im, s5_b_re=m_s5_b_re, s5_b_im=m_s5_b_im, s5_c_re=m_s5_c_re, s5_c_im=m_s5_c_im, s5_d=m_s5_d, s5_glu_w=m_s5_glu_w, dn_conv_w=m_dn_conv_w, dn_a_log=m_dn_a_log, dn_dt_bias=m_dn_dt_bias, dn_norm_w=m_dn_norm_w, dn_proj_w=m_dn_proj_w, w_out=m_w_out, ffn_norm_w=m_ffn_norm_w, ffn_up=m_ffn_up, ffn_conv_w=m_ffn_conv_w, ffn_down=m_ffn_down, final_norm_w=m_final_norm_w)
    vo = dict(mix_norm_w=v_mix_norm_w, w_in=v_w_in, s5_log_dt=v_s5_log_dt, s5_a_re=v_s5_a_re, s5_a_im=v_s5_a_im, s5_b_re=v_s5_b_re, s5_b_im=v_s5_b_im, s5_c_re=v_s5_c_re, s5_c_im=v_s5_c_im, s5_d=v_s5_d, s5_glu_w=v_s5_glu_w, dn_conv_w=v_dn_conv_w, dn_a_log=v_dn_a_log, dn_dt_bias=v_dn_dt_bias, dn_norm_w=v_dn_norm_w, dn_proj_w=v_dn_proj_w, w_out=v_w_out, ffn_norm_w=v_ffn_norm_w, ffn_up=v_ffn_up, ffn_conv_w=v_ffn_conv_w, ffn_down=v_ffn_down, final_norm_w=v_final_norm_w)
    depth = w_in.shape[0]
    me = 4 * lax.axis_index("x") + 2 * lax.axis_index("y") + lax.axis_index("c")
    xs = x[0]
    target = loss_target[0]

    gather_names = BIG_NAMES + CONV_NAMES
    places = [None if n == "w_in" else ("rows" if n in ROW_SHARDED else "cols") for n in gather_names]
    gathered = _all_gather([w[n].astype(BF16) if n in BIG_NAMES else w[n] for n in gather_names], "gather_weights",
                           places)
    full = dict(zip(gather_names, gathered))
    g_in = full["w_in"]
    full["w_in"] = g_in.transpose(1, 2, 0, 3).reshape(depth, g_in.shape[2], -1)
    layer_w = []
    for l in range(depth):
        wi = full["w_in"][l]
        lw = {n: full[n][l] for n in gather_names if n != "w_in"}
        lw["w_in_main"] = jnp.concatenate([wi[:, :OFF_BA], wi[:, OFF_GS:]], axis=1)
        lw["w_in_ba"] = jnp.pad(wi[:, OFF_BA:OFF_GS], ((0, 0), (0, LANES - 2 * DN_HEADS)))
        layer_w.append(lw)

    saved = []
    h = xs
    for l in range(depth):
        h, sv = _layer_forward(l, h, layer_w[l], w)
        saved.append(sv)
    dx, dfinal, loss_tile = _final_loss(h, final_norm_w[None], target, "final_loss")

    grads = [None] * depth
    bufs = {}
    for l in reversed(range(depth)):
        dx, grads[l] = _layer_backward(l, dx, layer_w[l], w, saved[l], depth, bufs)
        bufs = {n: grads[l][n] for n in PLACED_NAMES}

    def stacked(n):
        return jnp.stack([grads[l][n] for l in range(depth)])

    big_send = []
    for n in BIG_NAMES:
        if n in PLACED_NAMES:
            big_send.append(bufs[n])
            continue
        g = stacked(n)
        if n in ROW_SHARDED:
            g = g.reshape(depth, N_CHIPS, 2, g.shape[1] // N_DEV, g.shape[2]).transpose(2, 1, 0, 3, 4)
        else:
            g = g.reshape(depth, g.shape[1], N_CHIPS, 2, g.shape[2] // N_DEV).transpose(3, 2, 0, 1, 4)
        big_send.append(g.reshape(2, N_CHIPS, -1, g.shape[-1]))
    from_sibling = _sibling_swap(big_send, "swap_grads")
    chip_partials = [_pair_sum(own, got, "pair_sum_" + n) for n, own, got in zip(BIG_NAMES, big_send, from_sibling)]
    big_recv = _chip_scatter(chip_partials, "scatter_grads")
    small_list = [stacked(n) for n in SMALL_NAMES if n != "final_norm_w"] + [dfinal[0]]
    small_list += [stacked(n) for n in CONV_NAMES] + [loss_tile[0, 0:1]]
    small_shapes = [a.shape for a in small_list]
    (small_recv,) = _all_gather([_pack(small_list)], "gather_small")
    small_sum = _unpack(_sum_parts(small_recv, "sum_small"), small_shapes)
    loss = small_sum[-1][0]
    small_names = [n for n in SMALL_NAMES if n != "final_norm_w"] + ["final_norm_w"]
    g_out = dict(zip(small_names, small_sum[:len(small_names)]))
    for n, gfull in zip(CONV_NAMES, small_sum[len(small_names):len(small_names) + 2]):
        shard = w[n].shape[-1]
        g_out[n] = lax.dynamic_slice_in_dim(gfull, me * shard, shard, axis=2)

    d_out, m_out, v_out = {}, {}, {}
    for n, parts in zip(BIG_NAMES, big_recv):
        shp = w[n].shape
        two = lambda a: a.reshape(-1, shp[-1])
        g2, d2, m2, v2 = _reduce_adamw(parts, two(w[n]), two(mo[n]), two(vo[n]), "adamw_" + n)
        g_out[n], d_out[n], m_out[n], v_out[n] = (a.reshape(shp) for a in (g2, d2, m2, v2))
    rest = list(small_names) + list(CONV_NAMES)
    rest_shapes = [w[n].shape for n in rest]
    d2, m2, v2 = _adamw(_pack([g_out[n] for n in rest]), _pack([w[n] for n in rest]), _pack([mo[n] for n in rest]),
                        _pack([vo[n] for n in rest]), "adamw_small")
    for n, d, m_, v_ in zip(rest, _unpack(d2, rest_shapes), _unpack(m2, rest_shapes), _unpack(v2, rest_shapes)):
        d_out[n], m_out[n], v_out[n] = d, m_, v_

    return (loss, dx[None], *[g_out[n] for n in WEIGHT_ORDER], *[d_out[n] for n in WEIGHT_ORDER],
            *[m_out[n] for n in WEIGHT_ORDER], *[v_out[n] for n in WEIGHT_ORDER])
```

```python
import functools
import math

import jax
import jax.numpy as jnp
from jax import lax
from jax.experimental import pallas as pl
from jax.experimental.pallas import tpu as pltpu

F32 = jnp.float32
BF16 = jnp.bfloat16

D_MODEL = 2048
DEPTH = 4
S5_WIDTH = 1024
S5_GROUP = 16
S5_GROUPS = 64
S5_STATE = 64
DN_HEADS = 8
DN_DK = 128
DN_QKV = 3072
DN_CONV = 4
DN_CHUNK = 64
FFN_DIM = 5632
FFN_CONV = 3
NORM_EPS = 1e-6
N_IN = 9232
OFF_Z = 4096
OFF_BA = 5120
OFF_GS = 5136
N_MAIN = 9216
MAIN_Z = 4096
MAIN_GS = 5120
MAIN_GD = 7168

ADAM_LR = 0.001
ADAM_B1 = 0.9
ADAM_B2 = 0.999
ADAM_EPS = 1e-08
ADAM_WD = 0.01
ADAM_STEP = 10

N_DEV = 8
LANES = 128
SUBLANES = 8
VMEM_LIMIT_BYTES = 48 * 1024 * 1024

S5_SLABS = 8
S5_SLAB_STATE = 512
S5_TB = 256
DN_TB = 512
PACK_ROWS = 512


def _cparams(sem):
    return pltpu.CompilerParams(dimension_semantics=sem, vmem_limit_bytes=VMEM_LIMIT_BYTES)


def _dot(a, b, dims, precision=None):
    return lax.dot_general(a, b, (dims, ((), ())), precision=precision, preferred_element_type=F32)


_NN = ((1,), (0,))
_NT = ((1,), (1,))
_TN = ((0,), (0,))


def _mm(a, b, mode, out_dtype, name, res=None, tm=1024, tn=1024, tk=2048, place=None):
    if mode == "nn":
        (m, k), (_, n) = a.shape, b.shape
    elif mode == "nt":
        (m, k), (n, _) = a.shape, b.shape
    else:
        (k, m), (_, n) = a.shape, b.shape
    tm, tn, tk = min(tm, m), min(tn, n), min(tk, k)
    assert m % tm == 0 and n % tn == 0 and k % tk == 0, (name, a.shape, b.shape)
    nk = k // tk
    if mode == "tn":
        a_spec = pl.BlockSpec((tk, tm), lambda i, j, kk: (kk, i))
    else:
        a_spec = pl.BlockSpec((tm, tk), lambda i, j, kk: (i, kk))
    if mode == "nt":
        b_spec = pl.BlockSpec((tn, tk), lambda i, j, kk: (j, kk))
    else:
        b_spec = pl.BlockSpec((tk, tn), lambda i, j, kk: (kk, j))
    dims = {"nn": _NN, "nt": _NT, "tn": _TN}[mode]
    o_spec = pl.BlockSpec((tm, tn), lambda i, j, kk: (i, j))
    has_res = res is not None

    has_buf = place is not None and place[3] is not None

    def body(*refs):
        if has_buf:
            refs = refs[:-3] + refs[-2:]
        if has_res:
            a_ref, b_ref, r_ref, o_ref, acc = refs
        else:
            a_ref, b_ref, o_ref, acc = refs
        p = _dot(a_ref[...], b_ref[...], dims)

        def finish(total):
            if has_res:
                total = total + r_ref[...]
            o_ref[...] = total.astype(out_dtype)

        if nk == 1:
            finish(p)
        else:
            kk = pl.program_id(2)

            @pl.when(kk == 0)
            def _():
                acc[...] = p

            @pl.when(jnp.logical_and(kk > 0, kk < nk - 1))
            def _():
                acc[...] += p

            @pl.when(kk == nk - 1)
            def _():
                finish(acc[...] + p)

    in_specs = [a_spec, b_spec] + ([o_spec] if has_res else [])
    args = (a, b) + ((res,) if has_res else ())
    out_shape, out_spec, aliases = jax.ShapeDtypeStruct((m, n), out_dtype), o_spec, {}
    if place is not None:
        shape, block, index_map, buf = place
        out_shape, out_spec = jax.ShapeDtypeStruct(shape, out_dtype), pl.BlockSpec(block, index_map)
        if buf is not None:
            aliases = {len(args): 0}
            in_specs = in_specs + [pl.BlockSpec(memory_space=pl.ANY)]
            args = args + (buf,)
    return pl.pallas_call(
        body, name=name, grid=(m // tm, n // tn, nk), in_specs=in_specs, out_specs=out_spec,
        out_shape=out_shape, input_output_aliases=aliases,
        scratch_shapes=[pltpu.VMEM((tm, tn) if nk > 1 else (SUBLANES, LANES), F32)],
        compiler_params=_cparams(("parallel", "parallel", "arbitrary")),
    )(*args)


def _row_spec(tb, width, cw, off):
    if cw is None:
        return pl.BlockSpec((tb, width), lambda j, i: (i, 0))
    return pl.BlockSpec((tb, cw), lambda j, i: (i, j + off))


def _par_spec(rows, width, cw, off):
    if cw is None:
        return pl.BlockSpec((rows, width), lambda j, i: (0, 0))
    return pl.BlockSpec((rows, cw), lambda j, i: (0, j + off))


def _tile_fwd(name, fn, tiled, params, outs, rows, tb, ncol=1):
    tb = min(tb, rows)
    nt, npar = len(tiled), len(params)

    def body(*refs):
        vals = [r[...] for r in refs[:nt + npar]]
        res = fn(*vals)
        for o_ref, r in zip(refs[nt + npar:], res):
            o_ref[...] = r.astype(o_ref.dtype)

    in_specs = [_row_spec(tb, a.shape[1], cw, off) for a, cw, off in tiled]
    in_specs += [_par_spec(a.shape[0], a.shape[1], cw, off) for a, cw, off in params]
    out_specs = [_row_spec(tb, w, cw, 0) for w, cw, _ in outs]
    out_shape = [jax.ShapeDtypeStruct((rows, w), dt) for w, _, dt in outs]
    return pl.pallas_call(
        body, name=name, grid=(ncol, rows // tb), in_specs=in_specs, out_specs=out_specs, out_shape=out_shape,
        compiler_params=_cparams(("parallel", "parallel")),
    )(*[a for a, _, _ in tiled], *[a for a, _, _ in params])


def _tile_bwd(name, fn, tiled, params, cots, gdtypes, rows, tb, ncol=1, add_first=None):
    tb = min(tb, rows)
    nt, npar, nc = len(tiled), len(params), len(cots)
    want = [i for i, g in enumerate(gdtypes) if g is not None]
    nadd = 0 if add_first is None else 1

    def body(*refs):
        vals = [r[...] for r in refs[:nt + npar]]
        cot_refs = refs[nt + npar:nt + npar + nc]
        base = nt + npar + nc + nadd
        g_refs = refs[base:base + len(want)]
        p_refs = refs[base + len(want):]
        _, vjp = jax.vjp(fn, *vals)
        grads = list(vjp(tuple(c[...].astype(F32) for c in cot_refs)))
        if nadd:
            grads[0] = grads[0] + refs[base - 1][...]
        for g_ref, i in zip(g_refs, want):
            g_ref[...] = grads[i].astype(g_ref.dtype)
        jcol, irow = pl.program_id(0), pl.program_id(1)
        for p_ref, g, (_, cw, _) in zip(p_refs, grads[nt:], params):
            first = (irow == 0) if cw is not None else jnp.logical_and(irow == 0, jcol == 0)

            @pl.when(first)
            def _():
                p_ref[...] = g

            @pl.when(jnp.logical_not(first))
            def _():
                p_ref[...] += g

    in_specs = [_row_spec(tb, a.shape[1], cw, off) for a, cw, off in tiled]
    in_specs += [_par_spec(a.shape[0], a.shape[1], cw, off) for a, cw, off in params]
    in_specs += [_row_spec(tb, a.shape[1], cw, off) for a, cw, off in cots]
    extra = ()
    if nadd:
        in_specs.append(_row_spec(tb, tiled[0][0].shape[1], tiled[0][1], tiled[0][2]))
        extra = (add_first,)
    out_specs, out_shape = [], []
    for i in want:
        a, cw, _ = tiled[i]
        width = a.shape[1] if cw is None else ncol * cw
        out_specs.append(_row_spec(tb, width, cw, 0))
        out_shape.append(jax.ShapeDtypeStruct((rows, width), gdtypes[i]))
    for a, cw, _ in params:
        width = a.shape[1] if cw is None else ncol * cw
        out_specs.append(_par_spec(a.shape[0], width, cw, 0))
        out_shape.append(jax.ShapeDtypeStruct((a.shape[0], width), F32))
    res = pl.pallas_call(
        body, name=name, grid=(ncol, rows // tb), in_specs=in_specs, out_specs=out_specs, out_shape=out_shape,
        compiler_params=_cparams(("arbitrary", "arbitrary")),
    )(*[a for a, _, _ in tiled], *[a for a, _, _ in params], *[a for a, _, _ in cots], *extra)
    return res[:len(want)], res[len(want):]


def _sigmoid(x):
    return 1.0 / (1.0 + jnp.exp(-x))


def _silu(x):
    return x * _sigmoid(x)


def _softplus(x):
    return jnp.maximum(x, 0.0) + jnp.log1p(jnp.exp(-jnp.abs(x)))


def _f_rms(x, w):
    return (x * lax.rsqrt(jnp.mean(x * x, axis=-1, keepdims=True) + NORM_EPS) * w,)


def _f_merge(glu_a, glu_b, br_dn, gs, gd):
    return (_sigmoid(gs) * (glu_a * _sigmoid(glu_b)) + _sigmoid(gd) * br_dn,)


def _f_ffn_gate(act, val):
    return (_silu(act) * val,)


def _l2n(x):
    return x * lax.rsqrt(jnp.sum(x * x, axis=-1, keepdims=True) + NORM_EPS)


def _f_dn_q(c):
    return (_l2n(_silu(c)) * (DN_DK ** -0.5),)


def _f_dn_k(c):
    return (_l2n(_silu(c)),)


def _f_dn_v(c):
    return (_silu(c),)


def _f_dn_gates(ba, a_log, dt_bias):
    col = lax.broadcasted_iota(jnp.int32, ba.shape, 1)
    beta = _sigmoid(ba)
    g = -jnp.exp(a_log) * _softplus(ba + dt_bias)
    return (jnp.where(col < DN_HEADS, beta, jnp.where(col < 2 * DN_HEADS, g, 0.0)),)


def _f_dn_post(o, z, w):
    return (_f_rms(o, w)[0] * _silu(z),)


def _shift_down(x, halo, s, tb):
    if s == 0:
        return x
    y = pltpu.roll(x, s, 0)
    row8 = lax.broadcasted_iota(jnp.int32, halo.shape, 0)
    top = jnp.where(row8 < s, pltpu.roll(halo, s, 0), y[0:SUBLANES])
    if tb == SUBLANES:
        return top
    return jnp.concatenate([top, y[SUBLANES:]], axis=0)


def _shift_up(x, halo, s, tb):
    if s == 0:
        return x
    y = pltpu.roll(x, tb - s, 0)
    row8 = lax.broadcasted_iota(jnp.int32, halo.shape, 0)
    bot = jnp.where(row8 >= SUBLANES - s, pltpu.roll(halo, SUBLANES - s, 0), y[tb - SUBLANES:])
    if tb == SUBLANES:
        return bot
    return jnp.concatenate([y[:tb - SUBLANES], bot], axis=0)


def _conv_fwd(x, w, kw, name, x_off=0, width=None, cw=512, tb=512):
    rows = x.shape[0]
    width = w.shape[1] if width is None else width
    tb = min(tb, rows)
    nb = tb // SUBLANES

    def body(x_ref, h_ref, w_ref, o_ref):
        i = pl.program_id(1)
        xv = x_ref[...]
        halo = jnp.where(i > 0, h_ref[...], 0.0)
        acc = w_ref[kw - 1:kw, :] * xv
        for s in range(1, kw):
            acc = acc + w_ref[kw - 1 - s:kw - s, :] * _shift_down(xv, halo, s, tb)
        o_ref[...] = acc

    return pl.pallas_call(
        body, name=name, grid=(width // cw, rows // tb),
        in_specs=[pl.BlockSpec((tb, cw), lambda j, i: (i, j + x_off)),
                  pl.BlockSpec((SUBLANES, cw), lambda j, i: (jnp.maximum(i * nb - 1, 0), j + x_off)),
                  pl.BlockSpec((kw, cw), lambda j, i: (0, j))],
        out_specs=pl.BlockSpec((tb, cw), lambda j, i: (i, j)),
        out_shape=jax.ShapeDtypeStruct((rows, width), F32),
        compiler_params=_cparams(("parallel", "parallel")),
    )(x, x, w)


def _conv_bwd(x, w, dout, kw, name, x_off=0, cw=512, tb=512):
    rows, width = dout.shape
    tb = min(tb, rows)
    nb = tb // SUBLANES
    nrow = rows // tb

    def body(x_ref, h_ref, w_ref, d_ref, dn_ref, dx_ref, dw_ref):
        i = pl.program_id(1)
        xv, dv = x_ref[...], d_ref[...]
        halo = jnp.where(i > 0, h_ref[...], 0.0)
        nxt = jnp.where(i < nrow - 1, dn_ref[...], 0.0)

        @pl.when(i == 0)
        def _():
            dw_ref[...] = jnp.zeros_like(dw_ref)

        acc = w_ref[kw - 1:kw, :] * dv
        dw_ref[kw - 1:kw, :] += jnp.sum(dv * xv, axis=0, keepdims=True)
        for s in range(1, kw):
            acc = acc + w_ref[kw - 1 - s:kw - s, :] * _shift_up(dv, nxt, s, tb)
            dw_ref[kw - 1 - s:kw - s, :] += jnp.sum(dv * _shift_down(xv, halo, s, tb), axis=0, keepdims=True)
        dx_ref[...] = acc

    return pl.pallas_call(
        body, name=name, grid=(width // cw, nrow),
        in_specs=[pl.BlockSpec((tb, cw), lambda j, i: (i, j + x_off)),
                  pl.BlockSpec((SUBLANES, cw), lambda j, i: (jnp.maximum(i * nb - 1, 0), j + x_off)),
                  pl.BlockSpec((kw, cw), lambda j, i: (0, j)),
                  pl.BlockSpec((tb, cw), lambda j, i: (i, j)),
                  pl.BlockSpec((SUBLANES, cw), lambda j, i: (jnp.minimum((i + 1) * nb, rows // SUBLANES - 1), j))],
        out_specs=[pl.BlockSpec((tb, cw), lambda j, i: (i, j)),
                   pl.BlockSpec((SUBLANES, cw), lambda j, i: (0, j))],
        out_shape=[jax.ShapeDtypeStruct((rows, width), F32), jax.ShapeDtypeStruct((SUBLANES, width), F32)],
        compiler_params=_cparams(("parallel", "arbitrary")),
    )(x, x, w, dout, dout)


def _conv_taps(xv, halo, w_ref, kw, tb):
    acc = w_ref[kw - 1:kw, :] * xv
    for s in range(1, kw):
        acc = acc + w_ref[kw - 1 - s:kw - s, :] * _shift_down(xv, halo, s, tb)
    return acc


def _ffn_mix_specs(rows, f, kw, cw, tb):
    nj, nb, last8 = f // cw, tb // SUBLANES, rows // SUBLANES - 1
    blk = lambda off: pl.BlockSpec((tb, cw), lambda j, i: (i, j + off))
    prev = lambda off: pl.BlockSpec((SUBLANES, cw), lambda j, i: (jnp.maximum(i * nb - 1, 0), j + off))
    nxt = lambda off: pl.BlockSpec((SUBLANES, cw), lambda j, i: (jnp.minimum((i + 1) * nb, last8), j + off))
    wsp = lambda off: pl.BlockSpec((kw, cw), lambda j, i: (0, j + off))
    return nj, blk, prev, nxt, wsp


def _ffn_mix_fwd(up, w, name, cw=512, tb=512):
    rows, f, kw = up.shape[0], up.shape[1] // 2, w.shape[0]
    tb = min(tb, rows)
    nj, blk, prev, _, wsp = _ffn_mix_specs(rows, f, kw, cw, tb)

    def body(xa_ref, xv_ref, ha_ref, hv_ref, wa_ref, wv_ref, o_ref):
        first = pl.program_id(1) == 0
        ca = _conv_taps(xa_ref[...], jnp.where(first, 0.0, ha_ref[...]), wa_ref, kw, tb)
        cv = _conv_taps(xv_ref[...], jnp.where(first, 0.0, hv_ref[...]), wv_ref, kw, tb)
        o_ref[...] = _f_ffn_gate(ca, cv)[0].astype(BF16)

    return pl.pallas_call(
        body, name=name, grid=(nj, rows // tb),
        in_specs=[blk(0), blk(nj), prev(0), prev(nj), wsp(0), wsp(nj)],
        out_specs=blk(0), out_shape=jax.ShapeDtypeStruct((rows, f), BF16),
        compiler_params=_cparams(("parallel", "parallel")),
    )(up, up, up, up, w, w)


def _ffn_mix_bwd(up, w, dhid, name, cw=512, tb=512):
    rows, f, kw = up.shape[0], up.shape[1] // 2, w.shape[0]
    tb = min(tb, rows)
    te = tb + SUBLANES
    nrow = rows // tb
    nj, blk, prev, nxt, wsp = _ffn_mix_specs(rows, f, kw, cw, tb)

    def body(xa_ref, xv_ref, ha_ref, hv_ref, na_ref, nv_ref, wa_ref, wv_ref, d_ref, dn_ref,
             da_ref, dv_ref, dwa_ref, dwv_ref):
        i = pl.program_id(1)
        first, last = i == 0, i == nrow - 1

        @pl.when(first)
        def _():
            dwa_ref[...] = jnp.zeros_like(dwa_ref)
            dwv_ref[...] = jnp.zeros_like(dwv_ref)

        def conv_ext(x_ref, h_ref, n_ref, w_ref):
            xe = jnp.concatenate([x_ref[...], jnp.where(last, 0.0, n_ref[...])], axis=0)
            return _conv_taps(xe, jnp.where(first, 0.0, h_ref[...]), w_ref, kw, te)

        dh = jnp.concatenate([d_ref[...], jnp.where(last, 0.0, dn_ref[...])], axis=0)
        _, vjp = jax.vjp(_f_ffn_gate, conv_ext(xa_ref, ha_ref, na_ref, wa_ref), conv_ext(xv_ref, hv_ref, nv_ref, wv_ref))
        dca, dcv = vjp((dh,))

        def back(dc_ext, x_ref, h_ref, w_ref, dx_ref, dw_ref):
            dc = dc_ext[0:tb]
            xv = x_ref[...]
            halo = jnp.where(first, 0.0, h_ref[...])
            acc = w_ref[kw - 1:kw, :] * dc
            dw_ref[kw - 1:kw, :] += jnp.sum(dc * xv, axis=0, keepdims=True)
            for s in range(1, kw):
                acc = acc + w_ref[kw - 1 - s:kw - s, :] * pltpu.roll(dc_ext, te - s, 0)[0:tb]
                dw_ref[kw - 1 - s:kw - s, :] += jnp.sum(dc * _shift_down(xv, halo, s, tb), axis=0, keepdims=True)
            dx_ref[...] = acc.astype(dx_ref.dtype)

        back(dca, xa_ref, ha_ref, wa_ref, da_ref, dwa_ref)
        back(dcv, xv_ref, hv_ref, wv_ref, dv_ref, dwv_ref)

    wide = jax.ShapeDtypeStruct((rows, f), BF16)
    taps = jax.ShapeDtypeStruct((SUBLANES, f), F32)
    tap_spec = pl.BlockSpec((SUBLANES, cw), lambda j, i: (0, j))
    return pl.pallas_call(
        body, name=name, grid=(nj, nrow),
        in_specs=[blk(0), blk(nj), prev(0), prev(nj), nxt(0), nxt(nj), wsp(0), wsp(nj), blk(0), nxt(0)],
        out_specs=[blk(0), blk(0), tap_spec, tap_spec], out_shape=[wide, wide, taps, taps],
        compiler_params=_cparams(("parallel", "arbitrary")),
    )(up, up, up, up, up, up, w, w, dhid, dhid)


DN_PRE_TB = 1024


def _dn_pre_specs(rows, tb):
    nb, last8, off = tb // SUBLANES, rows // SUBLANES - 1, S5_WIDTH // LANES
    blk = pl.BlockSpec((tb, LANES), lambda j, i: (i, j + off))
    prev = pl.BlockSpec((SUBLANES, LANES), lambda j, i: (jnp.maximum(i * nb - 1, 0), j + off))
    nxt = pl.BlockSpec((SUBLANES, LANES), lambda j, i: (jnp.minimum((i + 1) * nb, last8), j + off))
    wsp = pl.BlockSpec((DN_CONV, LANES), lambda j, i: (0, j))
    return blk, prev, nxt, wsp


def _dn_head_kind(j, q_fn, k_fn, v_fn):
    pl.when(j < DN_HEADS)(q_fn)
    pl.when(jnp.logical_and(j >= DN_HEADS, j < 2 * DN_HEADS))(k_fn)
    pl.when(j >= 2 * DN_HEADS)(v_fn)


def _dn_pre_fwd(proj, w, name):
    rows = proj.shape[0]
    tb = min(DN_PRE_TB, rows)
    blk, prev, _, wsp = _dn_pre_specs(rows, tb)

    def body(x_ref, h_ref, w_ref, o_ref):
        first = pl.program_id(1) == 0
        c = _conv_taps(x_ref[...], jnp.where(first, 0.0, h_ref[...]), w_ref, DN_CONV, tb)

        def store(fn):
            def run():
                o_ref[...] = fn(c)[0]
            return run

        _dn_head_kind(pl.program_id(0), store(_f_dn_q), store(_f_dn_k), store(_f_dn_v))

    return pl.pallas_call(
        body, name=name, grid=(DN_QKV // LANES, rows // tb), in_specs=[blk, prev, wsp],
        out_specs=pl.BlockSpec((tb, LANES), lambda j, i: (i, j)),
        out_shape=jax.ShapeDtypeStruct((rows, DN_QKV), F32),
        compiler_params=_cparams(("parallel", "parallel")),
    )(proj, proj, w)


def _dn_pre_bwd(proj, w, dq, dk, dv, name):
    rows = proj.shape[0]
    tb = min(DN_PRE_TB, rows)
    te = tb + SUBLANES
    nrow = rows // tb
    kw = DN_CONV
    blk, prev, nxt, wsp = _dn_pre_specs(rows, tb)
    nb, last8 = tb // SUBLANES, rows // SUBLANES - 1

    def cot_specs(part):
        col = lambda j: jnp.clip(j - part * DN_HEADS, 0, DN_HEADS - 1)
        return (pl.BlockSpec((tb, LANES), lambda j, i: (i, col(j))),
                pl.BlockSpec((SUBLANES, LANES), lambda j, i: (jnp.minimum((i + 1) * nb, last8), col(j))))

    def body(x_ref, h_ref, n_ref, w_ref, dq_ref, dqn_ref, dk_ref, dkn_ref, dv_ref, dvn_ref, dx_ref, dw_ref):
        i = pl.program_id(1)
        first, last = i == 0, i == nrow - 1

        @pl.when(first)
        def _():
            dw_ref[...] = jnp.zeros_like(dw_ref)

        xv = x_ref[...]
        halo = jnp.where(first, 0.0, h_ref[...])
        c_ext = _conv_taps(jnp.concatenate([xv, jnp.where(last, 0.0, n_ref[...])], axis=0), halo, w_ref, kw, te)

        def back(fn, d_ref, dn_ref):
            def run():
                ct = jnp.concatenate([d_ref[...], jnp.where(last, 0.0, dn_ref[...])], axis=0)
                _, vjp = jax.vjp(fn, c_ext)
                (dc_ext,) = vjp((ct,))
                dc = dc_ext[0:tb]
                acc = w_ref[kw - 1:kw, :] * dc
                dw_ref[kw - 1:kw, :] += jnp.sum(dc * xv, axis=0, keepdims=True)
                for s in range(1, kw):
                    acc = acc + w_ref[kw - 1 - s:kw - s, :] * pltpu.roll(dc_ext, te - s, 0)[0:tb]
                    dw_ref[kw - 1 - s:kw - s, :] += jnp.sum(dc * _shift_down(xv, halo, s, tb), axis=0, keepdims=True)
                dx_ref[...] = acc.astype(dx_ref.dtype)
            return run

        _dn_head_kind(pl.program_id(0), back(_f_dn_q, dq_ref, dqn_ref), back(_f_dn_k, dk_ref, dkn_ref),
                      back(_f_dn_v, dv_ref, dvn_ref))

    return pl.pallas_call(
        body, name=name, grid=(DN_QKV // LANES, nrow),
        in_specs=[blk, prev, nxt, wsp, *cot_specs(0), *cot_specs(1), *cot_specs(2)],
        out_specs=[pl.BlockSpec((tb, LANES), lambda j, i: (i, j)),
                   pl.BlockSpec((SUBLANES, LANES), lambda j, i: (0, j))],
        out_shape=[jax.ShapeDtypeStruct((rows, DN_QKV), BF16), jax.ShapeDtypeStruct((SUBLANES, DN_QKV), F32)],
        compiler_params=_cparams(("parallel", "arbitrary")),
    )(proj, proj, proj, w, dq, dq, dk, dk, dv, dv)


def _f_s5_disc(log_dt, a_re, a_im, bt_re, bt_im):
    dt = jnp.exp(log_dt)
    mag = jnp.exp(a_re * dt)
    abar_re, abar_im = mag * jnp.cos(a_im * dt), mag * jnp.sin(a_im * dt)
    den = a_re * a_re + a_im * a_im
    nr, ni = abar_re - 1.0, abar_im
    coef_re = (nr * a_re + ni * a_im) / den
    coef_im = (ni * a_re - nr * a_im) / den
    bbar_re = coef_re[None] * bt_re - coef_im[None] * bt_im
    bbar_im = coef_re[None] * bt_im + coef_im[None] * bt_re
    return abar_re, abar_im, bbar_re, bbar_im


def _s5_disc_fwd(log_dt, a_re, a_im, bt_re, bt_im, name):
    def body(*refs):
        res = _f_s5_disc(*[r[...] for r in refs[:5]])
        for o_ref, r in zip(refs[5:], res):
            o_ref[...] = r

    shp = [a_re, a_re, bt_re, bt_re]
    return pl.pallas_call(body, name=name, out_shape=[jax.ShapeDtypeStruct(s.shape, F32) for s in shp])(
        log_dt, a_re, a_im, bt_re, bt_im)


def _s5_disc_bwd(log_dt, a_re, a_im, bt_re, bt_im, cots, name):
    def body(*refs):
        _, vjp = jax.vjp(_f_s5_disc, *[r[...] for r in refs[:5]])
        grads = vjp(tuple(r[...] for r in refs[5:9]))
        for o_ref, g in zip(refs[9:], grads):
            o_ref[...] = g

    ins = [log_dt, a_re, a_im, bt_re, bt_im]
    return pl.pallas_call(body, name=name, out_shape=[jax.ShapeDtypeStruct(s.shape, F32) for s in ins])(*ins, *cots)


def _cmul(ar, ai, br, bi):
    return ar * br - ai * bi, ar * bi + ai * br


def _seg_scan(xr_ref, xi_ref, ar, ai, cr, ci, pr_ref, pi_ref, tb, reverse):
    sl = tb // SUBLANES
    pitch = _seg_pitch(sl)
    groups = xr_ref.shape[0]
    lane = lambda a, k: a[:, k * LANES:(k + 1) * LANES]
    outs_r, outs_i = [], []
    a_k = [(lane(ar, k), lane(ai, k)) for k in range(groups)]

    def step(i, carry):
        t = sl - 1 - i if reverse else i
        rows = pl.ds(t, SUBLANES, stride=pitch)
        nxt = []
        for k, (xr, xi, pr, pi) in enumerate(carry):
            akr, aki = a_k[k]
            mr, mi = _cmul(akr, aki, xr, xi)
            xr, xi = mr + xr_ref[k, rows, :], mi + xi_ref[k, rows, :]
            xr_ref[k, rows, :] = xr
            xi_ref[k, rows, :] = xi
            pr_ref[k, pl.ds(t, 1), :] = pr
            pi_ref[k, pl.ds(t, 1), :] = pi
            nr, ni = _cmul(akr, aki, pr, pi)
            nxt.append((xr, xi, nr, ni))
        return tuple(nxt)

    zero = jnp.zeros((SUBLANES, LANES), F32)
    lax.fori_loop(0, sl, step, tuple((zero, zero, akr, aki) for akr, aki in a_k))
    last = 0 if reverse else sl - 1
    for k in range(groups):
        qr, qi = pr_ref[k, last:last + 1, :], pi_ref[k, last:last + 1, :]
        tr, ti = pr_ref[k], pi_ref[k]
        ckr, cki = lane(cr, k), lane(ci, k)
        order = range(SUBLANES - 1, -1, -1) if reverse else range(SUBLANES)
        for j in order:
            rows = slice(j * pitch, j * pitch + sl)
            edge = j * pitch if reverse else j * pitch + sl - 1
            er, ei = xr_ref[k, edge:edge + 1, :], xi_ref[k, edge:edge + 1, :]
            mr, mi = _cmul(tr, ti, ckr, cki)
            xr_ref[k, rows, :] += mr
            xi_ref[k, rows, :] += mi
            mr, mi = _cmul(qr, qi, ckr, cki)
            ckr, cki = er + mr, ei + mi
        outs_r.append(ckr)
        outs_i.append(cki)
    return jnp.concatenate(outs_r, axis=1), jnp.concatenate(outs_i, axis=1)


def _seg_pitch(sl):
    return sl + SUBLANES


def _to_groups(ref, val):
    sl = val.shape[0] // SUBLANES
    pitch = _seg_pitch(sl)
    for k in range(ref.shape[0]):
        for j in range(SUBLANES):
            ref[k, j * pitch:j * pitch + sl, :] = val[j * sl:(j + 1) * sl, k * LANES:(k + 1) * LANES]


def _from_groups(ref):
    pitch = ref.shape[1] // SUBLANES
    sl = pitch - SUBLANES
    return jnp.concatenate(
        [jnp.concatenate([ref[k, j * pitch:j * pitch + sl, :] for j in range(SUBLANES)], axis=0)
         for k in range(ref.shape[0])], axis=1)


_INV_SQRT2 = 1.0 / math.sqrt(2.0)
_INV_SQRT2PI = 1.0 / math.sqrt(2.0 * math.pi)


def _gelu(y):
    return 0.5 * y * (1.0 + lax.erf(y * _INV_SQRT2))


def _gelu_grad(y):
    return 0.5 * (1.0 + lax.erf(y * _INV_SQRT2)) + y * jnp.exp(-0.5 * y * y) * _INV_SQRT2PI


def _s5_states(u, bd_re, bd_im, ar, ai, cr, ci, xr_ref, xi_ref, pr_ref, pi_ref, tb):
    ub = u.astype(BF16)
    _to_groups(xr_ref, _dot(ub, bd_re, _NN))
    _to_groups(xi_ref, _dot(ub, bd_im, _NN))
    return _seg_scan(xr_ref, xi_ref, ar, ai, cr, ci, pr_ref, pi_ref, tb, reverse=False)


def _s5_scratch(tb, nbuf):
    groups = S5_SLAB_STATE // LANES
    sl = tb // SUBLANES
    return ([pltpu.VMEM((1, S5_SLAB_STATE), F32)] * 2
            + [pltpu.VMEM((groups, SUBLANES * _seg_pitch(sl), LANES), F32)] * nbuf
            + [pltpu.VMEM((groups, sl, LANES), F32)] * 2)


def _s5_specs(tb, u_off):
    slab3 = lambda r, c: pl.BlockSpec((None, r, c), lambda s, t: (s, 0, 0))
    return dict(
        u=lambda tmap: pl.BlockSpec((tb, LANES), lambda s, t: (tmap(t), s + u_off)),
        bd=slab3(LANES, S5_SLAB_STATE), cd=slab3(S5_SLAB_STATE, LANES), a=slab3(1, S5_SLAB_STATE),
        d=pl.BlockSpec((1, LANES), lambda s, t: (0, s)))


def _s5_fwd(proj, bd_re, bd_im, cd_re, cd_im, a_re, a_im, d, name):
    rows = proj.shape[0]
    tb = min(S5_TB, rows)
    nt = rows // tb
    sp = _s5_specs(tb, 0)

    def body(u_ref, bdr, bdi, cdr, cdi, ar_ref, ai_ref, d_ref, y_ref, sr_ref, si_ref,
             cr_s, ci_s, xr_s, xi_s, pr_s, pi_s):
        t = pl.program_id(1)

        @pl.when(t == 0)
        def _():
            cr_s[...] = jnp.zeros_like(cr_s)
            ci_s[...] = jnp.zeros_like(ci_s)

        cr, ci = cr_s[...], ci_s[...]
        sr_ref[...] = cr
        si_ref[...] = ci
        u = u_ref[...]
        cr, ci = _s5_states(u, bdr[...], bdi[...], ar_ref[...], ai_ref[...], cr, ci, xr_s, xi_s, pr_s, pi_s, tb)
        cr_s[...] = cr
        ci_s[...] = ci
        y = (_dot(_from_groups(xr_s).astype(BF16), cdr[...], _NN)
             - _dot(_from_groups(xi_s).astype(BF16), cdi[...], _NN) + d_ref[...] * u)
        y_ref[...] = _gelu(y).astype(BF16)

    st_spec = pl.BlockSpec((None, None, 1, S5_SLAB_STATE), lambda s, t: (s, t, 0, 0))
    st_shape = jax.ShapeDtypeStruct((S5_SLABS, nt, 1, S5_SLAB_STATE), F32)
    return pl.pallas_call(
        body, name=name, grid=(S5_SLABS, nt),
        in_specs=[sp["u"](lambda t: t), sp["bd"], sp["bd"], sp["cd"], sp["cd"], sp["a"], sp["a"], sp["d"]],
        out_specs=[pl.BlockSpec((tb, LANES), lambda s, t: (t, s)), st_spec, st_spec],
        out_shape=[jax.ShapeDtypeStruct((rows, S5_WIDTH), BF16), st_shape, st_shape],
        scratch_shapes=_s5_scratch(tb, 2),
        compiler_params=_cparams(("parallel", "arbitrary")),
    )(proj, bd_re, bd_im, cd_re, cd_im, a_re, a_im, d)


def _s5_bwd(proj, dy, st_re, st_im, bd_re, bd_im, cd_re, cd_im, a_re, a_im, d, name):
    rows = proj.shape[0]
    tb = min(S5_TB, rows)
    nt = rows // tb
    sp = _s5_specs(tb, 0)
    rev = lambda t: nt - 1 - t

    def body(u_ref, dy_ref, sr_ref, si_ref, bdr, bdi, cdr, cdi, ar_ref, ai_ref, d_ref,
             du_ref, gbr, gbi, gcr, gci, gar, gai, gd_ref, lr_s, li_s, tr_s, ti_s, xr_s, xi_s, pr_s, pi_s):
        t = pl.program_id(1)

        @pl.when(t == 0)
        def _():
            lr_s[...] = jnp.zeros_like(lr_s)
            li_s[...] = jnp.zeros_like(li_s)
            for r in (gbr, gbi, gcr, gci, gar, gai, gd_ref):
                r[...] = jnp.zeros_like(r)

        u = u_ref[...]
        ar, ai = ar_ref[...], ai_ref[...]
        cr, ci = sr_ref[...], si_ref[...]
        _s5_states(u, bdr[...], bdi[...], ar, ai, cr, ci, xr_s, xi_s, pr_s, pi_s, tb)
        xr, xi = _from_groups(xr_s), _from_groups(xi_s)
        xrb, xib = xr.astype(BF16), xi.astype(BF16)
        ypre = _dot(xrb, cdr[...], _NN) - _dot(xib, cdi[...], _NN) + d_ref[...] * u
        dyp = dy_ref[...] * _gelu_grad(ypre)
        dypb = dyp.astype(BF16)
        gd_ref[...] += jnp.sum(dyp * u, axis=0, keepdims=True)
        gcr[...] += _dot(xrb, dypb, _TN)
        gci[...] -= _dot(xib, dypb, _TN)
        _to_groups(tr_s, _dot(dypb, cdr[...], _NT))
        _to_groups(ti_s, -_dot(dypb, cdi[...], _NT))
        nr, ni = _seg_scan(tr_s, ti_s, ar, -ai, lr_s[...], li_s[...], pr_s, pi_s, tb, reverse=True)
        lr_s[...] = nr
        li_s[...] = ni
        lr, li = _from_groups(tr_s), _from_groups(ti_s)
        row = lax.broadcasted_iota(jnp.int32, (tb, 1), 0)
        lrb, lib = lr.astype(BF16), li.astype(BF16)
        du_ref[...] = _dot(lrb, bdr[...], _NT) + _dot(lib, bdi[...], _NT) + d_ref[...] * dyp
        ub = u.astype(BF16)
        gbr[...] += _dot(ub, lrb, _TN)
        gbi[...] += _dot(ub, lib, _TN)
        xpr = jnp.where(row == 0, cr, pltpu.roll(xr, 1, 0))
        xpi = jnp.where(row == 0, ci, pltpu.roll(xi, 1, 0))
        gar[...] += jnp.sum(lr * xpr + li * xpi, axis=0, keepdims=True)
        gai[...] += jnp.sum(li * xpr - lr * xpi, axis=0, keepdims=True)

    st_spec = pl.BlockSpec((None, None, 1, S5_SLAB_STATE), lambda s, t: (s, rev(t), 0, 0))
    slab = lambda r, c: pl.BlockSpec((None, r, c), lambda s, t: (s, 0, 0))
    return pl.pallas_call(
        body, name=name, grid=(S5_SLABS, nt),
        in_specs=[sp["u"](rev), pl.BlockSpec((tb, LANES), lambda s, t: (rev(t), s)), st_spec, st_spec,
                  sp["bd"], sp["bd"], sp["cd"], sp["cd"], sp["a"], sp["a"], sp["d"]],
        out_specs=[pl.BlockSpec((tb, LANES), lambda s, t: (rev(t), s)),
                   slab(LANES, S5_SLAB_STATE), slab(LANES, S5_SLAB_STATE),
                   slab(S5_SLAB_STATE, LANES), slab(S5_SLAB_STATE, LANES),
                   slab(1, S5_SLAB_STATE), slab(1, S5_SLAB_STATE),
                   pl.BlockSpec((1, LANES), lambda s, t: (0, s))],
        out_shape=[jax.ShapeDtypeStruct((rows, S5_WIDTH), F32),
                   jax.ShapeDtypeStruct((S5_SLABS, LANES, S5_SLAB_STATE), F32),
                   jax.ShapeDtypeStruct((S5_SLABS, LANES, S5_SLAB_STATE), F32),
                   jax.ShapeDtypeStruct((S5_SLABS, S5_SLAB_STATE, LANES), F32),
                   jax.ShapeDtypeStruct((S5_SLABS, S5_SLAB_STATE, LANES), F32),
                   jax.ShapeDtypeStruct((S5_SLABS, 1, S5_SLAB_STATE), F32),
                   jax.ShapeDtypeStruct((S5_SLABS, 1, S5_SLAB_STATE), F32),
                   jax.ShapeDtypeStruct((1, S5_WIDTH), F32)],
        scratch_shapes=_s5_scratch(tb, 4),
        compiler_params=_cparams(("parallel", "arbitrary")),
    )(proj, dy, st_re, st_im, bd_re, bd_im, cd_re, cd_im, a_re, a_im, d)


@functools.partial(jax.custom_vjp, nondiff_argnums=(2,))
def _bdot(a, b, dims):
    return _dot(a.astype(BF16), b.astype(BF16), dims)


def _bdot_fwd(a, b, dims):
    return _bdot(a, b, dims), (a, b)


def _bdot_bwd(dims, res, ct):
    a, b = res
    if dims == _NN:
        return _bdot(ct, b, _NT), _bdot(a, ct, _TN)
    if dims == _NT:
        return _bdot(ct, b, _NN), _bdot(ct, a, _TN)
    return _bdot(b, ct, _NT), _bdot(a, ct, _NN)


_bdot.defvjp(_bdot_fwd, _bdot_bwd)


def _split_bf16(a):
    hi = a.astype(BF16)
    return hi, (a - hi.astype(F32)).astype(BF16)


@functools.partial(jax.custom_vjp, nondiff_argnums=(2,))
def _dot3(a, b, dims):
    ah, al = _split_bf16(a)
    bh, bl = _split_bf16(b)
    return _dot(ah, bh, dims) + (_dot(ah, bl, dims) + _dot(al, bh, dims))


def _dot3_fwd(a, b, dims):
    return _dot3(a, b, dims), (a, b)


def _dot3_bwd(dims, res, ct):
    a, b = res
    if dims == _NN:
        return _dot3(ct, b, _NT), _dot3(a, ct, _TN)
    if dims == _NT:
        return _dot3(ct, b, _NN), _dot3(ct, a, _TN)
    return _dot3(b, ct, _NT), _dot3(a, ct, _NN)


_dot3.defvjp(_dot3_fwd, _dot3_bwd)


def _tril_ones(c):
    r = lax.broadcasted_iota(jnp.int32, (c, c), 0)
    col = lax.broadcasted_iota(jnp.int32, (c, c), 1)
    return jnp.where(r >= col, 1.0, 0.0).astype(BF16)


@jax.custom_vjp
def _chunk_cumsum(x):
    xh, xl = _split_bf16(x)
    t = _tril_ones(x.shape[0])
    return _dot(t, xh, _NN) + _dot(t, xl, _NN)


def _chunk_cumsum_fwd(x):
    return _chunk_cumsum(x), None


def _chunk_cumsum_bwd(_, ct):
    ch, cl = _split_bf16(ct)
    t = _tril_ones(ct.shape[0])
    return (_dot(t, ch, _TN) + _dot(t, cl, _TN),)


_chunk_cumsum.defvjp(_chunk_cumsum_fwd, _chunk_cumsum_bwd)


def _unit_lower_inverses(lms, n):
    r = lax.broadcasted_iota(jnp.int32, (n, n), 0)
    c = lax.broadcasted_iota(jnp.int32, (n, n), 1)
    eye = jnp.where(r == c, 1.0, 0.0)
    ps = [eye - lm for lm in lms]
    powers = list(lms)
    steps = int(math.log2(n)) - 1
    for _ in range(steps):
        powers = [_dot3(x, x, _NN) for x in powers]
        ps = [p + _dot3(p, x, _NN) for p, x in zip(ps, powers)]
    return ps


def _dn_chunk(qs, ks, vs, gs, bs, ss):
    c = qs[0].shape[0]
    r = lax.broadcasted_iota(jnp.int32, (c, c), 0)
    col = lax.broadcasted_iota(jnp.int32, (c, c), 1)
    tril = r >= col
    strict = r > col
    gls = [jnp.broadcast_to(g, (c, LANES)) for g in gs]
    gcs = [_chunk_cumsum(gl) for gl in gls]
    gtots = [jnp.sum(gl, axis=0, keepdims=True) for gl in gls]
    gdiffs = [_chunk_cumsum(jnp.where(strict, jnp.broadcast_to(g, (c, c)), 0.0)) for g in gs]
    decays = [jnp.where(tril, jnp.exp(jnp.where(tril, gd, 0.0)), 0.0) for gd in gdiffs]
    kbs = [k * b for k, b in zip(ks, bs)]
    vbs = [v * b for v, b in zip(vs, bs)]
    lmats = [jnp.where(strict, _bdot(kb, k, _NT) * d, 0.0) for kb, k, d in zip(kbs, ks, decays)]
    attns = [jnp.where(tril, _bdot(q, k, _NT) * d, 0.0) for q, k, d in zip(qs, ks, decays)]
    tinvs = _unit_lower_inverses(lmats, c)
    us = [_dot3(t, vb, _NN) for t, vb in zip(tinvs, vbs)]
    ws = [_dot3(t, kb * jnp.exp(gc), _NN) for t, kb, gc in zip(tinvs, kbs, gcs)]
    ws_s = [_bdot(w, s, _NN) for w, s in zip(ws, ss)]
    qs_s = [_bdot(q * jnp.exp(gc), s, _NN) for q, gc, s in zip(qs, gcs, ss)]
    v_news = [u - x for u, x in zip(us, ws_s)]
    os_ = [x + _bdot(a, vn, _NN) for x, a, vn in zip(qs_s, attns, v_news)]
    s_news = [s * jnp.exp(gt) + _bdot(k * jnp.exp(gt - gc), vn, _TN)
              for s, gt, k, gc, vn in zip(ss, gtots, ks, gcs, v_news)]
    return tuple(os_), tuple(s_news)


def _dn_specs(tb, hb, tmap):
    groups = DN_HEADS // hb
    blk = lambda part: pl.BlockSpec((tb, hb * LANES), lambda hg, t: (tmap(t), hg + part * groups))
    colv = pl.BlockSpec((hb, tb, 1), lambda hg, t: (hg, tmap(t), 0))
    st = pl.BlockSpec((hb, tb // DN_CHUNK, DN_DK, DN_DK), lambda hg, t: (hg, tmap(t), 0, 0))
    return blk, colv, st


def _dn_fwd(qkv, gcol, bcol, name, hb=DN_HEADS):
    rows = qkv.shape[0]
    tb = min(DN_TB, rows)
    nt = rows // tb
    nch = tb // DN_CHUNK
    blk, colv, st = _dn_specs(tb, hb, lambda t: t)

    def body(q_ref, k_ref, v_ref, g_ref, b_ref, o_ref, st_ref, s_scr):
        @pl.when(pl.program_id(1) == 0)
        def _():
            s_scr[...] = jnp.zeros_like(s_scr)

        def chunk(ci, carry):
            rs = pl.ds(pl.multiple_of(ci * DN_CHUNK, DN_CHUNK), DN_CHUNK)
            cols = [slice(j * LANES, (j + 1) * LANES) for j in range(hb)]
            s_in = tuple(s_scr[j] for j in range(hb))
            for j in range(hb):
                st_ref[j, ci] = s_in[j]
            os_, s_new = _dn_chunk(tuple(q_ref[rs, cs] for cs in cols), tuple(k_ref[rs, cs] for cs in cols),
                                   tuple(v_ref[rs, cs] for cs in cols), tuple(g_ref[j, rs, :] for j in range(hb)),
                                   tuple(b_ref[j, rs, :] for j in range(hb)), s_in)
            for j in range(hb):
                o_ref[rs, cols[j]] = os_[j]
                s_scr[j] = s_new[j]
            return carry

        lax.fori_loop(0, nch, chunk, 0)

    return pl.pallas_call(
        body, name=name, grid=(DN_HEADS // hb, nt),
        in_specs=[blk(0), blk(1), blk(2), colv, colv],
        out_specs=[blk(0), st],
        out_shape=[jax.ShapeDtypeStruct((rows, DN_HEADS * DN_DK), F32),
                   jax.ShapeDtypeStruct((DN_HEADS, rows // DN_CHUNK, DN_DK, DN_DK), F32)],
        scratch_shapes=[pltpu.VMEM((hb, DN_DK, DN_DK), F32)],
        compiler_params=_cparams(("parallel", "arbitrary")),
    )(qkv, qkv, qkv, gcol, bcol)


def _dn_bwd(qkv, gcol, bcol, states, do, name, hb=DN_HEADS):
    rows = qkv.shape[0]
    tb = min(DN_TB // 2, rows)
    nt = rows // tb
    nch = tb // DN_CHUNK
    blk, colv, st = _dn_specs(tb, hb, lambda t: nt - 1 - t)

    def body(q_ref, k_ref, v_ref, g_ref, b_ref, st_ref, do_ref, dq_ref, dk_ref, dv_ref, dg_ref, db_ref, ds_scr):
        @pl.when(pl.program_id(1) == 0)
        def _():
            ds_scr[...] = jnp.zeros_like(ds_scr)

        def chunk(cj, carry):
            ci = nch - 1 - cj
            rs = pl.ds(pl.multiple_of(ci * DN_CHUNK, DN_CHUNK), DN_CHUNK)
            cols = [slice(j * LANES, (j + 1) * LANES) for j in range(hb)]
            heads = range(hb)
            args = (tuple(q_ref[rs, cs] for cs in cols), tuple(k_ref[rs, cs] for cs in cols),
                    tuple(v_ref[rs, cs] for cs in cols), tuple(g_ref[j, rs, :] for j in heads),
                    tuple(b_ref[j, rs, :] for j in heads), tuple(st_ref[j, ci] for j in heads))
            _, vjp = jax.vjp(_dn_chunk, *args)
            dq, dk, dv, dg, db, ds = vjp((tuple(do_ref[rs, cs] for cs in cols), tuple(ds_scr[j] for j in heads)))
            for j in heads:
                dq_ref[rs, cols[j]] = dq[j]
                dk_ref[rs, cols[j]] = dk[j]
                dv_ref[rs, cols[j]] = dv[j]
                dg_ref[j, rs, :] = dg[j]
                db_ref[j, rs, :] = db[j]
                ds_scr[j] = ds[j]
            return carry

        lax.fori_loop(0, nch, chunk, 0)

    wide = jax.ShapeDtypeStruct((rows, DN_HEADS * DN_DK), F32)
    narrow = jax.ShapeDtypeStruct((DN_HEADS, rows, 1), F32)
    return pl.pallas_call(
        body, name=name, grid=(DN_HEADS // hb, nt),
        in_specs=[blk(0), blk(1), blk(2), colv, colv, st, blk(0)],
        out_specs=[blk(0), blk(0), blk(0), colv, colv],
        out_shape=[wide, wide, wide, narrow, narrow],
        scratch_shapes=[pltpu.VMEM((hb, DN_DK, DN_DK), F32)],
        compiler_params=_cparams(("parallel", "arbitrary")),
    )(qkv, qkv, qkv, gcol, bcol, states, do)


def _final_loss(x, w, target, name, tb=256):
    rows, width = x.shape
    tb = min(tb, rows)

    def body(x_ref, w_ref, t_ref, dx_ref, dw_ref, loss_ref):
        i = pl.program_id(0)
        (y,), vjp = jax.vjp(_f_rms, x_ref[...], w_ref[...])
        err = y - t_ref[...]
        part = 0.5 * jnp.sum(jnp.mean(err * err, axis=-1, keepdims=True), axis=0, keepdims=True)
        dx, dw = vjp((err * (1.0 / width),))
        dx_ref[...] = dx

        @pl.when(i == 0)
        def _():
            dw_ref[...] = dw
            loss_ref[...] = jnp.broadcast_to(part, loss_ref.shape)

        @pl.when(i > 0)
        def _():
            dw_ref[...] += dw
            loss_ref[...] += jnp.broadcast_to(part, loss_ref.shape)

    row = pl.BlockSpec((tb, width), lambda i: (i, 0))
    par = pl.BlockSpec((1, width), lambda i: (0, 0))
    return pl.pallas_call(
        body, name=name, grid=(rows // tb,), in_specs=[row, par, row],
        out_specs=[row, par, pl.BlockSpec((SUBLANES, LANES), lambda i: (0, 0))],
        out_shape=[jax.ShapeDtypeStruct((rows, width), F32), jax.ShapeDtypeStruct((1, width), F32),
                   jax.ShapeDtypeStruct((SUBLANES, LANES), F32)],
        compiler_params=_cparams(("arbitrary",)),
    )(x, w, target)


_ANY = pl.BlockSpec(memory_space=pl.ANY)
N_CHIPS = 4


def _mesh_place():
    x, y, c = lax.axis_index("x"), lax.axis_index("y"), lax.axis_index("c")
    other_chips = [(1 - x, y), (x, 1 - y), (1 - x, 1 - y)]
    return x, y, c, other_chips


def _remote(src, dst, send_sem, recv_sem, dev):
    return pltpu.make_async_remote_copy(src_ref=src, dst_ref=dst, send_sem=send_sem, recv_sem=recv_sem,
                                        device_id=dev, device_id_type=pl.DeviceIdType.MESH)


def _gathered_shape(shape, place):
    if place == "cols":
        return shape[:-1] + (N_DEV * shape[-1],)
    if place == "rows":
        return shape[:-2] + (N_DEV * shape[-2], shape[-1])
    return (N_DEV,) + tuple(shape)


def _all_gather(arrs, name, places=None):
    n = len(arrs)
    places = places or [None] * n

    def body(*refs):
        ins, outs = refs[:n], refs[n:2 * n]
        send_sems, recv_sems, loc_sems = refs[2 * n:]
        x, y, c, chips = _mesh_place()
        me, sibling = (x, y, c), (x, y, 1 - c)

        def slot(i, px, py, pc):
            p = 4 * px + 2 * py + pc
            if places[i] == "cols":
                width = arrs[i].shape[-1]
                return outs[i].at[:, :, pl.ds(pl.multiple_of(p * width, LANES), width)]
            if places[i] == "rows":
                height = arrs[i].shape[-2]
                return outs[i].at[:, pl.ds(pl.multiple_of(p * height, SUBLANES), height), :]
            return outs[i].at[p]

        local = []
        for i in range(n):
            cp = pltpu.make_async_copy(ins[i], slot(i, *me), loc_sems.at[i])
            cp.start()
            local.append(cp)
        sends = []
        for i in range(n):
            cp = _remote(ins[i], slot(i, *me), send_sems.at[i, 0], recv_sems.at[i, 0], sibling)
            cp.start()
            sends.append(cp)
            for j, chip in enumerate(chips):
                cp = _remote(ins[i], slot(i, *me), send_sems.at[i, 1 + j], recv_sems.at[i, 1 + j], (*chip, c))
                cp.start()
                sends.append(cp)
        for j, chip in enumerate(chips):
            for i in range(n):
                got = slot(i, *chip, c)
                _remote(got, got, send_sems.at[i, 1 + j], recv_sems.at[i, 1 + j], (*chip, c)).wait_recv()
                cp = _remote(got, got, send_sems.at[i, 4 + j], recv_sems.at[i, 4 + j], sibling)
                cp.start()
                sends.append(cp)
        for i in range(n):
            got = slot(i, *sibling)
            _remote(got, got, send_sems.at[i, 0], recv_sems.at[i, 0], sibling).wait_recv()
            for j, chip in enumerate(chips):
                got = slot(i, *chip, 1 - c)
                _remote(got, got, send_sems.at[i, 4 + j], recv_sems.at[i, 4 + j], sibling).wait_recv()
        for cp in sends:
            cp.wait_send()
        for cp in local:
            cp.wait()

    return pl.pallas_call(
        body, name=name, in_specs=[_ANY] * n, out_specs=[_ANY] * n,
        out_shape=[jax.ShapeDtypeStruct(_gathered_shape(a.shape, p), a.dtype) for a, p in zip(arrs, places)],
        scratch_shapes=[pltpu.SemaphoreType.DMA((n, N_DEV - 1)), pltpu.SemaphoreType.DMA((n, N_DEV - 1)),
                        pltpu.SemaphoreType.DMA((n,))],
    )(*arrs)


def _sibling_swap(arrs, name):
    n = len(arrs)

    def body(*refs):
        ins, outs = refs[:n], refs[n:2 * n]
        send_sems, recv_sems = refs[2 * n:]
        x, y, c, _ = _mesh_place()
        sibling = (x, y, 1 - c)
        cps = [_remote(ins[i].at[1 - c], outs[i], send_sems.at[i], recv_sems.at[i], sibling) for i in range(n)]
        for cp in cps:
            cp.start()
        for cp in cps:
            cp.wait()

    return pl.pallas_call(
        body, name=name, in_specs=[_ANY] * n, out_specs=[_ANY] * n,
        out_shape=[jax.ShapeDtypeStruct(a.shape[1:], a.dtype) for a in arrs],
        scratch_shapes=[pltpu.SemaphoreType.DMA((n,)), pltpu.SemaphoreType.DMA((n,))],
    )(*arrs)


def _chip_scatter(arrs, name):
    n = len(arrs)

    def body(*refs):
        ins, outs = refs[:n], refs[n:2 * n]
        send_sems, recv_sems, loc_sems = refs[2 * n:]
        x, y, c, chips = _mesh_place()
        mine = 2 * x + y
        local = []
        for i in range(n):
            cp = pltpu.make_async_copy(ins[i].at[mine], outs[i].at[mine], loc_sems.at[i])
            cp.start()
            local.append(cp)
        sends = []
        for i in range(n):
            for j, (px, py) in enumerate(chips):
                cp = _remote(ins[i].at[2 * px + py], outs[i].at[mine], send_sems.at[i, j], recv_sems.at[i, j],
                             (px, py, c))
                cp.start()
                sends.append(cp)
        for i in range(n):
            for j, (px, py) in enumerate(chips):
                got = outs[i].at[2 * px + py]
                _remote(got, got, send_sems.at[i, j], recv_sems.at[i, j], (px, py, c)).wait_recv()
        for cp in sends:
            cp.wait_send()
        for cp in local:
            cp.wait()

    return pl.pallas_call(
        body, name=name, in_specs=[_ANY] * n, out_specs=[_ANY] * n,
        out_shape=[jax.ShapeDtypeStruct(a.shape, a.dtype) for a in arrs],
        scratch_shapes=[pltpu.SemaphoreType.DMA((n, N_CHIPS - 1)), pltpu.SemaphoreType.DMA((n, N_CHIPS - 1)),
                        pltpu.SemaphoreType.DMA((n,))],
    )(*arrs)


def _pair_sum(own, got, name):
    _, _, rows, cols = own.shape
    tb = _row_tile(rows, cols, budget=512 * 1024)

    def body(c_ref, a_ref, b_ref, o_ref):
        o_ref[...] = (a_ref[...].astype(F32) + b_ref[...].astype(F32)).astype(o_ref.dtype)

    core = lax.axis_index("c").astype(jnp.int32).reshape(1)
    grid_spec = pltpu.PrefetchScalarGridSpec(
        num_scalar_prefetch=1, grid=(N_CHIPS, rows // tb),
        in_specs=[pl.BlockSpec((None, None, tb, cols), lambda p, i, c_ref: (c_ref[0], p, i, 0)),
                  pl.BlockSpec((None, tb, cols), lambda p, i, c_ref: (p, i, 0))],
        out_specs=pl.BlockSpec((None, tb, cols), lambda p, i, c_ref: (p, i, 0)))
    return pl.pallas_call(
        body, name=name, grid_spec=grid_spec, out_shape=jax.ShapeDtypeStruct(got.shape, got.dtype),
        compiler_params=_cparams(("parallel", "parallel")),
    )(core, own, got)


def _adamw_math(w, g, m, v):
    m = ADAM_B1 * m + (1.0 - ADAM_B1) * g
    v = ADAM_B2 * v + (1.0 - ADAM_B2) * (g * g)
    m_hat = m / (1.0 - ADAM_B1 ** ADAM_STEP)
    v_hat = v / (1.0 - ADAM_B2 ** ADAM_STEP)
    delta = -ADAM_LR * (m_hat / (jnp.sqrt(v_hat) + ADAM_EPS) + ADAM_WD * w)
    return delta, m, v


def _row_tile(rows, cols, budget=128 * 1024):
    if rows * cols <= budget or rows % SUBLANES:
        return rows
    best = SUBLANES
    for t in range(SUBLANES, rows + 1, SUBLANES):
        if rows % t == 0 and t * cols <= budget:
            best = t
    return best


def _reduce_adamw(parts, w, m, v, name):
    rows, cols = w.shape
    nparts = parts.shape[0]
    tb = _row_tile(rows, cols)

    def body(p_ref, w_ref, m_ref, v_ref, g_ref, d_ref, nm_ref, nv_ref):
        g = p_ref[0].astype(F32)
        for s in range(1, nparts):
            g = g + p_ref[s].astype(F32)
        delta, nm, nv = _adamw_math(w_ref[...], g, m_ref[...], v_ref[...])
        g_ref[...] = g
        d_ref[...] = delta
        nm_ref[...] = nm
        nv_ref[...] = nv

    spec = pl.BlockSpec((tb, cols), lambda i: (i, 0))
    shp = jax.ShapeDtypeStruct((rows, cols), F32)
    return pl.pallas_call(
        body, name=name, grid=(rows // tb,),
        in_specs=[pl.BlockSpec((nparts, tb, cols), lambda i: (0, i, 0)), spec, spec, spec],
        out_specs=[spec] * 4, out_shape=[shp] * 4,
        compiler_params=_cparams(("parallel",)),
    )(parts, w, m, v)


def _sum_parts(parts, name):
    nparts, rows, cols = parts.shape
    tb = _row_tile(rows, cols)

    def body(p_ref, g_ref):
        g = p_ref[0]
        for s in range(1, nparts):
            g = g + p_ref[s]
        g_ref[...] = g

    return pl.pallas_call(
        body, name=name, grid=(rows // tb,),
        in_specs=[pl.BlockSpec((nparts, tb, cols), lambda i: (0, i, 0))],
        out_specs=pl.BlockSpec((tb, cols), lambda i: (i, 0)),
        out_shape=jax.ShapeDtypeStruct((rows, cols), F32),
        compiler_params=_cparams(("parallel",)),
    )(parts)


def _adamw(g, w, m, v, name):
    rows, cols = w.shape
    tb = _row_tile(rows, cols)

    def body(g_ref, w_ref, m_ref, v_ref, d_ref, nm_ref, nv_ref):
        delta, nm, nv = _adamw_math(w_ref[...], g_ref[...], m_ref[...], v_ref[...])
        d_ref[...] = delta
        nm_ref[...] = nm
        nv_ref[...] = nv

    spec = pl.BlockSpec((tb, cols), lambda i: (i, 0))
    shp = jax.ShapeDtypeStruct((rows, cols), F32)
    return pl.pallas_call(
        body, name=name, grid=(rows // tb,), in_specs=[spec] * 4, out_specs=[spec] * 3, out_shape=[shp] * 3,
        compiler_params=_cparams(("parallel",)),
    )(g, w, m, v)


def _pack(arrs):
    flat = jnp.concatenate([a.reshape(-1) for a in arrs])
    pad = (-flat.shape[0]) % (PACK_ROWS * LANES)
    return jnp.pad(flat, (0, pad)).reshape(-1, LANES)


def _unpack(packed, shapes):
    flat = packed.reshape(-1)
    out, off = [], 0
    for s in shapes:
        n = math.prod(s)
        out.append(flat[off:off + n].reshape(s))
        off += n
    return out


def _block_diag_b(bt):
    bb = bt.transpose(1, 0, 2).reshape(S5_SLABS, 8, S5_GROUP, S5_STATE)
    eye = jnp.eye(8, dtype=bt.dtype)
    return (bb[:, :, :, None, :] * eye[None, :, None, :, None]).reshape(S5_SLABS, LANES, S5_SLAB_STATE)


def _block_diag_b_grad(g):
    g5 = g.reshape(S5_SLABS, 8, S5_GROUP, 8, S5_STATE)
    diag = jnp.stack([g5[:, a, :, a, :] for a in range(8)], axis=1)
    return diag.reshape(S5_GROUPS, S5_GROUP, S5_STATE).transpose(1, 0, 2)


def _block_diag_c(cw):
    cc = cw.reshape(S5_SLABS, 8, S5_GROUP, S5_STATE).transpose(0, 1, 3, 2)
    eye = jnp.eye(8, dtype=cw.dtype)
    return (cc[:, :, :, None, :] * eye[None, :, None, :, None]).reshape(S5_SLABS, S5_SLAB_STATE, LANES)


def _block_diag_c_grad(g):
    g5 = g.reshape(S5_SLABS, 8, S5_STATE, 8, S5_GROUP)
    diag = jnp.stack([g5[:, a, :, a, :] for a in range(8)], axis=1)
    return diag.transpose(0, 1, 3, 2).reshape(S5_GROUPS, S5_GROUP, S5_STATE)


def _cols_full(gathered):
    _, k, n = gathered.shape
    return gathered.transpose(1, 0, 2).reshape(k, N_DEV * n)


def _cols_split(full):
    k, n8 = full.shape
    return full.reshape(k, N_DEV, n8 // N_DEV).transpose(1, 0, 2)


SMALL_NAMES = ("mix_norm_w", "s5_log_dt", "s5_a_re", "s5_a_im", "s5_b_re", "s5_b_im", "s5_c_re", "s5_c_im", "s5_d",
               "dn_a_log", "dn_dt_bias", "dn_norm_w", "ffn_norm_w", "final_norm_w")
CONV_NAMES = ("dn_conv_w", "ffn_conv_w")
BIG_NAMES = ("w_in", "s5_glu_w", "dn_proj_w", "w_out", "ffn_up", "ffn_down")
ROW_SHARDED = ("w_out", "ffn_down")
WEIGHT_ORDER = ("mix_norm_w", "w_in", "s5_log_dt", "s5_a_re", "s5_a_im", "s5_b_re", "s5_b_im", "s5_c_re", "s5_c_im",
                "s5_d", "s5_glu_w", "dn_conv_w", "dn_a_log", "dn_dt_bias", "dn_norm_w", "dn_proj_w", "w_out",
                "ffn_norm_w", "ffn_up", "ffn_conv_w", "ffn_down", "final_norm_w")


def _layer_forward(l, x, wts, sm):
    rows = x.shape[0]
    sv = {"x0": x}
    nm = f"l{l}_"
    mixw = sm["mix_norm_w"][l][None]
    (h,) = _tile_fwd(nm + "mix_norm", _f_rms, [(x, None, 0)], [(mixw, None, 0)], [(D_MODEL, None, BF16)], rows, 256)
    proj = _mm(h, wts["w_in_main"], "nn", F32, nm + "proj")
    ba = _mm(h, wts["w_in_ba"], "nn", F32, nm + "proj_ba")
    sv.update(h=h, proj=proj, ba=ba)
    disc = _s5_disc_fwd(sm["s5_log_dt"][l][:, None], sm["s5_a_re"][l], sm["s5_a_im"][l],
                        sm["s5_b_re"][l].transpose(2, 0, 1), sm["s5_b_im"][l].transpose(2, 0, 1), nm + "s5_disc")
    abar_re, abar_im, bbar_re, bbar_im = disc
    s5p = dict(
        bd_re=_block_diag_b(bbar_re).astype(BF16), bd_im=_block_diag_b(bbar_im).astype(BF16),
        cd_re=_block_diag_c(sm["s5_c_re"][l]).astype(BF16), cd_im=_block_diag_c(sm["s5_c_im"][l]).astype(BF16),
        a_re=abar_re.reshape(S5_SLABS, 1, S5_SLAB_STATE), a_im=abar_im.reshape(S5_SLABS, 1, S5_SLAB_STATE),
        d=sm["s5_d"][l][None])
    y_s5, st_re, st_im = _s5_fwd(proj, s5p["bd_re"], s5p["bd_im"], s5p["cd_re"], s5p["cd_im"],
                                 s5p["a_re"], s5p["a_im"], s5p["d"], nm + "s5_scan")
    glu = _mm(y_s5, wts["s5_glu_w"], "nn", F32, nm + "glu")
    sv.update(s5p=s5p, y_s5=y_s5, st_re=st_re, st_im=st_im, glu=glu)
    qkv = _dn_pre_fwd(proj, wts["dn_conv_w"], nm + "dn_pre")
    pad8 = lambda a: jnp.pad(a[None], ((0, 0), (DN_HEADS, LANES - 2 * DN_HEADS)))
    alog, dtb = pad8(sm["dn_a_log"][l]), pad8(sm["dn_dt_bias"][l])
    bg, = _tile_fwd(nm + "dn_gates", _f_dn_gates, [(ba, None, 0)], [(alog, None, 0), (dtb, None, 0)],
                    [(LANES, None, F32)], rows, 512)
    bcol = bg[:, 0:DN_HEADS].T[:, :, None]
    gcol = bg[:, DN_HEADS:2 * DN_HEADS].T[:, :, None]
    o, states = _dn_fwd(qkv, gcol, bcol, nm + "dn_chunk")
    dnw = sm["dn_norm_w"][l][None]
    y_dn, = _tile_fwd(nm + "dn_post", _f_dn_post, [(o, LANES, 0), (proj, LANES, MAIN_Z // LANES)], [(dnw, None, 0)],
                      [(1024, LANES, BF16)], rows, 512, ncol=8)
    br_dn = _mm(y_dn, wts["dn_proj_w"], "nn", F32, nm + "dn_proj")
    sv.update(qkv=qkv, alog=alog, dtb=dtb, bcol=bcol, gcol=gcol, o=o, states=states, dnw=dnw,
              y_dn=y_dn, br_dn=br_dn)
    cw = 512
    merged, = _tile_fwd(nm + "merge", _f_merge,
                        [(glu, cw, 0), (glu, cw, D_MODEL // cw), (br_dn, cw, 0),
                         (proj, cw, MAIN_GS // cw), (proj, cw, MAIN_GD // cw)], [],
                        [(D_MODEL, cw, BF16)], rows, 512, ncol=D_MODEL // cw)
    x1 = _mm(merged, wts["w_out"], "nn", F32, nm + "w_out", res=x)
    sv.update(merged=merged, x1=x1)
    ffw = sm["ffn_norm_w"][l][None]
    (h2,) = _tile_fwd(nm + "ffn_norm", _f_rms, [(x1, None, 0)], [(ffw, None, 0)], [(D_MODEL, None, BF16)], rows, 256)
    up = _mm(h2, wts["ffn_up"], "nn", F32, nm + "ffn_up")
    hid = _ffn_mix_fwd(up, wts["ffn_conv_w"], nm + "ffn_mix")
    x2 = _mm(hid, wts["ffn_down"], "nn", F32, nm + "ffn_down", res=x1, tk=1408)
    sv.update(h2=h2, up=up, hid=hid)
    return x2, sv


def _wgrad(name, act, dout, l, depth, bufs, call_name):
    kdim, ndim = act.shape[1], dout.shape[1]
    if name not in PLACED_NAMES:
        return _mm(act, dout, "tn", BF16, call_name, tm=1408)
    if name in ROW_SHARDED:
        k = kdim // N_DEV
        tn = min(1024, ndim)
        place = ((2, N_CHIPS, depth * k, ndim), (None, None, k, tn),
                 lambda i, j, kk: (i % 2, i // 2, l, j), bufs.get(name))
        return _mm(act, dout, "tn", BF16, call_name, tm=k, tn=tn, place=place)
    n = ndim // N_DEV
    tm = min(1024, kdim)
    per_layer = kdim // tm
    place = ((2, N_CHIPS, depth * kdim, n), (None, None, tm, n),
             lambda i, j, kk: (j % 2, j // 2, l * per_layer + i, 0), bufs.get(name))
    return _mm(act, dout, "tn", BF16, call_name, tm=tm, tn=n, place=place)


PLACED_NAMES = ("s5_glu_w", "dn_proj_w", "w_out", "ffn_up")


def _layer_backward(l, dx2, wts, sm, sv, depth, bufs):
    rows = dx2.shape[0]
    nm = f"l{l}_b_"
    cw = 512
    gr = {}
    dxb = dx2.astype(BF16)
    gr["ffn_down"] = _wgrad("ffn_down", sv["hid"], dxb, l, depth, bufs, nm + "ffn_down_w")
    dhid = _mm(dxb, wts["ffn_down"], "nt", F32, nm + "ffn_down_x", tn=1408)
    dup_a, dup_v, dw_a, dw_v = _ffn_mix_bwd(sv["up"], wts["ffn_conv_w"], dhid, nm + "ffn_mix")
    gr["ffn_conv_w"] = jnp.concatenate([dw_a[:FFN_CONV], dw_v[:FFN_CONV]], axis=1)
    dupb = jnp.concatenate([dup_a, dup_v], axis=1)
    gr["ffn_up"] = _wgrad("ffn_up", sv["h2"], dupb, l, depth, bufs, nm + "ffn_up_w")
    dh2 = _mm(dupb, wts["ffn_up"], "nt", F32, nm + "ffn_up_x", tk=2816)
    ffw = sm["ffn_norm_w"][l][None]
    (dx1,), (dffw,) = _tile_bwd(nm + "ffn_norm", _f_rms, [(sv["x1"], None, 0)], [(ffw, None, 0)],
                                [(dh2, None, 0)], [F32], rows, 256, add_first=dx2)
    gr["ffn_norm_w"] = dffw[0]
    dx1b = dx1.astype(BF16)
    gr["w_out"] = _wgrad("w_out", sv["merged"], dx1b, l, depth, bufs, nm + "w_out_w")
    dmerged = _mm(dx1b, wts["w_out"], "nt", F32, nm + "w_out_x")
    (dga, dgb, dbr, dgs, dgd), _ = _tile_bwd(
        nm + "merge", _f_merge,
        [(sv["glu"], cw, 0), (sv["glu"], cw, D_MODEL // cw), (sv["br_dn"], cw, 0),
         (sv["proj"], cw, MAIN_GS // cw), (sv["proj"], cw, MAIN_GD // cw)], [], [(dmerged, cw, 0)],
        [BF16, BF16, BF16, BF16, BF16], rows, 512, ncol=D_MODEL // cw)
    dglu = jnp.concatenate([dga, dgb], axis=1)
    gr["s5_glu_w"] = _wgrad("s5_glu_w", sv["y_s5"], dglu, l, depth, bufs, nm + "glu_w")
    dy_s5 = _mm(dglu, wts["s5_glu_w"], "nt", F32, nm + "glu_x")
    gr["dn_proj_w"] = _wgrad("dn_proj_w", sv["y_dn"], dbr, l, depth, bufs, nm + "dn_proj_w")
    dy_dn = _mm(dbr, wts["dn_proj_w"], "nt", F32, nm + "dn_proj_x")
    s5p = sv["s5p"]
    du, gbr, gbi, gcr, gci, gar, gai, gd = _s5_bwd(
        sv["proj"], dy_s5, sv["st_re"], sv["st_im"], s5p["bd_re"], s5p["bd_im"], s5p["cd_re"], s5p["cd_im"],
        s5p["a_re"], s5p["a_im"], s5p["d"], nm + "s5_scan")
    gr["s5_d"] = gd[0]
    gr["s5_c_re"] = _block_diag_c_grad(gcr)
    gr["s5_c_im"] = _block_diag_c_grad(gci)
    bt_re, bt_im = sm["s5_b_re"][l].transpose(2, 0, 1), sm["s5_b_im"][l].transpose(2, 0, 1)
    dldt, dare, daim, dbtr, dbti = _s5_disc_bwd(
        sm["s5_log_dt"][l][:, None], sm["s5_a_re"][l], sm["s5_a_im"][l], bt_re, bt_im,
        [gar.reshape(S5_GROUPS, S5_STATE), gai.reshape(S5_GROUPS, S5_STATE),
         _block_diag_b_grad(gbr), _block_diag_b_grad(gbi)], nm + "s5_disc")
    gr.update(s5_log_dt=dldt[:, 0], s5_a_re=dare, s5_a_im=daim,
              s5_b_re=dbtr.transpose(1, 2, 0), s5_b_im=dbti.transpose(1, 2, 0))
    (do, dz), (ddnw,) = _tile_bwd(nm + "dn_post", _f_dn_post,
                                  [(sv["o"], LANES, 0), (sv["proj"], LANES, MAIN_Z // LANES)],
                                  [(sv["dnw"], None, 0)], [(dy_dn, LANES, 0)], [F32, BF16], rows, 512, ncol=8)
    gr["dn_norm_w"] = ddnw[0]
    dq, dk, dv, dgc, dbc = _dn_bwd(sv["qkv"], sv["gcol"], sv["bcol"], sv["states"], do, nm + "dn_chunk")
    dbg = jnp.pad(jnp.concatenate([dbc[:, :, 0].T, dgc[:, :, 0].T], axis=1), ((0, 0), (0, LANES - 2 * DN_HEADS)))
    (dba,), (dalog, ddtb) = _tile_bwd(nm + "dn_gates", _f_dn_gates, [(sv["ba"], None, 0)],
                                      [(sv["alog"], None, 0), (sv["dtb"], None, 0)], [(dbg, None, 0)],
                                      [BF16], rows, 512)
    gr["dn_a_log"] = dalog[0, DN_HEADS:2 * DN_HEADS]
    gr["dn_dt_bias"] = ddtb[0, DN_HEADS:2 * DN_HEADS]
    dqkv, ddnconv = _dn_pre_bwd(sv["proj"], wts["dn_conv_w"], dq, dk, dv, nm + "dn_pre")
    gr["dn_conv_w"] = ddnconv[:DN_CONV]
    dproj = jnp.concatenate([du.astype(BF16), dqkv, dz, dgs, dgd], axis=1)
    gmain = _mm(sv["h"], dproj, "tn", BF16, nm + "proj_w")
    gba = _mm(sv["h"], dba, "tn", BF16, nm + "proj_ba_w", tk=1024)
    gr["w_in"] = jnp.concatenate([gmain[:, :OFF_BA], gba[:, :2 * DN_HEADS], gmain[:, OFF_BA:]], axis=1)
    dh = _mm(dproj, wts["w_in_main"], "nt", F32, nm + "proj_x", tk=2304)
    dh = _mm(dba, wts["w_in_ba"], "nt", F32, nm + "proj_ba_x", res=dh)
    mixw = sm["mix_norm_w"][l][None]
    (dx0,), (dmixw,) = _tile_bwd(nm + "mix_norm", _f_rms, [(sv["x0"], None, 0)], [(mixw, None, 0)],
                                 [(dh, None, 0)], [F32], rows, 256, add_first=dx1)
    gr["mix_norm_w"] = dmixw[0]
    return dx0, gr


def kernel(x, mix_norm_w, w_in, s5_log_dt, s5_a_re, s5_a_im, s5_b_re, s5_b_im, s5_c_re, s5_c_im, s5_d, s5_glu_w, dn_conv_w, dn_a_log, dn_dt_bias, dn_norm_w, dn_proj_w, w_out, ffn_norm_w, ffn_up, ffn_conv_w, ffn_down, final_norm_w, loss_target, m_mix_norm_w, m_w_in, m_s5_log_dt, m_s5_a_re, m_s5_a_im, m_s5_b_re, m_s5_b_im, m_s5_c_re, m_s5_c_im, m_s5_d, m_s5_glu_w, m_dn_conv_w, m_dn_a_log, m_dn_dt_bias, m_dn_norm_w, m_dn_proj_w, m_w_out, m_ffn_norm_w, m_ffn_up, m_ffn_conv_w, m_ffn_down, m_final_norm_w, v_mix_norm_w, v_w_in, v_s5_log_dt, v_s5_a_re, v_s5_a_im, v_s5_b_re, v_s5_b_im, v_s5_c_re, v_s5_c_im, v_s5_d, v_s5_glu_w, v_dn_conv_w, v_dn_a_log, v_dn_dt_bias, v_dn_norm_w, v_dn_proj_w, v_w_out, v_ffn_norm_w, v_ffn_up, v_ffn_conv_w, v_ffn_down, v_final_norm_w):
    w = dict(mix_norm_w=mix_norm_w, w_in=w_in, s5_log_dt=s5_log_dt, s5_a_re=s5_a_re, s5_a_im=s5_a_im, s5_b_re=s5_b_re, s5_b_im=s5_b_im, s5_c_re=s5_c_re, s5_c_im=s5_c_im, s5_d=s5_d, s5_glu_w=s5_glu_w, dn_conv_w=dn_conv_w, dn_a_log=dn_a_log, dn_dt_bias=dn_dt_bias, dn_norm_w=dn_norm_w, dn_proj_w=dn_proj_w, w_out=w_out, ffn_norm_w=ffn_norm_w, ffn_up=ffn_up, ffn_conv_w=ffn_conv_w, ffn_down=ffn_down, final_norm_w=final_norm_w)
    mo = dict(mix_norm_w=m_mix_norm_w, w_in=m_w_in, s5_log_dt=m_s5_log_dt, s5_a_re=m_s5_a_re, s5_a_im=m_s5_a_im, s5_b_re=m_s5_b_re, s5_b_im=m_s5_b_im, s5_c_re=m_s5_c_re, s5_c_im=m_s5_c_im, s5_d=m_s5_d, s5_glu_w=m_s5_glu_w, dn_conv_w=m_dn_conv_w, dn_a_log=m_dn_a_log, dn_dt_bias=m_dn_dt_bias, dn_norm_w=m_dn_norm_w, dn_proj_w=m_dn_proj_w, w_out=m_w_out, ffn_norm_w=m_ffn_norm_w, ffn_up=m_ffn_up, ffn_conv_w=m_ffn_conv_w, ffn_down=m_ffn_down, final_norm_w=m_final_norm_w)
    vo = dict(mix_norm_w=v_mix_norm_w, w_in=v_w_in, s5_log_dt=v_s5_log_dt, s5_a_re=v_s5_a_re, s5_a_im=v_s5_a_im, s5_b_re=v_s5_b_re, s5_b_im=v_s5_b_im, s5_c_re=v_s5_c_re, s5_c_im=v_s5_c_im, s5_d=v_s5_d, s5_glu_w=v_s5_glu_w, dn_conv_w=v_dn_conv_w, dn_a_log=v_dn_a_log, dn_dt_bias=v_dn_dt_bias, dn_norm_w=v_dn_norm_w, dn_proj_w=v_dn_proj_w, w_out=v_w_out, ffn_norm_w=v_ffn_norm_w, ffn_up=v_ffn_up, ffn_conv_w=v_ffn_conv_w, ffn_down=v_ffn_down, final_norm_w=v_final_norm_w)
    depth = w_in.shape[0]
    me = 4 * lax.axis_index("x") + 2 * lax.axis_index("y") + lax.axis_index("c")
    xs = x[0]
    target = loss_target[0]

    gather_names = BIG_NAMES + CONV_NAMES
    places = [None if n == "w_in" else ("rows" if n in ROW_SHARDED else "cols") for n in gather_names]
    gathered = _all_gather([w[n].astype(BF16) if n in BIG_NAMES else w[n] for n in gather_names], "gather_weights",
                           places)
    full = dict(zip(gather_names, gathered))
    g_in = full["w_in"]
    full["w_in"] = g_in.transpose(1, 2, 0, 3).reshape(depth, g_in.shape[2], -1)
    layer_w = []
    for l in range(depth):
        wi = full["w_in"][l]
        lw = {n: full[n][l] for n in gather_names if n != "w_in"}
        lw["w_in_main"] = jnp.concatenate([wi[:, :OFF_BA], wi[:, OFF_GS:]], axis=1)
        lw["w_in_ba"] = jnp.pad(wi[:, OFF_BA:OFF_GS], ((0, 0), (0, LANES - 2 * DN_HEADS)))
        layer_w.append(lw)

    saved = []
    h = xs
    for l in range(depth):
        h, sv = _layer_forward(l, h, layer_w[l], w)
        saved.append(sv)
    dx, dfinal, loss_tile = _final_loss(h, final_norm_w[None], target, "final_loss")

    grads = [None] * depth
    bufs = {}
    for l in reversed(range(depth)):
        dx, grads[l] = _layer_backward(l, dx, layer_w[l], w, saved[l], depth, bufs)
        bufs = {n: grads[l][n] for n in PLACED_NAMES}

    def stacked(n):
        return jnp.stack([grads[l][n] for l in range(depth)])

    big_send = []
    for n in BIG_NAMES:
        if n in PLACED_NAMES:
            big_send.append(bufs[n])
            continue
        g = stacked(n)
        if n in ROW_SHARDED:
            g = g.reshape(depth, N_CHIPS, 2, g.shape[1] // N_DEV, g.shape[2]).transpose(2, 1, 0, 3, 4)
        else:
            g = g.reshape(depth, g.shape[1], N_CHIPS, 2, g.shape[2] // N_DEV).transpose(3, 2, 0, 1, 4)
        big_send.append(g.reshape(2, N_CHIPS, -1, g.shape[-1]))
    from_sibling = _sibling_swap(big_send, "swap_grads")
    chip_partials = [_pair_sum(own, got, "pair_sum_" + n) for n, own, got in zip(BIG_NAMES, big_send, from_sibling)]
    big_recv = _chip_scatter(chip_partials, "scatter_grads")
    small_list = [stacked(n) for n in SMALL_NAMES if n != "final_norm_w"] + [dfinal[0]]
    small_list += [stacked(n) for n in CONV_NAMES] + [loss_tile[0, 0:1]]
    small_shapes = [a.shape for a in small_list]
    (small_recv,) = _all_gather([_pack(small_list)], "gather_small")
    small_sum = _unpack(_sum_parts(small_recv, "sum_small"), small_shapes)
    loss = small_sum[-1][0]
    small_names = [n for n in SMALL_NAMES if n != "final_norm_w"] + ["final_norm_w"]
    g_out = dict(zip(small_names, small_sum[:len(small_names)]))
    for n, gfull in zip(CONV_NAMES, small_sum[len(small_names):len(small_names) + 2]):
        shard = w[n].shape[-1]
        g_out[n] = lax.dynamic_slice_in_dim(gfull, me * shard, shard, axis=2)

    d_out, m_out, v_out = {}, {}, {}
    for n, parts in zip(BIG_NAMES, big_recv):
        shp = w[n].shape
        two = lambda a: a.reshape(-1, shp[-1])
        g2, d2, m2, v2 = _reduce_adamw(parts, two(w[n]), two(mo[n]), two(vo[n]), "adamw_" + n)
        g_out[n], d_out[n], m_out[n], v_out[n] = (a.reshape(shp) for a in (g2, d2, m2, v2))
    rest = list(small_names) + list(CONV_NAMES)
    rest_shapes = [w[n].shape for n in rest]
    d2, m2, v2 = _adamw(_pack([g_out[n] for n in rest]), _pack([w[n] for n in rest]), _pack([mo[n] for n in rest]),
                        _pack([vo[n] for n in rest]), "adamw_small")
    for n, d, m_, v_ in zip(rest, _unpack(d2, rest_shapes), _unpack(m2, rest_shapes), _unpack(v2, rest_shapes)):
        d_out[n], m_out[n], v_out[n] = d, m_, v_

    return (loss, dx[None], *[g_out[n] for n in WEIGHT_ORDER], *[d_out[n] for n in WEIGHT_ORDER],
            *[m_out[n] for n in WEIGHT_ORDER], *[v_out[n] for n in WEIGHT_ORDER])
```
